```python
import math
import jax, jax.numpy as jnp
from jax import lax
import numpy as np

D_MODEL = 1024
BATCH = 8
SEQ = 2048
DEPTH = 1

MEM_LEN = 256
SSM_HEAD_DIM = 64
SSM_HEADS = D_MODEL // SSM_HEAD_DIM
SSM_D_INNER = SSM_HEADS * SSM_HEAD_DIM
SSM_GROUPS = 2
SSM_STATE = 128
CONV_WIDTH = 4
CHUNK = 128
CONV_DIM = SSM_D_INNER + 2 * SSM_GROUPS * SSM_STATE
ATTN_HEAD_DIM = 64
ATTN_HEADS = D_MODEL // ATTN_HEAD_DIM
ATTN_WIDTH = ATTN_HEADS * ATTN_HEAD_DIM
Q_BLOCK = 128
MIX_WIDTH = SSM_D_INNER + ATTN_WIDTH
IN_COLS = 2 * SSM_D_INNER + 2 * SSM_GROUPS * SSM_STATE + SSM_HEADS + 3 * ATTN_WIDTH + ATTN_HEADS
XATTN_HEADS = 4
XATTN_HEAD_DIM = D_MODEL // XATTN_HEADS
D_FF = 4 * D_MODEL
EPS = 1e-5

kernel_name = "hymba_ssd_fox_memxattn_layer"


def rms_norm(u, g):
    uf = u.astype(jnp.float32)
    y = uf * lax.rsqrt(jnp.mean(uf * uf, axis=-1, keepdims=True) + EPS)
    return (y * g.astype(jnp.float32)).astype(u.dtype)


def segsum(a):
    T = a.shape[-1]
    x = jnp.broadcast_to(a[..., :, None], a.shape + (T,))
    x = jnp.where(jnp.tril(jnp.ones((T, T), dtype=bool), -1), x, 0.0)
    x = jnp.cumsum(x, axis=-2)
    return jnp.where(jnp.tril(jnp.ones((T, T), dtype=bool)), x, -jnp.inf)


def causal_depthwise_conv(u, w, b):
    c = u.shape[-1]
    out = lax.conv_general_dilated(
        u, w[:, None, :].astype(u.dtype), window_strides=(1,),
        padding=[(CONV_WIDTH - 1, 0)], dimension_numbers=("NWC", "WIO", "NWC"),
        feature_group_count=c)
    return out + b.astype(u.dtype)


def ssd_chunked(xh, dt, A, Bm, Cm):
    b, S, g, r, p = xh.shape
    n = Bm.shape[-1]
    c = S // CHUNK
    X = (xh * dt[..., None]).reshape(b, c, CHUNK, g, r, p)
    dA = (dt * A).reshape(b, c, CHUNK, g, r).transpose(0, 3, 4, 1, 2)
    Bc = Bm.reshape(b, c, CHUNK, g, n)
    Cc = Cm.reshape(b, c, CHUNK, g, n)
    A_cs = jnp.cumsum(dA, axis=-1)
    Lmat = jnp.exp(segsum(dA))
    CB = jnp.einsum("bclgn,bcsgn->bcgls", Cc, Bc)
    y_diag = jnp.einsum("bcgls,bgrcls,bcsgrp->bclgrp", CB, Lmat, X)
    decay_states = jnp.exp(A_cs[..., -1:] - A_cs)
    states = jnp.einsum("bclgn,bgrcl,bclgrp->bcgrpn", Bc, decay_states, X)
    states = jnp.concatenate([jnp.zeros_like(states[:, :1]), states], axis=1)
    A_last = jnp.pad(A_cs[..., -1], ((0, 0), (0, 0), (0, 0), (1, 0)))
    chunk_decay = jnp.exp(segsum(A_last))
    new_states = jnp.einsum("bgrzc,bcgrpn->bzgrpn", chunk_decay, states)
    states_in = new_states[:, :-1]
    y_off = jnp.einsum("bclgn,bcgrpn,bgrcl->bclgrp", Cc, states_in, jnp.exp(A_cs))
    return (y_diag + y_off).reshape(b, S, g, r, p)


def forgetting_attention(q, k, v, log_f):
    b, S, h, d = q.shape
    cum = jnp.cumsum(log_f, axis=1).transpose(0, 2, 1)
    scale = d ** -0.5
    outs = []
    for i in range(S // Q_BLOCK):
        qs, qe = i * Q_BLOCK, (i + 1) * Q_BLOCK
        s = jnp.einsum("bqhd,bkhd->bhqk", q[:, qs:qe], k[:, :qe]) * scale
        s = s + cum[:, :, qs:qe, None] - cum[:, :, None, :qe]
        mask = jnp.arange(qs, qe)[:, None] >= jnp.arange(qe)[None, :]
        s = jnp.where(mask, s, -jnp.inf)
        pr = jax.nn.softmax(s, axis=-1)
        outs.append(jnp.einsum("bhqk,bkhd->bqhd", pr, v[:, :qe]))
    return jnp.concatenate(outs, axis=1)


def parallel_mixer(h, w_in, conv_w, conv_b, dt_bias, a_log, d_skip, ssm_norm_w,
                   g_q, g_k, f_bias, w_out):
    b, S, _ = h.shape
    proj = h @ w_in
    sizes = [SSM_D_INNER, CONV_DIM, SSM_HEADS, ATTN_WIDTH, ATTN_WIDTH, ATTN_WIDTH]
    idx = list(np.cumsum(sizes))
    z, xbc, dt_raw, q, k, v, f_raw = jnp.split(proj, idx, axis=-1)
    xbc = jax.nn.silu(causal_depthwise_conv(xbc, conv_w, conv_b)).astype(jnp.float32)
    xs, Bm, Cm = jnp.split(xbc, [SSM_D_INNER, SSM_D_INNER + SSM_GROUPS * SSM_STATE], axis=-1)
    r = SSM_HEADS // SSM_GROUPS
    xs = xs.reshape(b, S, SSM_GROUPS, r, SSM_HEAD_DIM)
    Bm = Bm.reshape(b, S, SSM_GROUPS, SSM_STATE)
    Cm = Cm.reshape(b, S, SSM_GROUPS, SSM_STATE)
    dt = jax.nn.softplus(dt_raw.astype(jnp.float32) + dt_bias.astype(jnp.float32))
    dt = dt.reshape(b, S, SSM_GROUPS, r)
    A = -jnp.exp(a_log.astype(jnp.float32)).reshape(SSM_GROUPS, r)
    y = ssd_chunked(xs, dt, A, Bm, Cm)
    y = y + d_skip.astype(jnp.float32).reshape(SSM_GROUPS, r)[..., None] * xs
    y = y.reshape(b, S, SSM_D_INNER) * jax.nn.silu(z.astype(jnp.float32))
    y = y.reshape(b, S, SSM_GROUPS, SSM_D_INNER // SSM_GROUPS)
    y = y * lax.rsqrt(jnp.mean(y * y, axis=-1, keepdims=True) + EPS)
    y = y.reshape(b, S, SSM_D_INNER) * ssm_norm_w.astype(jnp.float32)
    q = rms_norm(q.astype(jnp.float32).reshape(b, S, ATTN_HEADS, ATTN_HEAD_DIM), g_q)
    k = rms_norm(k.astype(jnp.float32).reshape(b, S, ATTN_HEADS, ATTN_HEAD_DIM), g_k)
    v = v.astype(jnp.float32).reshape(b, S, ATTN_HEADS, ATTN_HEAD_DIM)
    log_f = jax.nn.log_sigmoid(f_raw.astype(jnp.float32) + f_bias.astype(jnp.float32))
    o = forgetting_attention(q, k, v, log_f).reshape(b, S, ATTN_WIDTH)
    mixed = jnp.concatenate([y, o], axis=-1).astype(h.dtype)
    return mixed @ w_out


def memory_cross_attention(h, mem_n, xq_w, xkv_w, xg_q, xg_k, xo_w):
    b, S, _ = h.shape
    q = (h @ xq_w).astype(jnp.float32).reshape(b, S, XATTN_HEADS, XATTN_HEAD_DIM)
    kv = (mem_n @ xkv_w).astype(jnp.float32)
    k, v = jnp.split(kv, 2, axis=-1)
    k = k.reshape(b, MEM_LEN, XATTN_HEADS, XATTN_HEAD_DIM)
    v = v.reshape(b, MEM_LEN, XATTN_HEADS, XATTN_HEAD_DIM)
    q = rms_norm(q, xg_q)
    k = rms_norm(k, xg_k)
    s = jnp.einsum("bqhd,bkhd->bhqk", q, k) * (XATTN_HEAD_DIM ** -0.5)
    pr = jax.nn.softmax(s, axis=-1)
    o = jnp.einsum("bhqk,bkhd->bqhd", pr, v).reshape(b, S, D_MODEL).astype(h.dtype)
    return o @ xo_w


def squared_relu_mlp(h, w_up, w_down):
    u = jax.nn.relu(h @ w_up)
    return (u * u) @ w_down


def setup_inputs(seed: int = 0) -> dict:
    key = jax.random.key(seed)
    ks = jax.random.split(key, 24)
    f32 = jnp.float32

    def nrm(k, shape, fan_in):
        return jax.random.normal(k, shape, f32) * (fan_in ** -0.5)

    def gain(k, shape):
        return 1.0 + 0.02 * jax.random.normal(k, shape, f32)

    dt0 = jnp.exp(jax.random.uniform(ks[6], (DEPTH, SSM_HEADS), f32,
                                     math.log(1e-3), math.log(1e-1)))
    dt_bias = dt0 + jnp.log(-jnp.expm1(-dt0))
    return {
        "x": jax.random.normal(ks[0], (BATCH, SEQ, D_MODEL), f32),
        "mem": jax.random.normal(ks[1], (BATCH, MEM_LEN, D_MODEL), f32),
        "g_mix": gain(ks[2], (DEPTH, D_MODEL)),
        "w_in": nrm(ks[3], (DEPTH, D_MODEL, IN_COLS), D_MODEL),
        "conv_w": nrm(ks[4], (DEPTH, CONV_WIDTH, CONV_DIM), CONV_WIDTH),
        "conv_b": 0.02 * jax.random.normal(ks[5], (DEPTH, CONV_DIM), f32),
        "dt_bias": dt_bias,
        "a_log": jnp.log(jax.random.uniform(ks[7], (DEPTH, SSM_HEADS), f32, 1.0, 16.0)),
        "d_skip": gain(ks[8], (DEPTH, SSM_HEADS)),
        "ssm_norm_w": gain(ks[9], (DEPTH, SSM_D_INNER)),
        "g_q": gain(ks[10], (DEPTH, ATTN_HEAD_DIM)),
        "g_k": gain(ks[11], (DEPTH, ATTN_HEAD_DIM)),
        "f_bias": jax.random.uniform(ks[12], (DEPTH, ATTN_HEADS), f32, 2.0, 6.0),
        "w_out": nrm(ks[13], (DEPTH, MIX_WIDTH, D_MODEL), MIX_WIDTH),
        "g_xattn": gain(ks[14], (DEPTH, D_MODEL)),
        "g_mem": gain(ks[15], (DEPTH, D_MODEL)),
        "xq_w": nrm(ks[16], (DEPTH, D_MODEL, D_MODEL), D_MODEL),
        "xkv_w": nrm(ks[17], (DEPTH, D_MODEL, 2 * D_MODEL), D_MODEL),
        "xg_q": gain(ks[18], (DEPTH, XATTN_HEAD_DIM)),
        "xg_k": gain(ks[19], (DEPTH, XATTN_HEAD_DIM)),
        "xo_w": nrm(ks[20], (DEPTH, D_MODEL, D_MODEL), D_MODEL),
        "g_mlp": gain(ks[21], (DEPTH, D_MODEL)),
        "w_up": nrm(ks[22], (DEPTH, D_MODEL, D_FF), D_MODEL),
        "w_down": nrm(ks[23], (DEPTH, D_FF, D_MODEL), D_FF),
    }


def reference(x, mem, g_mix, w_in, conv_w, conv_b, dt_bias, a_log, d_skip, ssm_norm_w,
              g_q, g_k, f_bias, w_out, g_xattn, g_mem, xq_w, xkv_w, xg_q, xg_k, xo_w,
              g_mlp, w_up, w_down):
    for l in range(DEPTH):
        h = rms_norm(x, g_mix[l])
        x = x + parallel_mixer(h, w_in[l], conv_w[l], conv_b[l], dt_bias[l], a_log[l],
                               d_skip[l], ssm_norm_w[l], g_q[l], g_k[l], f_bias[l], w_out[l])
        h = rms_norm(x, g_xattn[l])
        mem_n = rms_norm(mem, g_mem[l])
        x = x + memory_cross_attention(h, mem_n, xq_w[l], xkv_w[l], xg_q[l], xg_k[l], xo_w[l])
        h = rms_norm(x, g_mlp[l])
        x = x + squared_relu_mlp(h, w_up[l], w_down[l])
    return x
```

```python
import functools

import numpy as np
import jax
import jax.numpy as jnp
from jax import lax
from jax.experimental import pallas as pl
from jax.experimental.pallas import tpu as pltpu

F32 = jnp.float32
BF16 = jnp.bfloat16

D_MODEL = 1024
BATCH = 8
SEQ = 2048
TOKENS = BATCH * SEQ
MEM_LEN = 256
SSM_HEAD_DIM = 64
SSM_HEADS = 16
SSM_D_INNER = 1024
SSM_GROUPS = 2
SSM_STATE = 128
CONV_WIDTH = 4
CHUNK = 128
ATTN_HEAD_DIM = 64
ATTN_HEADS = 16
ATTN_WIDTH = 1024
XATTN_HEADS = 4
XATTN_HEAD_DIM = 256
D_FF = 4096
EPS = 1e-5

LANES = 128
N_MAIN = 5632
HEAD_PAIRS = ATTN_HEADS // 2
GROUP_W = SSM_D_INNER // SSM_GROUPS
DT_LANE = 16
VMEM_LIMIT = 56 * 1024 * 1024
NEG_BIG = -1e30


def _rms(xf, g_row):
    ms = jnp.mean(xf * xf, axis=-1, keepdims=True)
    return xf * lax.rsqrt(ms + EPS) * g_row


def _split2(a):
    hi = a.astype(BF16)
    mid = (a - hi.astype(F32)).astype(BF16)
    return jnp.concatenate([hi, mid], axis=-1)


def _split3(a):
    hi = a.astype(BF16)
    r1 = a - hi.astype(F32)
    mid = r1.astype(BF16)
    lo = (r1 - mid.astype(F32)).astype(BF16)
    return jnp.concatenate([hi, mid, lo], axis=-1)


def _softplus(x):
    return jnp.maximum(x, 0.0) + jnp.log1p(jnp.exp(-jnp.abs(x)))


def _silu(x):
    return x * (1.0 / (1.0 + jnp.exp(-x)))


def _dot(a, b):
    return jnp.dot(a, b, preferred_element_type=F32)


def _dot_nt(a, b):
    return lax.dot_general(a, b, (((1,), (1,)), ((), ())), preferred_element_type=F32)


def _norm_mm_kernel(x_ref, g_ref, w_ref, o_ref, h_ref):
    @pl.when(pl.program_id(1) == 0)
    def _():
        h_ref[...] = _rms(x_ref[...], g_ref[...]).astype(BF16)

    o_ref[...] = _dot(h_ref[...], w_ref[...]).astype(o_ref.dtype)


def _norm_mm(x, g_row, w, tm, tn, out_dtype=F32):
    m, k = x.shape
    n = w.shape[1]
    return pl.pallas_call(
        _norm_mm_kernel,
        out_shape=jax.ShapeDtypeStruct((m, n), out_dtype),
        grid=(m // tm, n // tn),
        in_specs=[
            pl.BlockSpec((tm, k), lambda i, j: (i, 0)),
            pl.BlockSpec((1, k), lambda i, j: (0, 0)),
            pl.BlockSpec((k, tn), lambda i, j: (0, j)),
        ],
        out_specs=pl.BlockSpec((tm, tn), lambda i, j: (i, j)),
        scratch_shapes=[pltpu.VMEM((tm, k), BF16)],
        compiler_params=pltpu.CompilerParams(
            dimension_semantics=("parallel", "arbitrary"),
            vmem_limit_bytes=VMEM_LIMIT),
        name="norm_mm",
    )(x, g_row, w)


def _gate_route_matrix():
    r = np.zeros((3 * LANES, HEAD_PAIRS * LANES), np.float32)
    for h in range(ATTN_HEADS):
        for m in range(3):
            base = (h // 2) * LANES
            r[m * LANES + h, base + 16 * m + (h % 2)] = 1.0
            r[m * LANES + h, base + 48 + 16 * m + (h % 2)] = -1.0
    return r


def _gates_kernel(g_ref, fb_ref, r_ref, aux_ref):
    row = lax.broadcasted_iota(jnp.int32, (CHUNK, CHUNK), 0)
    col = lax.broadcasted_iota(jnp.int32, (CHUNK, CHUNK), 1)
    tril = jnp.where(row >= col, 1.0, 0.0).astype(BF16)
    fb = fb_ref[...]
    rmat = r_ref[...]

    def body(blk, carry):
        r0 = pl.multiple_of(blk * CHUNK, CHUNK)
        g = g_ref[pl.ds(r0, CHUNK), :]
        log_f = -_softplus(-(g + fb))
        part = _dot(tril, _split3(log_f))
        cum = part[:, :LANES] + part[:, LANES:2 * LANES] + part[:, 2 * LANES:] + carry
        routed = _dot(_split3(cum), rmat)
        for hp in range(HEAD_PAIRS):
            aux_ref[hp, pl.ds(r0, CHUNK), :] = routed[:, hp * LANES:(hp + 1) * LANES].astype(BF16)
        return cum[CHUNK - 1:CHUNK, :]

    lax.fori_loop(0, SEQ // CHUNK, body, jnp.zeros((1, LANES), F32))


def _gates(gates_raw, fb_row, rmat):
    return pl.pallas_call(
        _gates_kernel,
        out_shape=jax.ShapeDtypeStruct((BATCH, HEAD_PAIRS, SEQ, LANES), BF16),
        grid=(BATCH,),
        in_specs=[
            pl.BlockSpec((SEQ, LANES), lambda b: (b, 0)),
            pl.BlockSpec((1, LANES), lambda b: (0, 0)),
            pl.BlockSpec((3 * LANES, HEAD_PAIRS * LANES), lambda b: (0, 0)),
        ],
        out_specs=pl.BlockSpec((None, HEAD_PAIRS, SEQ, LANES), lambda b: (b, 0, 0, 0)),
        compiler_params=pltpu.CompilerParams(dimension_semantics=("parallel",)),
        name="gates",
    )(gates_raw, fb_row, rmat)


def _head_expand_matrix():
    e = np.zeros((2 * LANES, SSM_D_INNER), np.float32)
    for h in range(SSM_HEADS):
        e[DT_LANE + h, h * SSM_HEAD_DIM:(h + 1) * SSM_HEAD_DIM] = 1.0
        e[LANES + DT_LANE + h, h * SSM_HEAD_DIM:(h + 1) * SSM_HEAD_DIM] = 1.0
    return e


def _ssd_kernel(z_ref, xs_ref, bc_ref, g_ref, cwx_ref, cwb_ref, cbx_ref, cbb_ref,
                dtb_ref, alog_ref, ee_ref, dsk_ref, nw_ref, y_ref,
                xbuf_ref, bbuf_ref, state_ref):
    c = pl.program_id(1)

    @pl.when(c == 0)
    def _():
        xbuf_ref[0:8, :] = jnp.zeros((8, SSM_D_INNER), F32)
        bbuf_ref[0:8, :] = jnp.zeros((8, 2 * SSM_GROUPS * SSM_STATE), F32)
        state_ref[...] = jnp.zeros_like(state_ref)

    def conv_silu(u_ref, buf_ref, w_ref, b_ref):
        buf_ref[8:8 + CHUNK, :] = u_ref[...]
        acc = b_ref[...] + w_ref[3:4, :] * buf_ref[8:8 + CHUNK, :]
        for tap in range(3):
            acc = acc + w_ref[tap:tap + 1, :] * buf_ref[5 + tap:5 + tap + CHUNK, :]
        buf_ref[0:8, :] = buf_ref[CHUNK:CHUNK + 8, :]
        return _silu(acc)

    xs = conv_silu(xs_ref, xbuf_ref, cwx_ref, cbx_ref)
    bc = conv_silu(bc_ref, bbuf_ref, cwb_ref, cbb_ref)

    dt = _softplus(g_ref[...] + dtb_ref[...])
    da = dt * (-jnp.exp(alog_ref[...]))
    row = lax.broadcasted_iota(jnp.int32, (CHUNK, CHUNK), 0)
    col = lax.broadcasted_iota(jnp.int32, (CHUNK, CHUNK), 1)
    causal = row >= col
    tril = jnp.where(causal, 1.0, 0.0).astype(BF16)
    part = _dot(tril, _split3(da))
    acs = part[:, :LANES] + part[:, LANES:2 * LANES] + part[:, 2 * LANES:]
    a_last = acs[CHUNK - 1:CHUNK, :]
    exp_a = jnp.exp(acs)
    dt_decay = dt * jnp.exp(a_last - acs)
    ee = ee_ref[...]
    e_dtdec = _dot(_split2(dt_decay), ee)
    e_expa = _dot(_split2(exp_a), ee)
    acs_t = acs.T
    dt_t = dt.T

    xs_b = xs.astype(BF16)
    xdec_b = (xs * e_dtdec).astype(BF16)
    lane = lax.broadcasted_iota(jnp.int32, (CHUNK, LANES), 1)
    first_head = lane < SSM_HEAD_DIM

    y_parts = []
    for g in range(SSM_GROUPS):
        b_g = bc[:, g * SSM_STATE:(g + 1) * SSM_STATE]
        c_g = bc[:, (SSM_GROUPS + g) * SSM_STATE:(SSM_GROUPS + g + 1) * SSM_STATE]
        c_gb = c_g.astype(BF16)
        cb = _dot_nt(c_gb, b_g.astype(BF16))
        state = state_ref[g]
        cols = slice(g * GROUP_W, (g + 1) * GROUP_W)
        y_off = _dot(c_gb, state.astype(BF16)) * e_expa[:, cols]
        st_new = _dot(b_g.T.astype(BF16), xdec_b[:, cols])
        state_ref[g] = state * e_expa[CHUNK - 1:CHUNK, cols] + st_new
        diag = []
        for pair in range(GROUP_W // LANES):
            x_pair = xs_b[:, g * GROUP_W + pair * LANES:g * GROUP_W + (pair + 1) * LANES]
            res = []
            for j in range(2):
                h = g * (SSM_HEADS // SSM_GROUPS) + 2 * pair + j
                hl = DT_LANE + h
                a_col = jnp.broadcast_to(acs[:, hl:hl + 1], (CHUNK, CHUNK))
                seg = jnp.where(causal, a_col - acs_t[hl:hl + 1, :], -jnp.inf)
                m_h = cb * jnp.exp(seg) * dt_t[hl:hl + 1, :]
                res.append(_dot(m_h.astype(BF16), x_pair))
            diag.append(jnp.where(first_head, res[0], res[1]))
        y_parts.append(jnp.concatenate(diag, axis=-1) + y_off)
    y = jnp.concatenate(y_parts, axis=-1) + dsk_ref[...] * xs
    y = y * _silu(z_ref[...])
    normed = []
    for g in range(SSM_GROUPS):
        y_g = y[:, g * GROUP_W:(g + 1) * GROUP_W]
        normed.append(y_g * lax.rsqrt(jnp.mean(y_g * y_g, axis=-1, keepdims=True) + EPS))
    y_ref[...] = (jnp.concatenate(normed, axis=-1) * nw_ref[...]).astype(BF16)


def _ssd(proj, gates_raw, cw_x, cw_bc, cb_x, cb_bc, dtb_row, alog_row, ee, dsk_row, nw_row):
    nchunks = SEQ // CHUNK
    rowblk = lambda b, c: b * nchunks + c
    full = lambda shape: pl.BlockSpec(shape, lambda b, c: (0, 0))
    return pl.pallas_call(
        _ssd_kernel,
        out_shape=jax.ShapeDtypeStruct((TOKENS, SSM_D_INNER), BF16),
        grid=(BATCH, nchunks),
        in_specs=[
            pl.BlockSpec((CHUNK, SSM_D_INNER), lambda b, c: (rowblk(b, c), 0)),
            pl.BlockSpec((CHUNK, SSM_D_INNER), lambda b, c: (rowblk(b, c), 4)),
            pl.BlockSpec((CHUNK, GROUP_W), lambda b, c: (rowblk(b, c), 10)),
            pl.BlockSpec((CHUNK, LANES), lambda b, c: (rowblk(b, c), 0)),
            full((CONV_WIDTH, SSM_D_INNER)), full((CONV_WIDTH, GROUP_W)),
            full((1, SSM_D_INNER)), full((1, GROUP_W)),
            full((1, LANES)), full((1, LANES)),
            full((2 * LANES, SSM_D_INNER)),
            full((1, SSM_D_INNER)), full((1, SSM_D_INNER)),
        ],
        out_specs=pl.BlockSpec((CHUNK, SSM_D_INNER), lambda b, c: (rowblk(b, c), 0)),
        scratch_shapes=[
            pltpu.VMEM((CHUNK + 8, SSM_D_INNER), F32),
            pltpu.VMEM((CHUNK + 8, GROUP_W), F32),
            pltpu.VMEM((SSM_GROUPS, SSM_STATE, GROUP_W), F32),
        ],
        compiler_params=pltpu.CompilerParams(
            dimension_semantics=("parallel", "arbitrary"),
            vmem_limit_bytes=VMEM_LIMIT),
        name="ssd",
    )(proj, proj, proj, gates_raw, cw_x, cw_bc, cb_x, cb_bc, dtb_row, alog_row, ee, dsk_row, nw_row)


ATT_T = 256
ATT_K = 2 * LANES


def _attn_kernel(q_ref, k_ref, v_ref, aux_ref, gq_ref, gk_ref, o_ref, qa_ref, kt_ref, vb_ref):
    nblk = SEQ // ATT_T
    lane = lax.broadcasted_iota(jnp.int32, (ATT_T, LANES), 1)
    r2 = lax.broadcasted_iota(jnp.int32, (2 * LANES, LANES), 0)
    c2 = lax.broadcasted_iota(jnp.int32, (2 * LANES, LANES), 1)
    bd2 = jnp.where(((r2 & (LANES - 1)) >> 6) == (c2 >> 6), 1.0, 0.0).astype(BF16)
    part = lane >> 4
    sub = lane & 15
    is_val = (part < 3) & (sub < 2)
    is_neg = (part >= 3) & (part < 6) & (sub < 2)
    scale = ATTN_HEAD_DIM ** -0.5

    def head_norm(u, g_row):
        ssq = _dot(_split2(u * u), bd2)
        return u * lax.rsqrt(ssq * (1.0 / ATTN_HEAD_DIM) + EPS) * g_row

    def prep(blk, _):
        r0 = pl.multiple_of(blk * ATT_T, ATT_T)
        rows = pl.ds(r0, ATT_T)
        qn = head_norm(q_ref[rows, :], gq_ref[...]) * scale
        kn = head_norm(k_ref[rows, :], gk_ref[...])
        aux = aux_ref[rows, :].astype(F32)
        for j in range(2):
            mine = (lane >> 6) == j
            q_main = jnp.where(mine, qn, 0.0)
            q_aux = jnp.where(is_val & (sub == j), aux, jnp.where(is_neg & (sub == j), 1.0, 0.0))
            qa_ref[j, rows, :] = jnp.concatenate([q_main, q_aux], axis=-1).astype(BF16)
        k_aux = jnp.where(is_neg, aux, jnp.where(is_val, 1.0, 0.0))
        k_aug = jnp.concatenate([kn, k_aux], axis=-1)
        kt_ref[blk] = k_aug.T.astype(BF16)
        vb_ref[rows, :] = v_ref[rows, :].astype(BF16)
        return 0

    lax.fori_loop(0, nblk, prep, 0)

    rowi = lax.broadcasted_iota(jnp.int32, (ATT_T, ATT_T), 0)
    coli = lax.broadcasted_iota(jnp.int32, (ATT_T, ATT_T), 1)
    causal = rowi >= coli

    def q_block(i, _):
        r0 = pl.multiple_of(i * ATT_T, ATT_T)
        outs = []
        for j in range(2):
            qa = qa_ref[j, pl.ds(r0, ATT_T), :]

            def update(t, carry, masked):
                m_prev, l_prev, acc = carry
                s = _dot(qa, kt_ref[t])
                if masked:
                    s = jnp.where(causal, s, NEG_BIG)
                m_new = jnp.maximum(m_prev, jnp.max(s, axis=-1, keepdims=True))
                alpha = jnp.exp(m_prev - m_new)
                p = jnp.exp(s - m_new)
                l_new = alpha * l_prev + jnp.sum(p, axis=-1, keepdims=True)
                vv = vb_ref[pl.ds(pl.multiple_of(t * ATT_T, ATT_T), ATT_T), :]
                acc = alpha * acc + _dot(p.astype(BF16), vv)
                return m_new, l_new, acc

            init = (jnp.full((ATT_T, 1), NEG_BIG, F32), jnp.zeros((ATT_T, 1), F32),
                    jnp.zeros((ATT_T, LANES), F32))
            carry = lax.fori_loop(0, i, lambda t, cr: update(t, cr, False), init)
            _, l_fin, acc = update(i, carry, True)
            outs.append(acc / l_fin)
        o_ref[pl.ds(r0, ATT_T), :] = jnp.where(lane < ATTN_HEAD_DIM, outs[0], outs[1]).astype(BF16)
        return 0

    lax.fori_loop(0, nblk, q_block, 0)


def _attention(proj, aux, gq_row, gk_row):
    qcol = ATTN_WIDTH // LANES
    return pl.pallas_call(
        _attn_kernel,
        out_shape=jax.ShapeDtypeStruct((TOKENS, ATTN_WIDTH), BF16),
        grid=(BATCH, HEAD_PAIRS),
        in_specs=[
            pl.BlockSpec((SEQ, LANES), lambda b, hp: (b, qcol + hp)),
            pl.BlockSpec((SEQ, LANES), lambda b, hp: (b, 2 * qcol + hp)),
            pl.BlockSpec((SEQ, LANES), lambda b, hp: (b, 3 * qcol + hp)),
            pl.BlockSpec((None, None, SEQ, LANES), lambda b, hp: (b, hp, 0, 0)),
            pl.BlockSpec((1, LANES), lambda b, hp: (0, 0)),
            pl.BlockSpec((1, LANES), lambda b, hp: (0, 0)),
        ],
        out_specs=pl.BlockSpec((SEQ, LANES), lambda b, hp: (b, hp)),
        scratch_shapes=[
            pltpu.VMEM((2, SEQ, ATT_K), BF16),
            pltpu.VMEM((SEQ // ATT_T, ATT_K, ATT_T), BF16),
            pltpu.VMEM((SEQ, LANES), BF16),
        ],
        compiler_params=pltpu.CompilerParams(
            dimension_semantics=("parallel", "parallel"),
            vmem_limit_bytes=VMEM_LIMIT),
        name="fox_attention",
    )(proj, proj, proj, aux, gq_row, gk_row)


OUT_TN = 256


def _outproj_kernel(y_ref, o_ref, x_ref, w1_ref, w2_ref, out_ref):
    y = y_ref[...]
    o = o_ref[...]
    for n in range(0, D_MODEL, OUT_TN):
        cols = slice(n, n + OUT_TN)
        out_ref[:, cols] = x_ref[:, cols] + _dot(y, w1_ref[:, cols]) + _dot(o, w2_ref[:, cols])


def _outproj(y, o, x2d, w1, w2, tm=512):
    return pl.pallas_call(
        _outproj_kernel,
        out_shape=jax.ShapeDtypeStruct((TOKENS, D_MODEL), F32),
        grid=(TOKENS // tm,),
        in_specs=[
            pl.BlockSpec((tm, SSM_D_INNER), lambda i: (i, 0)),
            pl.BlockSpec((tm, ATTN_WIDTH), lambda i: (i, 0)),
            pl.BlockSpec((tm, D_MODEL), lambda i: (i, 0)),
            pl.BlockSpec((SSM_D_INNER, D_MODEL), lambda i: (0, 0)),
            pl.BlockSpec((ATTN_WIDTH, D_MODEL), lambda i: (0, 0)),
        ],
        out_specs=pl.BlockSpec((tm, D_MODEL), lambda i: (i, 0)),
        compiler_params=pltpu.CompilerParams(
            dimension_semantics=("parallel",), vmem_limit_bytes=VMEM_LIMIT),
        name="out_proj",
    )(y, o, x2d, w1, w2)


def _xattn_kernel(x_ref, g_ref, wq_ref, kv_ref, gq_ref, gk_ref, wo_ref, out_ref, o_scr):
    x = x_ref[...]
    h = _rms(x, g_ref[...]).astype(BF16)
    scale = XATTN_HEAD_DIM ** -0.5
    for a in range(XATTN_HEADS):
        cols = slice(a * XATTN_HEAD_DIM, (a + 1) * XATTN_HEAD_DIM)
        q = _dot(h, wq_ref[:, cols])
        qn = (_rms(q, gq_ref[...]) * scale).astype(BF16)
        kn = _rms(kv_ref[:, cols], gk_ref[...]).astype(BF16)
        v = kv_ref[:, D_MODEL + a * XATTN_HEAD_DIM:D_MODEL + (a + 1) * XATTN_HEAD_DIM].astype(BF16)
        s = _dot_nt(qn, kn)
        e = jnp.exp(s - jnp.max(s, axis=-1, keepdims=True))
        p = e / jnp.sum(e, axis=-1, keepdims=True)
        o_scr[:, cols] = _dot(p.astype(BF16), v).astype(BF16)
    o = o_scr[...]
    for n in range(0, D_MODEL, OUT_TN):
        cols = slice(n, n + OUT_TN)
        out_ref[:, cols] = x_ref[:, cols] + _dot(o, wo_ref[:, cols])


def _xattn(x1, g_row, wq, kv, gq_row, gk_row, wo, tm=512):
    nt = SEQ // tm
    return pl.pallas_call(
        _xattn_kernel,
        out_shape=jax.ShapeDtypeStruct((TOKENS, D_MODEL), F32),
        grid=(BATCH, nt),
        in_specs=[
            pl.BlockSpec((tm, D_MODEL), lambda b, i: (b * nt + i, 0)),
            pl.BlockSpec((1, D_MODEL), lambda b, i: (0, 0)),
            pl.BlockSpec((D_MODEL, D_MODEL), lambda b, i: (0, 0)),
            pl.BlockSpec((MEM_LEN, 2 * D_MODEL), lambda b, i: (b, 0)),
            pl.BlockSpec((1, XATTN_HEAD_DIM), lambda b, i: (0, 0)),
            pl.BlockSpec((1, XATTN_HEAD_DIM), lambda b, i: (0, 0)),
            pl.BlockSpec((D_MODEL, D_MODEL), lambda b, i: (0, 0)),
        ],
        out_specs=pl.BlockSpec((tm, D_MODEL), lambda b, i: (b * nt + i, 0)),
        scratch_shapes=[pltpu.VMEM((tm, D_MODEL), BF16)],
        compiler_params=pltpu.CompilerParams(
            dimension_semantics=("parallel", "parallel"), vmem_limit_bytes=VMEM_LIMIT),
        name="mem_xattn",
    )(x1, g_row, wq, kv, gq_row, gk_row, wo)


FF_CHUNK = 1024


def _mlp_kernel(x_ref, g_ref, wu_ref, wd_ref, out_ref, h_scr, acc_scr):
    h_scr[...] = _rms(x_ref[...], g_ref[...]).astype(BF16)
    acc_scr[...] = x_ref[...]
    for f in range(0, D_FF, FF_CHUNK):
        u = jnp.maximum(_dot(h_scr[...], wu_ref[:, f:f + FF_CHUNK]), 0.0)
        acc_scr[...] += _dot((u * u).astype(BF16), wd_ref[f:f + FF_CHUNK, :])
    out_ref[...] = acc_scr[...]


def _mlp(x2, g_row, wu, wd, tm=512):
    return pl.pallas_call(
        _mlp_kernel,
        out_shape=jax.ShapeDtypeStruct((TOKENS, D_MODEL), F32),
        grid=(TOKENS // tm,),
        in_specs=[
            pl.BlockSpec((tm, D_MODEL), lambda i: (i, 0)),
            pl.BlockSpec((1, D_MODEL), lambda i: (0, 0)),
            pl.BlockSpec((D_MODEL, D_FF), lambda i: (0, 0)),
            pl.BlockSpec((D_FF, D_MODEL), lambda i: (0, 0)),
        ],
        out_specs=pl.BlockSpec((tm, D_MODEL), lambda i: (i, 0)),
        scratch_shapes=[pltpu.VMEM((tm, D_MODEL), BF16), pltpu.VMEM((tm, D_MODEL), F32)],
        compiler_params=pltpu.CompilerParams(
            dimension_semantics=("parallel",), vmem_limit_bytes=VMEM_LIMIT),
        name="relu2_mlp",
    )(x2, g_row, wu, wd)


def _lane_row(vec, offset):
    return jnp.zeros((1, LANES), F32).at[0, offset:offset + vec.shape[0]].set(vec.astype(F32))


def _layer(x2d, mem2d, g_mix, w_in, conv_w, conv_b, dt_bias, a_log, d_skip, ssm_norm_w,
           g_q, g_k, f_bias, w_out, g_xattn, g_mem, xq_w, xkv_w, xg_q, xg_k, xo_w,
           g_mlp, w_up, w_down):
    z0, xbc0 = 0, SSM_D_INNER
    bc0 = xbc0 + SSM_D_INNER
    dt0 = bc0 + 2 * SSM_GROUPS * SSM_STATE
    q0 = dt0 + SSM_HEADS
    k0 = q0 + ATTN_WIDTH
    v0 = k0 + ATTN_WIDTH
    f0 = v0 + ATTN_WIDTH
    w_main = jnp.concatenate(
        [w_in[:, z0:xbc0], w_in[:, q0:f0], w_in[:, xbc0:dt0]], axis=1).astype(BF16)
    w_gate = jnp.concatenate(
        [w_in[:, f0:f0 + ATTN_HEADS], w_in[:, dt0:q0],
         jnp.zeros((D_MODEL, LANES - ATTN_HEADS - SSM_HEADS), F32)], axis=1).astype(BF16)
    row = lambda v: v.astype(F32).reshape(1, -1)

    proj = _norm_mm(x2d, row(g_mix), w_main, tm=1024, tn=512)
    gates_raw = _norm_mm(x2d, row(g_mix), w_gate, tm=1024, tn=LANES)

    aux = _gates(gates_raw, _lane_row(f_bias, 0), jnp.asarray(_gate_route_matrix(), BF16))

    y = _ssd(proj, gates_raw,
             conv_w[:, :SSM_D_INNER], conv_w[:, SSM_D_INNER:],
             row(conv_b[:SSM_D_INNER]), row(conv_b[SSM_D_INNER:]),
             _lane_row(dt_bias, DT_LANE), _lane_row(a_log, DT_LANE),
             jnp.asarray(_head_expand_matrix(), BF16),
             row(jnp.repeat(d_skip, SSM_HEAD_DIM)), row(ssm_norm_w))

    o = _attention(proj, aux, row(jnp.tile(g_q, 2)), row(jnp.tile(g_k, 2)))

    x1 = _outproj(y, o, x2d, w_out[:SSM_D_INNER].astype(BF16), w_out[SSM_D_INNER:].astype(BF16))

    kv = _norm_mm(mem2d, row(g_mem), xkv_w.astype(BF16), tm=1024, tn=512)
    x2 = _xattn(x1, row(g_xattn), xq_w.astype(BF16), kv, row(xg_q), row(xg_k), xo_w.astype(BF16))

    return _mlp(x2, row(g_mlp), w_up.astype(BF16), w_down.astype(BF16))


def kernel(x, mem, g_mix, w_in, conv_w, conv_b, dt_bias, a_log, d_skip, ssm_norm_w, g_q, g_k,
           f_bias, w_out, g_xattn, g_mem, xq_w, xkv_w, xg_q, xg_k, xo_w, g_mlp, w_up, w_down):
    x2d = x.reshape(TOKENS, D_MODEL)
    mem2d = mem.reshape(BATCH * MEM_LEN, D_MODEL)
    depth = g_mix.shape[0]
    for l in range(depth):
        x2d = _layer(x2d, mem2d, g_mix[l], w_in[l], conv_w[l], conv_b[l], dt_bias[l], a_log[l],
                     d_skip[l], ssm_norm_w[l], g_q[l], g_k[l], f_bias[l], w_out[l], g_xattn[l],
                     g_mem[l], xq_w[l], xkv_w[l], xg_q[l], xg_k[l], xo_w[l], g_mlp[l], w_up[l],
                     w_down[l])
    return x2d.reshape(BATCH, SEQ, D_MODEL)
```

```python
import functools

import numpy as np
import jax
import jax.numpy as jnp
from jax import lax
from jax.experimental import pallas as pl
from jax.experimental.pallas import tpu as pltpu

F32 = jnp.float32
BF16 = jnp.bfloat16

D_MODEL = 1024
BATCH = 8
SEQ = 2048
TOKENS = BATCH * SEQ
MEM_LEN = 256
SSM_HEAD_DIM = 64
SSM_HEADS = 16
SSM_D_INNER = 1024
SSM_GROUPS = 2
SSM_STATE = 128
CONV_WIDTH = 4
CHUNK = 128
ATTN_HEAD_DIM = 64
ATTN_HEADS = 16
ATTN_WIDTH = 1024
XATTN_HEADS = 4
XATTN_HEAD_DIM = 256
D_FF = 4096
EPS = 1e-5

LANES = 128
N_MAIN = 5632
HEAD_PAIRS = ATTN_HEADS // 2
GROUP_W = SSM_D_INNER // SSM_GROUPS
DT_LANE = 16
VMEM_LIMIT = 56 * 1024 * 1024
NEG_BIG = -1e30


def _rms(xf, g_row):
    ms = jnp.mean(xf * xf, axis=-1, keepdims=True)
    return xf * lax.rsqrt(ms + EPS) * g_row


def _split2(a):
    hi = a.astype(BF16)
    mid = (a - hi.astype(F32)).astype(BF16)
    return jnp.concatenate([hi, mid], axis=-1)


def _split3(a):
    hi = a.astype(BF16)
    r1 = a - hi.astype(F32)
    mid = r1.astype(BF16)
    lo = (r1 - mid.astype(F32)).astype(BF16)
    return jnp.concatenate([hi, mid, lo], axis=-1)


def _softplus(x):
    return jnp.maximum(x, 0.0) + jnp.log1p(jnp.exp(-jnp.abs(x)))


def _silu(x):
    return x * (1.0 / (1.0 + jnp.exp(-x)))


def _dot(a, b):
    return jnp.dot(a, b, preferred_element_type=F32)


def _dot_nt(a, b):
    return lax.dot_general(a, b, (((1,), (1,)), ((), ())), preferred_element_type=F32)


def _norm_mm_kernel(x_ref, g_ref, w_ref, o_ref, h_ref):
    @pl.when(pl.program_id(1) == 0)
    def _():
        h_ref[...] = _rms(x_ref[...], g_ref[...]).astype(BF16)

    o_ref[...] = _dot(h_ref[...], w_ref[...]).astype(o_ref.dtype)


def _norm_mm(x, g_row, w, tm, tn, out_dtype=F32):
    m, k = x.shape
    n = w.shape[1]
    return pl.pallas_call(
        _norm_mm_kernel,
        out_shape=jax.ShapeDtypeStruct((m, n), out_dtype),
        grid=(m // tm, n // tn),
        in_specs=[
            pl.BlockSpec((tm, k), lambda i, j: (i, 0)),
            pl.BlockSpec((1, k), lambda i, j: (0, 0)),
            pl.BlockSpec((k, tn), lambda i, j: (0, j)),
        ],
        out_specs=pl.BlockSpec((tm, tn), lambda i, j: (i, j)),
        scratch_shapes=[pltpu.VMEM((tm, k), BF16)],
        compiler_params=pltpu.CompilerParams(
            dimension_semantics=("parallel", "arbitrary"),
            vmem_limit_bytes=VMEM_LIMIT),
        name="norm_mm",
    )(x, g_row, w)


def _gate_route_matrix():
    r = np.zeros((3 * LANES, HEAD_PAIRS * LANES), np.float32)
    for h in range(ATTN_HEADS):
        for m in range(3):
            base = (h // 2) * LANES
            r[m * LANES + h, base + 16 * m + (h % 2)] = 1.0
            r[m * LANES + h, base + 48 + 16 * m + (h % 2)] = -1.0
    return r


def _gates_kernel(g_ref, fb_ref, r_ref, aux_ref):
    row = lax.broadcasted_iota(jnp.int32, (CHUNK, CHUNK), 0)
    col = lax.broadcasted_iota(jnp.int32, (CHUNK, CHUNK), 1)
    tril = jnp.where(row >= col, 1.0, 0.0).astype(BF16)
    fb = fb_ref[...]
    rmat = r_ref[...]

    def body(blk, carry):
        r0 = pl.multiple_of(blk * CHUNK, CHUNK)
        g = g_ref[pl.ds(r0, CHUNK), :]
        log_f = -_softplus(-(g + fb))
        part = _dot(tril, _split3(log_f))
        cum = part[:, :LANES] + part[:, LANES:2 * LANES] + part[:, 2 * LANES:] + carry
        routed = _dot(_split3(cum), rmat)
        for hp in range(HEAD_PAIRS):
            aux_ref[hp, pl.ds(r0, CHUNK), :] = routed[:, hp * LANES:(hp + 1) * LANES].astype(BF16)
        return cum[CHUNK - 1:CHUNK, :]

    lax.fori_loop(0, SEQ // CHUNK, body, jnp.zeros((1, LANES), F32))


def _gates(gates_raw, fb_row, rmat):
    return pl.pallas_call(
        _gates_kernel,
        out_shape=jax.ShapeDtypeStruct((BATCH, HEAD_PAIRS, SEQ, LANES), BF16),
        grid=(BATCH,),
        in_specs=[
            pl.BlockSpec((SEQ, LANES), lambda b: (b, 0)),
            pl.BlockSpec((1, LANES), lambda b: (0, 0)),
            pl.BlockSpec((3 * LANES, HEAD_PAIRS * LANES), lambda b: (0, 0)),
        ],
        out_specs=pl.BlockSpec((None, HEAD_PAIRS, SEQ, LANES), lambda b: (b, 0, 0, 0)),
        compiler_params=pltpu.CompilerParams(dimension_semantics=("parallel",)),
        name="gates",
    )(gates_raw, fb_row, rmat)


def _head_expand_matrix():
    e = np.zeros((2 * LANES, SSM_D_INNER), np.float32)
    for h in range(SSM_HEADS):
        e[DT_LANE + h, h * SSM_HEAD_DIM:(h + 1) * SSM_HEAD_DIM] = 1.0
        e[LANES + DT_LANE + h, h * SSM_HEAD_DIM:(h + 1) * SSM_HEAD_DIM] = 1.0
    return e


def _ssd_kernel(z_ref, xs_ref, bc_ref, g_ref, cwx_ref, cwb_ref, cbx_ref, cbb_ref,
                dtb_ref, alog_ref, ee_ref, dsk_ref, nw_ref, y_ref,
                xbuf_ref, bbuf_ref, state_ref):
    c = pl.program_id(1)

    @pl.when(c == 0)
    def _():
        xbuf_ref[0:8, :] = jnp.zeros((8, SSM_D_INNER), F32)
        bbuf_ref[0:8, :] = jnp.zeros((8, 2 * SSM_GROUPS * SSM_STATE), F32)
        state_ref[...] = jnp.zeros_like(state_ref)

    def conv_silu(u_ref, buf_ref, w_ref, b_ref):
        buf_ref[8:8 + CHUNK, :] = u_ref[...]
        acc = b_ref[...] + w_ref[3:4, :] * buf_ref[8:8 + CHUNK, :]
        for tap in range(3):
            acc = acc + w_ref[tap:tap + 1, :] * buf_ref[5 + tap:5 + tap + CHUNK, :]
        buf_ref[0:8, :] = buf_ref[CHUNK:CHUNK + 8, :]
        return _silu(acc)

    xs = conv_silu(xs_ref, xbuf_ref, cwx_ref, cbx_ref)
    bc = conv_silu(bc_ref, bbuf_ref, cwb_ref, cbb_ref)

    dt = _softplus(g_ref[...] + dtb_ref[...])
    da = dt * (-jnp.exp(alog_ref[...]))
    row = lax.broadcasted_iota(jnp.int32, (CHUNK, CHUNK), 0)
    col = lax.broadcasted_iota(jnp.int32, (CHUNK, CHUNK), 1)
    causal = row >= col
    tril = jnp.where(causal, 1.0, 0.0).astype(BF16)
    part = _dot(tril, _split3(da))
    acs = part[:, :LANES] + part[:, LANES:2 * LANES] + part[:, 2 * LANES:]
    a_last = acs[CHUNK - 1:CHUNK, :]
    exp_a = jnp.exp(acs)
    dt_decay = dt * jnp.exp(a_last - acs)
    ee = ee_ref[...]
    e_dtdec = _dot(_split2(dt_decay), ee)
    e_expa = _dot(_split2(exp_a), ee)
    acs_t = acs.T
    dt_t = dt.T

    xs_b = xs.astype(BF16)
    xdec_b = (xs * e_dtdec).astype(BF16)
    lane = lax.broadcasted_iota(jnp.int32, (CHUNK, LANES), 1)
    first_head = lane < SSM_HEAD_DIM

    y_parts = []
    for g in range(SSM_GROUPS):
        b_g = bc[:, g * SSM_STATE:(g + 1) * SSM_STATE]
        c_g = bc[:, (SSM_GROUPS + g) * SSM_STATE:(SSM_GROUPS + g + 1) * SSM_STATE]
        c_gb = c_g.astype(BF16)
        cb = _dot_nt(c_gb, b_g.astype(BF16))
        state = state_ref[g]
        cols = slice(g * GROUP_W, (g + 1) * GROUP_W)
        y_off = _dot(c_gb, state.astype(BF16)) * e_expa[:, cols]
        st_new = _dot(b_g.T.astype(BF16), xdec_b[:, cols])
        state_ref[g] = state * e_expa[CHUNK - 1:CHUNK, cols] + st_new
        diag = []
        for pair in range(GROUP_W // LANES):
            x_pair = xs_b[:, g * GROUP_W + pair * LANES:g * GROUP_W + (pair + 1) * LANES]
            res = []
            for j in range(2):
                h = g * (SSM_HEADS // SSM_GROUPS) + 2 * pair + j
                hl = DT_LANE + h
                a_col = jnp.broadcast_to(acs[:, hl:hl + 1], (CHUNK, CHUNK))
                seg = jnp.where(causal, a_col - acs_t[hl:hl + 1, :], -jnp.inf)
                m_h = cb * jnp.exp(seg) * dt_t[hl:hl + 1, :]
                res.append(_dot(m_h.astype(BF16), x_pair))
            diag.append(jnp.where(first_head, res[0], res[1]))
        y_parts.append(jnp.concatenate(diag, axis=-1) + y_off)
    y = jnp.concatenate(y_parts, axis=-1) + dsk_ref[...] * xs
    y = y * _silu(z_ref[...])
    normed = []
    for g in range(SSM_GROUPS):
        y_g = y[:, g * GROUP_W:(g + 1) * GROUP_W]
        normed.append(y_g * lax.rsqrt(jnp.mean(y_g * y_g, axis=-1, keepdims=True) + EPS))
    y_ref[...] = (jnp.concatenate(normed, axis=-1) * nw_ref[...]).astype(BF16)


def _ssd(proj, gates_raw, cw_x, cw_bc, cb_x, cb_bc, dtb_row, alog_row, ee, dsk_row, nw_row):
    nchunks = SEQ // CHUNK
    rowblk = lambda b, c: b * nchunks + c
    full = lambda shape: pl.BlockSpec(shape, lambda b, c: (0, 0))
    return pl.pallas_call(
        _ssd_kernel,
        out_shape=jax.ShapeDtypeStruct((TOKENS, SSM_D_INNER), BF16),
        grid=(BATCH, nchunks),
        in_specs=[
            pl.BlockSpec((CHUNK, SSM_D_INNER), lambda b, c: (rowblk(b, c), 0)),
            pl.BlockSpec((CHUNK, SSM_D_INNER), lambda b, c: (rowblk(b, c), 4)),
            pl.BlockSpec((CHUNK, GROUP_W), lambda b, c: (rowblk(b, c), 10)),
            pl.BlockSpec((CHUNK, LANES), lambda b, c: (rowblk(b, c), 0)),
            full((CONV_WIDTH, SSM_D_INNER)), full((CONV_WIDTH, GROUP_W)),
            full((1, SSM_D_INNER)), full((1, GROUP_W)),
            full((1, LANES)), full((1, LANES)),
            full((2 * LANES, SSM_D_INNER)),
            full((1, SSM_D_INNER)), full((1, SSM_D_INNER)),
        ],
        out_specs=pl.BlockSpec((CHUNK, SSM_D_INNER), lambda b, c: (rowblk(b, c), 0)),
        scratch_shapes=[
            pltpu.VMEM((CHUNK + 8, SSM_D_INNER), F32),
            pltpu.VMEM((CHUNK + 8, GROUP_W), F32),
            pltpu.VMEM((SSM_GROUPS, SSM_STATE, GROUP_W), F32),
        ],
        compiler_params=pltpu.CompilerParams(
            dimension_semantics=("parallel", "arbitrary"),
            vmem_limit_bytes=VMEM_LIMIT),
        name="ssd",
    )(proj, proj, proj, gates_raw, cw_x, cw_bc, cb_x, cb_bc, dtb_row, alog_row, ee, dsk_row, nw_row)


ATT_T = 256
ATT_K = 2 * LANES


def _attn_kernel(q_ref, k_ref, v_ref, aux_ref, gq_ref, gk_ref, o_ref, qa_ref, kt_ref, vb_ref):
    nblk = SEQ // ATT_T
    lane = lax.broadcasted_iota(jnp.int32, (ATT_T, LANES), 1)
    r2 = lax.broadcasted_iota(jnp.int32, (2 * LANES, LANES), 0)
    c2 = lax.broadcasted_iota(jnp.int32, (2 * LANES, LANES), 1)
    bd2 = jnp.where(((r2 & (LANES - 1)) >> 6) == (c2 >> 6), 1.0, 0.0).astype(BF16)
    part = lane >> 4
    sub = lane & 15
    is_val = (part < 3) & (sub < 2)
    is_neg = (part >= 3) & (part < 6) & (sub < 2)
    scale = ATTN_HEAD_DIM ** -0.5

    def head_norm(u, g_row):
        ssq = _dot(_split2(u * u), bd2)
        return u * lax.rsqrt(ssq * (1.0 / ATTN_HEAD_DIM) + EPS) * g_row

    def prep(blk, _):
        r0 = pl.multiple_of(blk * ATT_T, ATT_T)
        rows = pl.ds(r0, ATT_T)
        qn = head_norm(q_ref[rows, :], gq_ref[...]) * scale
        kn = head_norm(k_ref[rows, :], gk_ref[...])
        aux = aux_ref[rows, :].astype(F32)
        for j in range(2):
            mine = (lane >> 6) == j
            q_main = jnp.where(mine, qn, 0.0)
            q_aux = jnp.where(is_val & (sub == j), aux, jnp.where(is_neg & (sub == j), 1.0, 0.0))
            qa_ref[j, rows, :] = jnp.concatenate([q_main, q_aux], axis=-1).astype(BF16)
        k_aux = jnp.where(is_neg, aux, jnp.where(is_val, 1.0, 0.0))
        k_aug = jnp.concatenate([kn, k_aux], axis=-1)
        kt_ref[blk] = k_aug.T.astype(BF16)
        vb_ref[rows, :] = v_ref[rows, :].astype(BF16)
        return 0

    lax.fori_loop(0, nblk, prep, 0)

    rowi = lax.broadcasted_iota(jnp.int32, (ATT_T, ATT_T), 0)
    coli = lax.broadcasted_iota(jnp.int32, (ATT_T, ATT_T), 1)
    causal = rowi >= coli

    for i in range(nblk):
        rows = slice(i * ATT_T, (i + 1) * ATT_T)
        outs = []
        for j in range(2):
            qa = qa_ref[j, rows, :]
            s = [_dot(qa, kt_ref[t]) for t in range(i)]
            s.append(jnp.where(causal, _dot(qa, kt_ref[i]), NEG_BIG))
            tile_max = functools.reduce(jnp.maximum, s)
            m = jnp.max(tile_max, axis=-1, keepdims=True)
            p = [jnp.exp(s_t - m) for s_t in s]
            denom = jnp.sum(functools.reduce(jnp.add, p), axis=-1, keepdims=True)
            acc = functools.reduce(jnp.add, [
                _dot(p[t].astype(BF16), vb_ref[t * ATT_T:(t + 1) * ATT_T, :]) for t in range(i + 1)])
            outs.append(acc / denom)
        o_ref[rows, :] = jnp.where(lane < ATTN_HEAD_DIM, outs[0], outs[1]).astype(BF16)


def _attention(proj, aux, gq_row, gk_row):
    qcol = ATTN_WIDTH // LANES
    return pl.pallas_call(
        _attn_kernel,
        out_shape=jax.ShapeDtypeStruct((TOKENS, ATTN_WIDTH), BF16),
        grid=(BATCH, HEAD_PAIRS),
        in_specs=[
            pl.BlockSpec((SEQ, LANES), lambda b, hp: (b, qcol + hp)),
            pl.BlockSpec((SEQ, LANES), lambda b, hp: (b, 2 * qcol + hp)),
            pl.BlockSpec((SEQ, LANES), lambda b, hp: (b, 3 * qcol + hp)),
            pl.BlockSpec((None, None, SEQ, LANES), lambda b, hp: (b, hp, 0, 0)),
            pl.BlockSpec((1, LANES), lambda b, hp: (0, 0)),
            pl.BlockSpec((1, LANES), lambda b, hp: (0, 0)),
        ],
        out_specs=pl.BlockSpec((SEQ, LANES), lambda b, hp: (b, hp)),
        scratch_shapes=[
            pltpu.VMEM((2, SEQ, ATT_K), BF16),
            pltpu.VMEM((SEQ // ATT_T, ATT_K, ATT_T), BF16),
            pltpu.VMEM((SEQ, LANES), BF16),
        ],
        compiler_params=pltpu.CompilerParams(
            dimension_semantics=("parallel", "parallel"),
            vmem_limit_bytes=VMEM_LIMIT),
        name="fox_attention",
    )(proj, proj, proj, aux, gq_row, gk_row)


OUT_TN = 256


def _outproj_kernel(y_ref, o_ref, x_ref, w1_ref, w2_ref, out_ref):
    y = y_ref[...]
    o = o_ref[...]
    for n in range(0, D_MODEL, OUT_TN):
        cols = slice(n, n + OUT_TN)
        out_ref[:, cols] = x_ref[:, cols] + _dot(y, w1_ref[:, cols]) + _dot(o, w2_ref[:, cols])


def _outproj(y, o, x2d, w1, w2, tm=512):
    return pl.pallas_call(
        _outproj_kernel,
        out_shape=jax.ShapeDtypeStruct((TOKENS, D_MODEL), F32),
        grid=(TOKENS // tm,),
        in_specs=[
            pl.BlockSpec((tm, SSM_D_INNER), lambda i: (i, 0)),
            pl.BlockSpec((tm, ATTN_WIDTH), lambda i: (i, 0)),
            pl.BlockSpec((tm, D_MODEL), lambda i: (i, 0)),
            pl.BlockSpec((SSM_D_INNER, D_MODEL), lambda i: (0, 0)),
            pl.BlockSpec((ATTN_WIDTH, D_MODEL), lambda i: (0, 0)),
        ],
        out_specs=pl.BlockSpec((tm, D_MODEL), lambda i: (i, 0)),
        compiler_params=pltpu.CompilerParams(
            dimension_semantics=("parallel",), vmem_limit_bytes=VMEM_LIMIT),
        name="out_proj",
    )(y, o, x2d, w1, w2)


def _xattn_kernel(x_ref, g_ref, wq_ref, kv_ref, gq_ref, gk_ref, wo_ref, out_ref, o_scr):
    x = x_ref[...]
    h = _rms(x, g_ref[...]).astype(BF16)
    scale = XATTN_HEAD_DIM ** -0.5
    for a in range(XATTN_HEADS):
        cols = slice(a * XATTN_HEAD_DIM, (a + 1) * XATTN_HEAD_DIM)
        q = _dot(h, wq_ref[:, cols])
        qn = (_rms(q, gq_ref[...]) * scale).astype(BF16)
        kn = _rms(kv_ref[:, cols], gk_ref[...]).astype(BF16)
        v = kv_ref[:, D_MODEL + a * XATTN_HEAD_DIM:D_MODEL + (a + 1) * XATTN_HEAD_DIM].astype(BF16)
        s = _dot_nt(qn, kn)
        e = jnp.exp(s - jnp.max(s, axis=-1, keepdims=True))
        p = e / jnp.sum(e, axis=-1, keepdims=True)
        o_scr[:, cols] = _dot(p.astype(BF16), v).astype(BF16)
    o = o_scr[...]
    for n in range(0, D_MODEL, OUT_TN):
        cols = slice(n, n + OUT_TN)
        out_ref[:, cols] = x_ref[:, cols] + _dot(o, wo_ref[:, cols])


def _xattn(x1, g_row, wq, kv, gq_row, gk_row, wo, tm=512):
    nt = SEQ // tm
    return pl.pallas_call(
        _xattn_kernel,
        out_shape=jax.ShapeDtypeStruct((TOKENS, D_MODEL), F32),
        grid=(BATCH, nt),
        in_specs=[
            pl.BlockSpec((tm, D_MODEL), lambda b, i: (b * nt + i, 0)),
            pl.BlockSpec((1, D_MODEL), lambda b, i: (0, 0)),
            pl.BlockSpec((D_MODEL, D_MODEL), lambda b, i: (0, 0)),
            pl.BlockSpec((MEM_LEN, 2 * D_MODEL), lambda b, i: (b, 0)),
            pl.BlockSpec((1, XATTN_HEAD_DIM), lambda b, i: (0, 0)),
            pl.BlockSpec((1, XATTN_HEAD_DIM), lambda b, i: (0, 0)),
            pl.BlockSpec((D_MODEL, D_MODEL), lambda b, i: (0, 0)),
        ],
        out_specs=pl.BlockSpec((tm, D_MODEL), lambda b, i: (b * nt + i, 0)),
        scratch_shapes=[pltpu.VMEM((tm, D_MODEL), BF16)],
        compiler_params=pltpu.CompilerParams(
            dimension_semantics=("parallel", "parallel"), vmem_limit_bytes=VMEM_LIMIT),
        name="mem_xattn",
    )(x1, g_row, wq, kv, gq_row, gk_row, wo)


FF_CHUNK = 1024


def _mlp_kernel(x_ref, g_ref, wu_ref, wd_ref, out_ref, h_scr, acc_scr):
    h_scr[...] = _rms(x_ref[...], g_ref[...]).astype(BF16)
    acc_scr[...] = x_ref[...]
    for f in range(0, D_FF, FF_CHUNK):
        u = jnp.maximum(_dot(h_scr[...], wu_ref[:, f:f + FF_CHUNK]), 0.0)
        acc_scr[...] += _dot((u * u).astype(BF16), wd_ref[f:f + FF_CHUNK, :])
    out_ref[...] = acc_scr[...]


def _mlp(x2, g_row, wu, wd, tm=512):
    return pl.pallas_call(
        _mlp_kernel,
        out_shape=jax.ShapeDtypeStruct((TOKENS, D_MODEL), F32),
        grid=(TOKENS // tm,),
        in_specs=[
            pl.BlockSpec((tm, D_MODEL), lambda i: (i, 0)),
            pl.BlockSpec((1, D_MODEL), lambda i: (0, 0)),
            pl.BlockSpec((D_MODEL, D_FF), lambda i: (0, 0)),
            pl.BlockSpec((D_FF, D_MODEL), lambda i: (0, 0)),
        ],
        out_specs=pl.BlockSpec((tm, D_MODEL), lambda i: (i, 0)),
        scratch_shapes=[pltpu.VMEM((tm, D_MODEL), BF16), pltpu.VMEM((tm, D_MODEL), F32)],
        compiler_params=pltpu.CompilerParams(
            dimension_semantics=("parallel",), vmem_limit_bytes=VMEM_LIMIT),
        name="relu2_mlp",
    )(x2, g_row, wu, wd)


def _lane_row(vec, offset):
    return jnp.zeros((1, LANES), F32).at[0, offset:offset + vec.shape[0]].set(vec.astype(F32))


def _layer(x2d, mem2d, g_mix, w_in, conv_w, conv_b, dt_bias, a_log, d_skip, ssm_norm_w,
           g_q, g_k, f_bias, w_out, g_xattn, g_mem, xq_w, xkv_w, xg_q, xg_k, xo_w,
           g_mlp, w_up, w_down):
    z0, xbc0 = 0, SSM_D_INNER
    bc0 = xbc0 + SSM_D_INNER
    dt0 = bc0 + 2 * SSM_GROUPS * SSM_STATE
    q0 = dt0 + SSM_HEADS
    k0 = q0 + ATTN_WIDTH
    v0 = k0 + ATTN_WIDTH
    f0 = v0 + ATTN_WIDTH
    w_main = jnp.concatenate(
        [w_in[:, z0:xbc0], w_in[:, q0:f0], w_in[:, xbc0:dt0]], axis=1).astype(BF16)
    w_gate = jnp.concatenate(
        [w_in[:, f0:f0 + ATTN_HEADS], w_in[:, dt0:q0],
         jnp.zeros((D_MODEL, LANES - ATTN_HEADS - SSM_HEADS), F32)], axis=1).astype(BF16)
    row = lambda v: v.astype(F32).reshape(1, -1)

    proj = _norm_mm(x2d, row(g_mix), w_main, tm=1024, tn=512)
    gates_raw = _norm_mm(x2d, row(g_mix), w_gate, tm=1024, tn=LANES)

    aux = _gates(gates_raw, _lane_row(f_bias, 0), jnp.asarray(_gate_route_matrix(), BF16))

    y = _ssd(proj, gates_raw,
             conv_w[:, :SSM_D_INNER], conv_w[:, SSM_D_INNER:],
             row(conv_b[:SSM_D_INNER]), row(conv_b[SSM_D_INNER:]),
             _lane_row(dt_bias, DT_LANE), _lane_row(a_log, DT_LANE),
             jnp.asarray(_head_expand_matrix(), BF16),
             row(jnp.repeat(d_skip, SSM_HEAD_DIM)), row(ssm_norm_w))

    o = _attention(proj, aux, row(jnp.tile(g_q, 2)), row(jnp.tile(g_k, 2)))

    x1 = _outproj(y, o, x2d, w_out[:SSM_D_INNER].astype(BF16), w_out[SSM_D_INNER:].astype(BF16))

    kv = _norm_mm(mem2d, row(g_mem), xkv_w.astype(BF16), tm=1024, tn=512)
    x2 = _xattn(x1, row(g_xattn), xq_w.astype(BF16), kv, row(xg_q), row(xg_k), xo_w.astype(BF16))

    return _mlp(x2, row(g_mlp), w_up.astype(BF16), w_down.astype(BF16))


def kernel(x, mem, g_mix, w_in, conv_w, conv_b, dt_bias, a_log, d_skip, ssm_norm_w, g_q, g_k,
           f_bias, w_out, g_xattn, g_mem, xq_w, xkv_w, xg_q, xg_k, xo_w, g_mlp, w_up, w_down):
    x2d = x.reshape(TOKENS, D_MODEL)
    mem2d = mem.reshape(BATCH * MEM_LEN, D_MODEL)
    depth = g_mix.shape[0]
    for l in range(depth):
        x2d = _layer(x2d, mem2d, g_mix[l], w_in[l], conv_w[l], conv_b[l], dt_bias[l], a_log[l],
                     d_skip[l], ssm_norm_w[l], g_q[l], g_k[l], f_bias[l], w_out[l], g_xattn[l],
                     g_mem[l], xq_w[l], xkv_w[l], xg_q[l], xg_k[l], xo_w[l], g_mlp[l], w_up[l],
                     w_down[l])
    return x2d.reshape(BATCH, SEQ, D_MODEL)
```

```python
import functools

import numpy as np
import jax
import jax.numpy as jnp
from jax import lax
from jax.experimental import pallas as pl
from jax.experimental.pallas import tpu as pltpu

F32 = jnp.float32
BF16 = jnp.bfloat16

D_MODEL = 1024
BATCH = 8
SEQ = 2048
TOKENS = BATCH * SEQ
MEM_LEN = 256
SSM_HEAD_DIM = 64
SSM_HEADS = 16
SSM_D_INNER = 1024
SSM_GROUPS = 2
SSM_STATE = 128
CONV_WIDTH = 4
CHUNK = 128
ATTN_HEAD_DIM = 64
ATTN_HEADS = 16
ATTN_WIDTH = 1024
XATTN_HEADS = 4
XATTN_HEAD_DIM = 256
D_FF = 4096
EPS = 1e-5

LANES = 128
N_MAIN = 5632
HEAD_PAIRS = ATTN_HEADS // 2
GROUP_W = SSM_D_INNER // SSM_GROUPS
DT_LANE = 16
VMEM_LIMIT = 56 * 1024 * 1024
NEG_BIG = -1e30


def _rms(xf, g_row):
    ms = jnp.mean(xf * xf, axis=-1, keepdims=True)
    return xf * lax.rsqrt(ms + EPS) * g_row


def _split2(a):
    hi = a.astype(BF16)
    mid = (a - hi.astype(F32)).astype(BF16)
    return jnp.concatenate([hi, mid], axis=-1)


def _split3(a):
    hi = a.astype(BF16)
    r1 = a - hi.astype(F32)
    mid = r1.astype(BF16)
    lo = (r1 - mid.astype(F32)).astype(BF16)
    return jnp.concatenate([hi, mid, lo], axis=-1)


def _softplus(x):
    return jnp.maximum(x, 0.0) + jnp.log1p(jnp.exp(-jnp.abs(x)))


def _silu(x):
    return x * (1.0 / (1.0 + jnp.exp(-x)))


def _dot(a, b):
    return jnp.dot(a, b, preferred_element_type=F32)


def _dot_nt(a, b):
    return lax.dot_general(a, b, (((1,), (1,)), ((), ())), preferred_element_type=F32)


def _norm_mm_kernel(x_ref, g_ref, w_ref, o_ref, h_ref):
    @pl.when(pl.program_id(1) == 0)
    def _():
        h_ref[...] = _rms(x_ref[...], g_ref[...]).astype(BF16)

    o_ref[...] = _dot(h_ref[...], w_ref[...]).astype(o_ref.dtype)


def _norm_mm(x, g_row, w, tm, tn, out_dtype=F32):
    m, k = x.shape
    n = w.shape[1]
    return pl.pallas_call(
        _norm_mm_kernel,
        out_shape=jax.ShapeDtypeStruct((m, n), out_dtype),
        grid=(m // tm, n // tn),
        in_specs=[
            pl.BlockSpec((tm, k), lambda i, j: (i, 0)),
            pl.BlockSpec((1, k), lambda i, j: (0, 0)),
            pl.BlockSpec((k, tn), lambda i, j: (0, j)),
        ],
        out_specs=pl.BlockSpec((tm, tn), lambda i, j: (i, j)),
        scratch_shapes=[pltpu.VMEM((tm, k), BF16)],
        compiler_params=pltpu.CompilerParams(
            dimension_semantics=("parallel", "arbitrary"),
            vmem_limit_bytes=VMEM_LIMIT),
        name="norm_mm",
    )(x, g_row, w)


IN_TM = 512
IN_TN = 512


def _inproj_kernel(x_ref, g_ref, w_ref, wg_ref, proj_ref, gate_ref):
    h = _rms(x_ref[...], g_ref[...]).astype(BF16)
    for n in range(0, N_MAIN, IN_TN):
        proj_ref[:, n:n + IN_TN] = _dot(h, w_ref[:, n:n + IN_TN]).astype(BF16)
    gate_ref[...] = _dot(h, wg_ref[...])


def _inproj(x2d, g_row, w_main, w_gate):
    return pl.pallas_call(
        _inproj_kernel,
        out_shape=(jax.ShapeDtypeStruct((TOKENS, N_MAIN), BF16),
                   jax.ShapeDtypeStruct((TOKENS, LANES), F32)),
        grid=(TOKENS // IN_TM,),
        in_specs=[
            pl.BlockSpec((IN_TM, D_MODEL), lambda i: (i, 0)),
            pl.BlockSpec((1, D_MODEL), lambda i: (0, 0)),
            pl.BlockSpec((D_MODEL, N_MAIN), lambda i: (0, 0)),
            pl.BlockSpec((D_MODEL, LANES), lambda i: (0, 0)),
        ],
        out_specs=(pl.BlockSpec((IN_TM, N_MAIN), lambda i: (i, 0)),
                   pl.BlockSpec((IN_TM, LANES), lambda i: (i, 0))),
        compiler_params=pltpu.CompilerParams(
            dimension_semantics=("parallel",), vmem_limit_bytes=VMEM_LIMIT),
        name="in_proj",
    )(x2d, g_row, w_main, w_gate)


AUX_PAIR_LANES = 16


def _gate_route_matrix():
    r = np.zeros((3 * LANES, LANES), np.float32)
    for h in range(ATTN_HEADS):
        for m in range(3):
            base = AUX_PAIR_LANES * (h // 2)
            r[m * LANES + h, base + 2 * m + (h % 2)] = 1.0
            r[m * LANES + h, base + 6 + 2 * m + (h % 2)] = -1.0
    return r


def _gates_kernel(g_ref, fb_ref, r_ref, aux_ref):
    row = lax.broadcasted_iota(jnp.int32, (CHUNK, CHUNK), 0)
    col = lax.broadcasted_iota(jnp.int32, (CHUNK, CHUNK), 1)
    tril = jnp.where(row >= col, 1.0, 0.0).astype(BF16)
    fb = fb_ref[...]
    rmat = r_ref[...]
    offset = jnp.zeros((1, LANES), F32)
    for blk in range(SEQ // CHUNK):
        rows = slice(blk * CHUNK, (blk + 1) * CHUNK)
        log_f = -_softplus(-(g_ref[rows, :] + fb))
        part = _dot(tril, _split3(log_f))
        local = part[:, :LANES] + part[:, LANES:2 * LANES] + part[:, 2 * LANES:]
        cum = local + offset
        offset = cum[CHUNK - 1:CHUNK, :]
        aux_ref[rows, :] = _dot(_split3(cum), rmat).astype(BF16)


def _gates(gates_raw, fb_row, rmat):
    return pl.pallas_call(
        _gates_kernel,
        out_shape=jax.ShapeDtypeStruct((TOKENS, LANES), BF16),
        grid=(BATCH,),
        in_specs=[
            pl.BlockSpec((SEQ, LANES), lambda b: (b, 0)),
            pl.BlockSpec((1, LANES), lambda b: (0, 0)),
            pl.BlockSpec((3 * LANES, LANES), lambda b: (0, 0)),
        ],
        out_specs=pl.BlockSpec((SEQ, LANES), lambda b: (b, 0)),
        compiler_params=pltpu.CompilerParams(dimension_semantics=("parallel",)),
        name="gates",
    )(gates_raw, fb_row, rmat)


def _head_expand_matrix():
    e = np.zeros((2 * LANES, SSM_D_INNER), np.float32)
    for h in range(SSM_HEADS):
        e[DT_LANE + h, h * SSM_HEAD_DIM:(h + 1) * SSM_HEAD_DIM] = 1.0
        e[LANES + DT_LANE + h, h * SSM_HEAD_DIM:(h + 1) * SSM_HEAD_DIM] = 1.0
    return e


def _ssd_kernel(z_ref, xs_ref, bc_ref, g_ref, cwx_ref, cwb_ref, cbx_ref, cbb_ref,
                dtb_ref, alog_ref, ee_ref, dsk_ref, nw_ref, y_ref,
                xbuf_ref, bbuf_ref, state_ref):
    c = pl.program_id(1)

    @pl.when(c == 0)
    def _():
        xbuf_ref[0:8, :] = jnp.zeros((8, SSM_D_INNER), F32)
        bbuf_ref[0:8, :] = jnp.zeros((8, 2 * SSM_GROUPS * SSM_STATE), F32)
        state_ref[...] = jnp.zeros_like(state_ref)

    def conv_silu(u_ref, buf_ref, w_ref, b_ref):
        buf_ref[8:8 + CHUNK, :] = u_ref[...].astype(F32)
        acc = b_ref[...] + w_ref[3:4, :] * buf_ref[8:8 + CHUNK, :]
        for tap in range(3):
            acc = acc + w_ref[tap:tap + 1, :] * buf_ref[5 + tap:5 + tap + CHUNK, :]
        buf_ref[0:8, :] = buf_ref[CHUNK:CHUNK + 8, :]
        return _silu(acc)

    xs = conv_silu(xs_ref, xbuf_ref, cwx_ref, cbx_ref)
    bc = conv_silu(bc_ref, bbuf_ref, cwb_ref, cbb_ref)

    dt = _softplus(g_ref[...] + dtb_ref[...])
    da = dt * (-jnp.exp(alog_ref[...]))
    row = lax.broadcasted_iota(jnp.int32, (CHUNK, CHUNK), 0)
    col = lax.broadcasted_iota(jnp.int32, (CHUNK, CHUNK), 1)
    causal = row >= col
    tril = jnp.where(causal, 1.0, 0.0).astype(BF16)
    part = _dot(tril, _split3(da))
    acs = part[:, :LANES] + part[:, LANES:2 * LANES] + part[:, 2 * LANES:]
    a_last = acs[CHUNK - 1:CHUNK, :]
    exp_a = jnp.exp(acs)
    dt_decay = dt * jnp.exp(a_last - acs)
    ee = ee_ref[...]
    e_dtdec = _dot(_split2(dt_decay), ee)
    e_expa = _dot(_split2(exp_a), ee)
    acs_t = acs.T
    dt_t = dt.T

    xs_b = xs.astype(BF16)
    xdec_b = (xs * e_dtdec).astype(BF16)
    lane = lax.broadcasted_iota(jnp.int32, (CHUNK, LANES), 1)
    first_head = lane < SSM_HEAD_DIM

    y_parts = []
    for g in range(SSM_GROUPS):
        b_g = bc[:, g * SSM_STATE:(g + 1) * SSM_STATE]
        c_g = bc[:, (SSM_GROUPS + g) * SSM_STATE:(SSM_GROUPS + g + 1) * SSM_STATE]
        c_gb = c_g.astype(BF16)
        cb = _dot_nt(c_gb, b_g.astype(BF16))
        state = state_ref[g]
        cols = slice(g * GROUP_W, (g + 1) * GROUP_W)
        y_off = _dot(c_gb, state.astype(BF16)) * e_expa[:, cols]
        st_new = _dot(b_g.T.astype(BF16), xdec_b[:, cols])
        state_ref[g] = state * e_expa[CHUNK - 1:CHUNK, cols] + st_new
        diag = []
        for pair in range(GROUP_W // LANES):
            x_pair = xs_b[:, g * GROUP_W + pair * LANES:g * GROUP_W + (pair + 1) * LANES]
            res = []
            for j in range(2):
                h = g * (SSM_HEADS // SSM_GROUPS) + 2 * pair + j
                hl = DT_LANE + h
                a_col = jnp.broadcast_to(acs[:, hl:hl + 1], (CHUNK, CHUNK))
                seg = jnp.where(causal, a_col - acs_t[hl:hl + 1, :], -jnp.inf)
                m_h = cb * jnp.exp(seg) * dt_t[hl:hl + 1, :]
                res.append(_dot(m_h.astype(BF16), x_pair))
            diag.append(jnp.where(first_head, res[0], res[1]))
        y_parts.append(jnp.concatenate(diag, axis=-1) + y_off)
    y = jnp.concatenate(y_parts, axis=-1) + dsk_ref[...] * xs
    y = y * _silu(z_ref[...].astype(F32))
    normed = []
    for g in range(SSM_GROUPS):
        y_g = y[:, g * GROUP_W:(g + 1) * GROUP_W]
        normed.append(y_g * lax.rsqrt(jnp.mean(y_g * y_g, axis=-1, keepdims=True) + EPS))
    y_ref[...] = (jnp.concatenate(normed, axis=-1) * nw_ref[...]).astype(BF16)


def _ssd(proj, gates_raw, cw_x, cw_bc, cb_x, cb_bc, dtb_row, alog_row, ee, dsk_row, nw_row):
    nchunks = SEQ // CHUNK
    rowblk = lambda b, c: b * nchunks + c
    full = lambda shape: pl.BlockSpec(shape, lambda b, c: (0, 0))
    return pl.pallas_call(
        _ssd_kernel,
        out_shape=jax.ShapeDtypeStruct((TOKENS, SSM_D_INNER), BF16),
        grid=(BATCH, nchunks),
        in_specs=[
            pl.BlockSpec((CHUNK, SSM_D_INNER), lambda b, c: (rowblk(b, c), 0)),
            pl.BlockSpec((CHUNK, SSM_D_INNER), lambda b, c: (rowblk(b, c), 4)),
            pl.BlockSpec((CHUNK, GROUP_W), lambda b, c: (rowblk(b, c), 10)),
            pl.BlockSpec((CHUNK, LANES), lambda b, c: (rowblk(b, c), 0)),
            full((CONV_WIDTH, SSM_D_INNER)), full((CONV_WIDTH, GROUP_W)),
            full((1, SSM_D_INNER)), full((1, GROUP_W)),
            full((1, LANES)), full((1, LANES)),
            full((2 * LANES, SSM_D_INNER)),
            full((1, SSM_D_INNER)), full((1, SSM_D_INNER)),
        ],
        out_specs=pl.BlockSpec((CHUNK, SSM_D_INNER), lambda b, c: (rowblk(b, c), 0)),
        scratch_shapes=[
            pltpu.VMEM((CHUNK + 8, SSM_D_INNER), F32),
            pltpu.VMEM((CHUNK + 8, GROUP_W), F32),
            pltpu.VMEM((SSM_GROUPS, SSM_STATE, GROUP_W), F32),
        ],
        compiler_params=pltpu.CompilerParams(
            dimension_semantics=("parallel", "arbitrary"),
            vmem_limit_bytes=VMEM_LIMIT),
        name="ssd",
    )(proj, proj, proj, gates_raw, cw_x, cw_bc, cb_x, cb_bc, dtb_row, alog_row, ee, dsk_row, nw_row)


ATT_T = 256
ATT_K = 2 * LANES


def _attn_kernel(q_ref, k_ref, v_ref, aux_ref, gq_ref, gk_ref, o_ref, kt_ref):
    nblk = SEQ // ATT_T
    pair = pl.program_id(1)
    lane = lax.broadcasted_iota(jnp.int32, (ATT_T, LANES), 1)
    r2 = lax.broadcasted_iota(jnp.int32, (2 * LANES, LANES), 0)
    c2 = lax.broadcasted_iota(jnp.int32, (2 * LANES, LANES), 1)
    bd2 = jnp.where(((r2 & (LANES - 1)) >> 6) == (c2 >> 6), 1.0, 0.0).astype(BF16)
    sub = lane & (AUX_PAIR_LANES - 1)
    in_pair = (lane >> 4) == pair
    is_val = in_pair & (sub < 6)
    is_neg = in_pair & (sub >= 6) & (sub < 12)
    scale = ATTN_HEAD_DIM ** -0.5
    rowi = lax.broadcasted_iota(jnp.int32, (ATT_T, ATT_T), 0)
    coli = lax.broadcasted_iota(jnp.int32, (ATT_T, ATT_T), 1)
    causal = rowi >= coli

    def head_norm(u, g_row):
        ssq = _dot(_split2(u * u), bd2)
        return u * lax.rsqrt(ssq * (1.0 / ATTN_HEAD_DIM) + EPS) * g_row

    for i in range(nblk):
        rows = slice(i * ATT_T, (i + 1) * ATT_T)
        qn = head_norm(q_ref[rows, :].astype(F32), gq_ref[...]) * scale
        kn = head_norm(k_ref[rows, :].astype(F32), gk_ref[...])
        aux = aux_ref[rows, :].astype(F32)
        k_aux = jnp.where(is_neg, aux, jnp.where(is_val, 1.0, 0.0))
        kt_ref[i] = jnp.concatenate([kn, k_aux], axis=-1).T.astype(BF16)
        outs = []
        for j in range(2):
            mine = (sub & 1) == j
            q_aux = jnp.where(is_val & mine, aux, jnp.where(is_neg & mine, 1.0, 0.0))
            q_main = jnp.where((lane >> 6) == j, qn, 0.0)
            qa = jnp.concatenate([q_main, q_aux], axis=-1).astype(BF16)
            s = [_dot(qa, kt_ref[t]) for t in range(i)]
            s.append(jnp.where(causal, _dot(qa, kt_ref[i]), NEG_BIG))
            tile_max = functools.reduce(jnp.maximum, s)
            m = jnp.max(tile_max, axis=-1, keepdims=True)
            p = [jnp.exp(s_t - m) for s_t in s]
            denom = jnp.sum(functools.reduce(jnp.add, p), axis=-1, keepdims=True)
            acc = functools.reduce(jnp.add, [
                _dot(p[t].astype(BF16), v_ref[t * ATT_T:(t + 1) * ATT_T, :]) for t in range(i + 1)])
            outs.append(acc / denom)
        o_ref[rows, :] = jnp.where(lane < ATTN_HEAD_DIM, outs[0], outs[1]).astype(BF16)


def _attention(proj, aux, gq_row, gk_row):
    qcol = ATTN_WIDTH // LANES
    return pl.pallas_call(
        _attn_kernel,
        out_shape=jax.ShapeDtypeStruct((TOKENS, ATTN_WIDTH), BF16),
        grid=(BATCH, HEAD_PAIRS),
        in_specs=[
            pl.BlockSpec((SEQ, LANES), lambda b, hp: (b, qcol + hp)),
            pl.BlockSpec((SEQ, LANES), lambda b, hp: (b, 2 * qcol + hp)),
            pl.BlockSpec((SEQ, LANES), lambda b, hp: (b, 3 * qcol + hp)),
            pl.BlockSpec((SEQ, LANES), lambda b, hp: (b, 0)),
            pl.BlockSpec((1, LANES), lambda b, hp: (0, 0)),
            pl.BlockSpec((1, LANES), lambda b, hp: (0, 0)),
        ],
        out_specs=pl.BlockSpec((SEQ, LANES), lambda b, hp: (b, hp)),
        scratch_shapes=[pltpu.VMEM((SEQ // ATT_T, ATT_K, ATT_T), BF16)],
        compiler_params=pltpu.CompilerParams(
            dimension_semantics=("parallel", "parallel"),
            vmem_limit_bytes=VMEM_LIMIT),
        name="fox_attention",
    )(proj, proj, proj, aux, gq_row, gk_row)


OUT_TN = 256


def _outproj_kernel(y_ref, o_ref, x_ref, w1_ref, w2_ref, out_ref):
    y = y_ref[...]
    o = o_ref[...]
    for n in range(0, D_MODEL, OUT_TN):
        cols = slice(n, n + OUT_TN)
        out_ref[:, cols] = x_ref[:, cols] + _dot(y, w1_ref[:, cols]) + _dot(o, w2_ref[:, cols])


def _outproj(y, o, x2d, w1, w2, tm=512):
    return pl.pallas_call(
        _outproj_kernel,
        out_shape=jax.ShapeDtypeStruct((TOKENS, D_MODEL), F32),
        grid=(TOKENS // tm,),
        in_specs=[
            pl.BlockSpec((tm, SSM_D_INNER), lambda i: (i, 0)),
            pl.BlockSpec((tm, ATTN_WIDTH), lambda i: (i, 0)),
            pl.BlockSpec((tm, D_MODEL), lambda i: (i, 0)),
            pl.BlockSpec((SSM_D_INNER, D_MODEL), lambda i: (0, 0)),
            pl.BlockSpec((ATTN_WIDTH, D_MODEL), lambda i: (0, 0)),
        ],
        out_specs=pl.BlockSpec((tm, D_MODEL), lambda i: (i, 0)),
        compiler_params=pltpu.CompilerParams(
            dimension_semantics=("parallel",), vmem_limit_bytes=VMEM_LIMIT),
        name="out_proj",
    )(y, o, x2d, w1, w2)


def _xattn_kernel(x_ref, g_ref, wq_ref, kv_ref, gq_ref, gk_ref, wo_ref, out_ref, o_scr):
    x = x_ref[...]
    h = _rms(x, g_ref[...]).astype(BF16)
    scale = XATTN_HEAD_DIM ** -0.5
    for a in range(XATTN_HEADS):
        cols = slice(a * XATTN_HEAD_DIM, (a + 1) * XATTN_HEAD_DIM)
        q = _dot(h, wq_ref[:, cols])
        qn = (_rms(q, gq_ref[...]) * scale).astype(BF16)
        kn = _rms(kv_ref[:, cols], gk_ref[...]).astype(BF16)
        v = kv_ref[:, D_MODEL + a * XATTN_HEAD_DIM:D_MODEL + (a + 1) * XATTN_HEAD_DIM].astype(BF16)
        s = _dot_nt(qn, kn)
        e = jnp.exp(s - jnp.max(s, axis=-1, keepdims=True))
        p = e / jnp.sum(e, axis=-1, keepdims=True)
        o_scr[:, cols] = _dot(p.astype(BF16), v).astype(BF16)
    o = o_scr[...]
    for n in range(0, D_MODEL, OUT_TN):
        cols = slice(n, n + OUT_TN)
        out_ref[:, cols] = x_ref[:, cols] + _dot(o, wo_ref[:, cols])


def _xattn(x1, g_row, wq, kv, gq_row, gk_row, wo, tm=512):
    nt = SEQ // tm
    return pl.pallas_call(
        _xattn_kernel,
        out_shape=jax.ShapeDtypeStruct((TOKENS, D_MODEL), F32),
        grid=(BATCH, nt),
        in_specs=[
            pl.BlockSpec((tm, D_MODEL), lambda b, i: (b * nt + i, 0)),
            pl.BlockSpec((1, D_MODEL), lambda b, i: (0, 0)),
            pl.BlockSpec((D_MODEL, D_MODEL), lambda b, i: (0, 0)),
            pl.BlockSpec((MEM_LEN, 2 * D_MODEL), lambda b, i: (b, 0)),
            pl.BlockSpec((1, XATTN_HEAD_DIM), lambda b, i: (0, 0)),
            pl.BlockSpec((1, XATTN_HEAD_DIM), lambda b, i: (0, 0)),
            pl.BlockSpec((D_MODEL, D_MODEL), lambda b, i: (0, 0)),
        ],
        out_specs=pl.BlockSpec((tm, D_MODEL), lambda b, i: (b * nt + i, 0)),
        scratch_shapes=[pltpu.VMEM((tm, D_MODEL), BF16)],
        compiler_params=pltpu.CompilerParams(
            dimension_semantics=("parallel", "parallel"), vmem_limit_bytes=VMEM_LIMIT),
        name="mem_xattn",
    )(x1, g_row, wq, kv, gq_row, gk_row, wo)


FF_CHUNK = 1024


def _mlp_kernel(x_ref, g_ref, wu_ref, wd_ref, out_ref, h_scr, acc_scr):
    h_scr[...] = _rms(x_ref[...], g_ref[...]).astype(BF16)
    acc_scr[...] = x_ref[...]
    for f in range(0, D_FF, FF_CHUNK):
        u = jnp.maximum(_dot(h_scr[...], wu_ref[:, f:f + FF_CHUNK]), 0.0)
        acc_scr[...] += _dot((u * u).astype(BF16), wd_ref[f:f + FF_CHUNK, :])
    out_ref[...] = acc_scr[...]


def _mlp(x2, g_row, wu, wd, tm=512):
    return pl.pallas_call(
        _mlp_kernel,
        out_shape=jax.ShapeDtypeStruct((TOKENS, D_MODEL), F32),
        grid=(TOKENS // tm,),
        in_specs=[
            pl.BlockSpec((tm, D_MODEL), lambda i: (i, 0)),
            pl.BlockSpec((1, D_MODEL), lambda i: (0, 0)),
            pl.BlockSpec((D_MODEL, D_FF), lambda i: (0, 0)),
            pl.BlockSpec((D_FF, D_MODEL), lambda i: (0, 0)),
        ],
        out_specs=pl.BlockSpec((tm, D_MODEL), lambda i: (i, 0)),
        scratch_shapes=[pltpu.VMEM((tm, D_MODEL), BF16), pltpu.VMEM((tm, D_MODEL), F32)],
        compiler_params=pltpu.CompilerParams(
            dimension_semantics=("parallel",), vmem_limit_bytes=VMEM_LIMIT),
        name="relu2_mlp",
    )(x2, g_row, wu, wd)


def _lane_row(vec, offset):
    return jnp.zeros((1, LANES), F32).at[0, offset:offset + vec.shape[0]].set(vec.astype(F32))


def _layer(x2d, mem2d, g_mix, w_in, conv_w, conv_b, dt_bias, a_log, d_skip, ssm_norm_w,
           g_q, g_k, f_bias, w_out, g_xattn, g_mem, xq_w, xkv_w, xg_q, xg_k, xo_w,
           g_mlp, w_up, w_down):
    z0, xbc0 = 0, SSM_D_INNER
    bc0 = xbc0 + SSM_D_INNER
    dt0 = bc0 + 2 * SSM_GROUPS * SSM_STATE
    q0 = dt0 + SSM_HEADS
    k0 = q0 + ATTN_WIDTH
    v0 = k0 + ATTN_WIDTH
    f0 = v0 + ATTN_WIDTH
    w_main = jnp.concatenate(
        [w_in[:, z0:xbc0], w_in[:, q0:f0], w_in[:, xbc0:dt0]], axis=1).astype(BF16)
    w_gate = jnp.concatenate(
        [w_in[:, f0:f0 + ATTN_HEADS], w_in[:, dt0:q0],
         jnp.zeros((D_MODEL, LANES - ATTN_HEADS - SSM_HEADS), F32)], axis=1).astype(BF16)
    row = lambda v: v.astype(F32).reshape(1, -1)

    proj, gates_raw = _inproj(x2d, row(g_mix), w_main, w_gate)

    aux = _gates(gates_raw, _lane_row(f_bias, 0), jnp.asarray(_gate_route_matrix(), BF16))

    y = _ssd(proj, gates_raw,
             conv_w[:, :SSM_D_INNER], conv_w[:, SSM_D_INNER:],
             row(conv_b[:SSM_D_INNER]), row(conv_b[SSM_D_INNER:]),
             _lane_row(dt_bias, DT_LANE), _lane_row(a_log, DT_LANE),
             jnp.asarray(_head_expand_matrix(), BF16),
             row(jnp.repeat(d_skip, SSM_HEAD_DIM)), row(ssm_norm_w))

    o = _attention(proj, aux, row(jnp.tile(g_q, 2)), row(jnp.tile(g_k, 2)))

    x1 = _outproj(y, o, x2d, w_out[:SSM_D_INNER].astype(BF16), w_out[SSM_D_INNER:].astype(BF16))

    kv = _norm_mm(mem2d, row(g_mem), xkv_w.astype(BF16), tm=1024, tn=512)
    x2 = _xattn(x1, row(g_xattn), xq_w.astype(BF16), kv, row(xg_q), row(xg_k), xo_w.astype(BF16))

    return _mlp(x2, row(g_mlp), w_up.astype(BF16), w_down.astype(BF16))


def kernel(x, mem, g_mix, w_in, conv_w, conv_b, dt_bias, a_log, d_skip, ssm_norm_w, g_q, g_k,
           f_bias, w_out, g_xattn, g_mem, xq_w, xkv_w, xg_q, xg_k, xo_w, g_mlp, w_up, w_down):
    x2d = x.reshape(TOKENS, D_MODEL)
    mem2d = mem.reshape(BATCH * MEM_LEN, D_MODEL)
    depth = g_mix.shape[0]
    for l in range(depth):
        x2d = _layer(x2d, mem2d, g_mix[l], w_in[l], conv_w[l], conv_b[l], dt_bias[l], a_log[l],
                     d_skip[l], ssm_norm_w[l], g_q[l], g_k[l], f_bias[l], w_out[l], g_xattn[l],
                     g_mem[l], xq_w[l], xkv_w[l], xg_q[l], xg_k[l], xo_w[l], g_mlp[l], w_up[l],
                     w_down[l])
    return x2d.reshape(BATCH, SEQ, D_MODEL)
```

```python
import functools

import numpy as np
import jax
import jax.numpy as jnp
from jax import lax
from jax.experimental import pallas as pl
from jax.experimental.pallas import tpu as pltpu

F32 = jnp.float32
BF16 = jnp.bfloat16

D_MODEL = 1024
BATCH = 8
SEQ = 2048
TOKENS = BATCH * SEQ
MEM_LEN = 256
SSM_HEAD_DIM = 64
SSM_HEADS = 16
SSM_D_INNER = 1024
SSM_GROUPS = 2
SSM_STATE = 128
CONV_WIDTH = 4
CHUNK = 128
ATTN_HEAD_DIM = 64
ATTN_HEADS = 16
ATTN_WIDTH = 1024
XATTN_HEADS = 4
XATTN_HEAD_DIM = 256
D_FF = 4096
EPS = 1e-5

LANES = 128
N_MAIN = 5632
HEAD_PAIRS = ATTN_HEADS // 2
GROUP_W = SSM_D_INNER // SSM_GROUPS
DT_LANE = 16
VMEM_LIMIT = 56 * 1024 * 1024
NEG_BIG = -1e30
LOG2E = 1.4426950408889634


def _rms(xf, g_row):
    ms = jnp.mean(xf * xf, axis=-1, keepdims=True)
    return xf * lax.rsqrt(ms + EPS) * g_row


def _split2(a):
    hi = a.astype(BF16)
    mid = (a - hi.astype(F32)).astype(BF16)
    return jnp.concatenate([hi, mid], axis=-1)


def _split3(a):
    hi = a.astype(BF16)
    r1 = a - hi.astype(F32)
    mid = r1.astype(BF16)
    lo = (r1 - mid.astype(F32)).astype(BF16)
    return jnp.concatenate([hi, mid, lo], axis=-1)


def _softplus(x):
    return jnp.maximum(x, 0.0) + jnp.log1p(jnp.exp(-jnp.abs(x)))


def _silu(x):
    half = 0.5 * x
    return half + half * jnp.tanh(half)


def _dot(a, b):
    return jnp.dot(a, b, preferred_element_type=F32)


def _dot_nt(a, b):
    return lax.dot_general(a, b, (((1,), (1,)), ((), ())), preferred_element_type=F32)


IN_TM = 512
IN_TN = 512


def _inproj_kernel(x_ref, g_ref, w_ref, wg_ref, proj_ref, gate_ref):
    h = _rms(x_ref[...], g_ref[...]).astype(BF16)
    for n in range(0, N_MAIN, IN_TN):
        proj_ref[:, n:n + IN_TN] = _dot(h, w_ref[:, n:n + IN_TN]).astype(BF16)
    gate_ref[...] = _dot(h, wg_ref[...])


def _inproj(x2d, g_row, w_main, w_gate):
    return pl.pallas_call(
        _inproj_kernel,
        out_shape=(jax.ShapeDtypeStruct((TOKENS, N_MAIN), BF16),
                   jax.ShapeDtypeStruct((TOKENS, LANES), F32)),
        grid=(TOKENS // IN_TM,),
        in_specs=[
            pl.BlockSpec((IN_TM, D_MODEL), lambda i: (i, 0)),
            pl.BlockSpec((1, D_MODEL), lambda i: (0, 0)),
            pl.BlockSpec((D_MODEL, N_MAIN), lambda i: (0, 0)),
            pl.BlockSpec((D_MODEL, LANES), lambda i: (0, 0)),
        ],
        out_specs=(pl.BlockSpec((IN_TM, N_MAIN), lambda i: (i, 0)),
                   pl.BlockSpec((IN_TM, LANES), lambda i: (i, 0))),
        compiler_params=pltpu.CompilerParams(
            dimension_semantics=("parallel",), vmem_limit_bytes=VMEM_LIMIT),
        name="in_proj",
    )(x2d, g_row, w_main, w_gate)


AUX_PAIR_LANES = 16


def _gate_route_matrix():
    r = np.zeros((3 * LANES, LANES), np.float32)
    for h in range(ATTN_HEADS):
        for m in range(3):
            base = AUX_PAIR_LANES * (h // 2)
            r[m * LANES + h, base + 2 * m + (h % 2)] = 1.0
            r[m * LANES + h, base + 6 + 2 * m + (h % 2)] = -1.0
    return r


def _gates_kernel(g_ref, fb_ref, r_ref, aux_ref):
    row = lax.broadcasted_iota(jnp.int32, (CHUNK, CHUNK), 0)
    col = lax.broadcasted_iota(jnp.int32, (CHUNK, CHUNK), 1)
    tril = jnp.where(row >= col, 1.0, 0.0).astype(BF16)
    fb = fb_ref[...]
    rmat = r_ref[...]
    offset = jnp.zeros((1, LANES), F32)
    for blk in range(SEQ // CHUNK):
        rows = slice(blk * CHUNK, (blk + 1) * CHUNK)
        log_f = -_softplus(-(g_ref[rows, :] + fb))
        part = _dot(tril, _split3(log_f))
        local = part[:, :LANES] + part[:, LANES:2 * LANES] + part[:, 2 * LANES:]
        cum = local + offset
        offset = cum[CHUNK - 1:CHUNK, :]
        aux_ref[rows, :] = _dot(_split3(cum * LOG2E), rmat).astype(BF16)


def _gates(gates_raw, fb_row, rmat):
    return pl.pallas_call(
        _gates_kernel,
        out_shape=jax.ShapeDtypeStruct((TOKENS, LANES), BF16),
        grid=(BATCH,),
        in_specs=[
            pl.BlockSpec((SEQ, LANES), lambda b: (b, 0)),
            pl.BlockSpec((1, LANES), lambda b: (0, 0)),
            pl.BlockSpec((3 * LANES, LANES), lambda b: (0, 0)),
        ],
        out_specs=pl.BlockSpec((SEQ, LANES), lambda b: (b, 0)),
        compiler_params=pltpu.CompilerParams(dimension_semantics=("parallel",)),
        name="gates",
    )(gates_raw, fb_row, rmat)


def _head_expand_matrix():
    e = np.zeros((2 * LANES, SSM_D_INNER), np.float32)
    for h in range(SSM_HEADS):
        e[DT_LANE + h, h * SSM_HEAD_DIM:(h + 1) * SSM_HEAD_DIM] = 1.0
        e[LANES + DT_LANE + h, h * SSM_HEAD_DIM:(h + 1) * SSM_HEAD_DIM] = 1.0
    return e


CONV_TAIL = 16


def _conv_shift_matrix():
    s = np.zeros(((CONV_WIDTH - 1) * CHUNK, CHUNK + CONV_TAIL), np.float32)
    for k in range(CONV_WIDTH - 1):
        for t in range(CHUNK):
            src = t - 1 - k
            s[k * CHUNK + t, src if src >= 0 else CHUNK + CONV_TAIL + src] = 1.0
    return s


def _ssd_kernel(z_ref, xs_ref, bc_ref, g_ref, cwx_ref, cwb_ref, cbx_ref, cbb_ref,
                dtb_ref, alog_ref, ee_ref, shift_ref, dsk_ref, nw_ref, y_ref,
                xtail_ref, btail_ref, state_ref):
    c = pl.program_id(1)

    @pl.when(c == 0)
    def _():
        xtail_ref[...] = jnp.zeros_like(xtail_ref)
        btail_ref[...] = jnp.zeros_like(btail_ref)
        state_ref[...] = jnp.zeros_like(state_ref)

    def conv_silu(u_ref, tail_ref, w_ref, b_ref):
        cur = u_ref[...]
        shifted = _dot(shift_ref[...], jnp.concatenate([cur, tail_ref[...]], axis=0))
        tail_ref[...] = cur[CHUNK - CONV_TAIL:, :]
        acc = b_ref[...] + w_ref[3:4, :] * cur.astype(F32)
        for k in range(CONV_WIDTH - 1):
            acc = acc + w_ref[2 - k:3 - k, :] * shifted[k * CHUNK:(k + 1) * CHUNK, :]
        return _silu(acc)

    xs = conv_silu(xs_ref, xtail_ref, cwx_ref, cbx_ref)
    bc = conv_silu(bc_ref, btail_ref, cwb_ref, cbb_ref)

    dt = _softplus(g_ref[...] + dtb_ref[...])
    da = dt * (-jnp.exp(alog_ref[...]))
    row = lax.broadcasted_iota(jnp.int32, (CHUNK, CHUNK), 0)
    col = lax.broadcasted_iota(jnp.int32, (CHUNK, CHUNK), 1)
    causal = row >= col
    tril = jnp.where(causal, 1.0, 0.0).astype(BF16)
    part = _dot(tril, _split3(da))
    acs = part[:, :LANES] + part[:, LANES:2 * LANES] + part[:, 2 * LANES:]
    a_last = acs[CHUNK - 1:CHUNK, :]
    exp_a = jnp.exp(acs)
    dt_decay = dt * jnp.exp(a_last - acs)
    ee = ee_ref[...]
    e_dtdec = _dot(_split2(dt_decay), ee)
    e_expa = _dot(_split2(exp_a), ee)
    acs_t = acs.T
    dt_t = dt.T

    xs_b = xs.astype(BF16)
    xdec_b = (xs * e_dtdec).astype(BF16)
    lane = lax.broadcasted_iota(jnp.int32, (CHUNK, LANES), 1)
    first_head = lane < SSM_HEAD_DIM

    y_parts = []
    for g in range(SSM_GROUPS):
        b_g = bc[:, g * SSM_STATE:(g + 1) * SSM_STATE]
        c_g = bc[:, (SSM_GROUPS + g) * SSM_STATE:(SSM_GROUPS + g + 1) * SSM_STATE]
        c_gb = c_g.astype(BF16)
        cb = _dot_nt(c_gb, b_g.astype(BF16))
        state = state_ref[g]
        cols = slice(g * GROUP_W, (g + 1) * GROUP_W)
        y_off = _dot(c_gb, state.astype(BF16)) * e_expa[:, cols]
        st_new = _dot(b_g.T.astype(BF16), xdec_b[:, cols])
        state_ref[g] = state * e_expa[CHUNK - 1:CHUNK, cols] + st_new
        diag = []
        for pair in range(GROUP_W // LANES):
            x_pair = xs_b[:, g * GROUP_W + pair * LANES:g * GROUP_W + (pair + 1) * LANES]
            res = []
            for j in range(2):
                h = g * (SSM_HEADS // SSM_GROUPS) + 2 * pair + j
                hl = DT_LANE + h
                a_col = jnp.broadcast_to(acs[:, hl:hl + 1], (CHUNK, CHUNK))
                seg = jnp.where(causal, a_col - acs_t[hl:hl + 1, :], -jnp.inf)
                m_h = cb * jnp.exp(seg) * dt_t[hl:hl + 1, :]
                res.append(_dot(m_h.astype(BF16), x_pair))
            diag.append(jnp.where(first_head, res[0], res[1]))
        y_parts.append(jnp.concatenate(diag, axis=-1) + y_off)
    y = jnp.concatenate(y_parts, axis=-1) + dsk_ref[...] * xs
    y = y * _silu(z_ref[...].astype(F32))
    normed = []
    for g in range(SSM_GROUPS):
        y_g = y[:, g * GROUP_W:(g + 1) * GROUP_W]
        normed.append(y_g * lax.rsqrt(jnp.mean(y_g * y_g, axis=-1, keepdims=True) + EPS))
    y_ref[...] = (jnp.concatenate(normed, axis=-1) * nw_ref[...]).astype(BF16)


def _ssd(proj, gates_raw, cw_x, cw_bc, cb_x, cb_bc, dtb_row, alog_row, ee, shift, dsk_row, nw_row):
    nchunks = SEQ // CHUNK
    rowblk = lambda b, c: b * nchunks + c
    full = lambda shape: pl.BlockSpec(shape, lambda b, c: (0, 0))
    return pl.pallas_call(
        _ssd_kernel,
        out_shape=jax.ShapeDtypeStruct((TOKENS, SSM_D_INNER), BF16),
        grid=(BATCH, nchunks),
        in_specs=[
            pl.BlockSpec((CHUNK, SSM_D_INNER), lambda b, c: (rowblk(b, c), 0)),
            pl.BlockSpec((CHUNK, SSM_D_INNER), lambda b, c: (rowblk(b, c), 4)),
            pl.BlockSpec((CHUNK, GROUP_W), lambda b, c: (rowblk(b, c), 10)),
            pl.BlockSpec((CHUNK, LANES), lambda b, c: (rowblk(b, c), 0)),
            full((CONV_WIDTH, SSM_D_INNER)), full((CONV_WIDTH, GROUP_W)),
            full((1, SSM_D_INNER)), full((1, GROUP_W)),
            full((1, LANES)), full((1, LANES)),
            full((2 * LANES, SSM_D_INNER)),
            full(((CONV_WIDTH - 1) * CHUNK, CHUNK + CONV_TAIL)),
            full((1, SSM_D_INNER)), full((1, SSM_D_INNER)),
        ],
        out_specs=pl.BlockSpec((CHUNK, SSM_D_INNER), lambda b, c: (rowblk(b, c), 0)),
        scratch_shapes=[
            pltpu.VMEM((CONV_TAIL, SSM_D_INNER), BF16),
            pltpu.VMEM((CONV_TAIL, GROUP_W), BF16),
            pltpu.VMEM((SSM_GROUPS, SSM_STATE, GROUP_W), F32),
        ],
        compiler_params=pltpu.CompilerParams(
            dimension_semantics=("parallel", "arbitrary"),
            vmem_limit_bytes=VMEM_LIMIT),
        name="ssd",
    )(proj, proj, proj, gates_raw, cw_x, cw_bc, cb_x, cb_bc, dtb_row, alog_row, ee, shift, dsk_row,
      nw_row)


ATT_T = 256
ATT_K = 2 * LANES


VT_ROWS = LANES + 16


def _attn_kernel(q_ref, k_ref, v_ref, aux_ref, gq_ref, gk_ref, o_ref, ka_ref, vt_ref, s_ref, m_ref):
    nblk = SEQ // ATT_T
    pair = pl.program_id(1)
    lane = lax.broadcasted_iota(jnp.int32, (ATT_T, LANES), 1)
    r2 = lax.broadcasted_iota(jnp.int32, (2 * LANES, LANES), 0)
    c2 = lax.broadcasted_iota(jnp.int32, (2 * LANES, LANES), 1)
    bd2 = jnp.where(((r2 & (LANES - 1)) >> 6) == (c2 >> 6), 1.0, 0.0).astype(BF16)
    sub = lane & (AUX_PAIR_LANES - 1)
    in_pair = (lane >> 4) == pair
    is_val = in_pair & (sub < 6)
    is_neg = in_pair & (sub >= 6) & (sub < 12)
    scale = ATTN_HEAD_DIM ** -0.5 * LOG2E
    krow = lax.broadcasted_iota(jnp.int32, (ATT_T, ATT_T), 0)
    qcol = lax.broadcasted_iota(jnp.int32, (ATT_T, ATT_T), 1)
    causal_t = qcol >= krow
    trow = lax.broadcasted_iota(jnp.int32, (LANES, ATT_T), 0)
    tsub = trow & (AUX_PAIR_LANES - 1)
    t_in_pair = (trow >> 4) == pair
    t_val = t_in_pair & (tsub < 6)
    t_neg = t_in_pair & (tsub >= 6) & (tsub < 12)

    def head_norm(u, g_row):
        ssq = _dot(_split2(u * u), bd2)
        return u * lax.rsqrt(ssq * (1.0 / ATTN_HEAD_DIM) + EPS) * g_row

    vt_ref[LANES:, :] = jnp.ones((VT_ROWS - LANES, SEQ), BF16)

    def scores(i):
        rows = slice(i * ATT_T, (i + 1) * ATT_T)
        qn_t = (head_norm(q_ref[rows, :].astype(F32), gq_ref[...]) * scale).T
        kn = head_norm(k_ref[rows, :].astype(F32), gk_ref[...])
        aux = aux_ref[rows, :].astype(F32)
        aux_t = aux.T
        k_aux = jnp.where(is_neg, aux, jnp.where(is_val, 1.0, 0.0))
        ka_ref[rows, :] = jnp.concatenate([kn, k_aux], axis=-1).astype(BF16)
        vt_ref[:LANES, rows] = v_ref[rows, :].astype(F32).T.astype(BF16)
        for j in range(2):
            mine = (tsub & 1) == j
            q_aux_t = jnp.where(t_val & mine, aux_t, jnp.where(t_neg & mine, 1.0, 0.0))
            q_main_t = jnp.where((trow >> 6) == j, qn_t, 0.0)
            qa_t = jnp.concatenate([q_main_t, q_aux_t], axis=0).astype(BF16)
            s = [_dot(ka_ref[t * ATT_T:(t + 1) * ATT_T, :], qa_t) for t in range(i)]
            s.append(jnp.where(causal_t, _dot(ka_ref[rows, :], qa_t), NEG_BIG))
            for t in range(i + 1):
                s_ref[i % 2, j, t] = s[t]
            m = jnp.max(functools.reduce(jnp.maximum, s), axis=0, keepdims=True)
            m_ref[i % 2, j] = jnp.broadcast_to(m, (8, ATT_T))

    def softmax_values(i):
        outs = []
        for j in range(2):
            m = m_ref[i % 2, j][0:1, :]
            acc = functools.reduce(jnp.add, [
                _dot(vt_ref[:, t * ATT_T:(t + 1) * ATT_T],
                     jnp.exp2(s_ref[i % 2, j, t] - m).astype(BF16))
                for t in range(i + 1)])
            outs.append(acc[:LANES, :] / acc[LANES:LANES + 1, :])
        o_t = jnp.where(trow < ATTN_HEAD_DIM, outs[0], outs[1])
        o_ref[i * ATT_T:(i + 1) * ATT_T, :] = o_t.T.astype(BF16)

    scores(0)
    for i in range(nblk):
        if i + 1 < nblk:
            scores(i + 1)
        softmax_values(i)


def _attention(proj, aux, gq_row, gk_row):
    qcol = ATTN_WIDTH // LANES
    return pl.pallas_call(
        _attn_kernel,
        out_shape=jax.ShapeDtypeStruct((TOKENS, ATTN_WIDTH), BF16),
        grid=(BATCH, HEAD_PAIRS),
        in_specs=[
            pl.BlockSpec((SEQ, LANES), lambda b, hp: (b, qcol + hp)),
            pl.BlockSpec((SEQ, LANES), lambda b, hp: (b, 2 * qcol + hp)),
            pl.BlockSpec((SEQ, LANES), lambda b, hp: (b, 3 * qcol + hp)),
            pl.BlockSpec((SEQ, LANES), lambda b, hp: (b, 0)),
            pl.BlockSpec((1, LANES), lambda b, hp: (0, 0)),
            pl.BlockSpec((1, LANES), lambda b, hp: (0, 0)),
        ],
        out_specs=pl.BlockSpec((SEQ, LANES), lambda b, hp: (b, hp)),
        scratch_shapes=[
            pltpu.VMEM((SEQ, ATT_K), BF16),
            pltpu.VMEM((VT_ROWS, SEQ), BF16),
            pltpu.VMEM((2, 2, SEQ // ATT_T, ATT_T, ATT_T), F32),
            pltpu.VMEM((2, 2, 8, ATT_T), F32),
        ],
        compiler_params=pltpu.CompilerParams(
            dimension_semantics=("parallel", "parallel"),
            vmem_limit_bytes=VMEM_LIMIT),
        name="fox_attention",
    )(proj, proj, proj, aux, gq_row, gk_row)


OUT_TN = 256


def _outproj_kernel(y_ref, o_ref, x_ref, w1_ref, w2_ref, out_ref):
    y = y_ref[...]
    o = o_ref[...]
    for n in range(0, D_MODEL, OUT_TN):
        cols = slice(n, n + OUT_TN)
        out_ref[:, cols] = x_ref[:, cols] + _dot(y, w1_ref[:, cols]) + _dot(o, w2_ref[:, cols])


def _outproj(y, o, x2d, w1, w2, tm=512):
    return pl.pallas_call(
        _outproj_kernel,
        out_shape=jax.ShapeDtypeStruct((TOKENS, D_MODEL), F32),
        grid=(TOKENS // tm,),
        in_specs=[
            pl.BlockSpec((tm, SSM_D_INNER), lambda i: (i, 0)),
            pl.BlockSpec((tm, ATTN_WIDTH), lambda i: (i, 0)),
            pl.BlockSpec((tm, D_MODEL), lambda i: (i, 0)),
            pl.BlockSpec((SSM_D_INNER, D_MODEL), lambda i: (0, 0)),
            pl.BlockSpec((ATTN_WIDTH, D_MODEL), lambda i: (0, 0)),
        ],
        out_specs=pl.BlockSpec((tm, D_MODEL), lambda i: (i, 0)),
        compiler_params=pltpu.CompilerParams(
            dimension_semantics=("parallel",), vmem_limit_bytes=VMEM_LIMIT),
        name="out_proj",
    )(y, o, x2d, w1, w2)


def _memkv_kernel(m_ref, g_ref, w_ref, gk_ref, knt_ref, v_ref):
    h = _rms(m_ref[...], g_ref[...]).astype(BF16)
    for a in range(XATTN_HEADS):
        cols = slice(a * XATTN_HEAD_DIM, (a + 1) * XATTN_HEAD_DIM)
        kn = _rms(_dot(h, w_ref[:, cols]), gk_ref[...])
        knt_ref[cols, :] = kn.T.astype(BF16)
    for n in range(0, D_MODEL, XATTN_HEAD_DIM):
        v_ref[:, n:n + XATTN_HEAD_DIM] = _dot(
            h, w_ref[:, D_MODEL + n:D_MODEL + n + XATTN_HEAD_DIM]).astype(BF16)


def _memkv(mem2d, g_row, wkv, gk_row):
    return pl.pallas_call(
        _memkv_kernel,
        out_shape=(jax.ShapeDtypeStruct((BATCH * D_MODEL, MEM_LEN), BF16),
                   jax.ShapeDtypeStruct((BATCH * MEM_LEN, D_MODEL), BF16)),
        grid=(BATCH,),
        in_specs=[
            pl.BlockSpec((MEM_LEN, D_MODEL), lambda b: (b, 0)),
            pl.BlockSpec((1, D_MODEL), lambda b: (0, 0)),
            pl.BlockSpec((D_MODEL, 2 * D_MODEL), lambda b: (0, 0)),
            pl.BlockSpec((1, XATTN_HEAD_DIM), lambda b: (0, 0)),
        ],
        out_specs=(pl.BlockSpec((D_MODEL, MEM_LEN), lambda b: (b, 0)),
                   pl.BlockSpec((MEM_LEN, D_MODEL), lambda b: (b, 0))),
        compiler_params=pltpu.CompilerParams(
            dimension_semantics=("parallel",), vmem_limit_bytes=VMEM_LIMIT),
        name="mem_kv",
    )(mem2d, g_row, wkv, gk_row)


def _xattn_kernel(x_ref, g_ref, wq_ref, knt_ref, v_ref, gq_ref, wo_ref, out_ref, o_scr):
    h = _rms(x_ref[...], g_ref[...]).astype(BF16)
    scale = XATTN_HEAD_DIM ** -0.5 * LOG2E
    for a in range(XATTN_HEADS):
        cols = slice(a * XATTN_HEAD_DIM, (a + 1) * XATTN_HEAD_DIM)
        q = _dot(h, wq_ref[:, cols])
        qn = (_rms(q, gq_ref[...]) * scale).astype(BF16)
        s = _dot(qn, knt_ref[cols, :])
        e = jnp.exp2(s - jnp.max(s, axis=-1, keepdims=True))
        p = e / jnp.sum(e, axis=-1, keepdims=True)
        o_scr[:, cols] = _dot(p.astype(BF16), v_ref[:, cols]).astype(BF16)
    o = o_scr[...]
    for n in range(0, D_MODEL, OUT_TN):
        cols = slice(n, n + OUT_TN)
        out_ref[:, cols] = x_ref[:, cols] + _dot(o, wo_ref[:, cols])


def _xattn(x1, g_row, wq, knt, v, gq_row, wo, tm=1024):
    nt = SEQ // tm
    return pl.pallas_call(
        _xattn_kernel,
        out_shape=jax.ShapeDtypeStruct((TOKENS, D_MODEL), F32),
        grid=(BATCH, nt),
        in_specs=[
            pl.BlockSpec((tm, D_MODEL), lambda b, i: (b * nt + i, 0)),
            pl.BlockSpec((1, D_MODEL), lambda b, i: (0, 0)),
            pl.BlockSpec((D_MODEL, D_MODEL), lambda b, i: (0, 0)),
            pl.BlockSpec((D_MODEL, MEM_LEN), lambda b, i: (b, 0)),
            pl.BlockSpec((MEM_LEN, D_MODEL), lambda b, i: (b, 0)),
            pl.BlockSpec((1, XATTN_HEAD_DIM), lambda b, i: (0, 0)),
            pl.BlockSpec((D_MODEL, D_MODEL), lambda b, i: (0, 0)),
        ],
        out_specs=pl.BlockSpec((tm, D_MODEL), lambda b, i: (b * nt + i, 0)),
        scratch_shapes=[pltpu.VMEM((tm, D_MODEL), BF16)],
        compiler_params=pltpu.CompilerParams(
            dimension_semantics=("parallel", "parallel"), vmem_limit_bytes=VMEM_LIMIT),
        name="mem_xattn",
    )(x1, g_row, wq, knt, v, gq_row, wo)


FF_CHUNK = 1024


def _mlp_kernel(x_ref, g_ref, wu_ref, wd_ref, out_ref, h_scr, acc_scr):
    h_scr[...] = _rms(x_ref[...], g_ref[...]).astype(BF16)
    acc_scr[...] = x_ref[...]
    for f in range(0, D_FF, FF_CHUNK):
        u = jnp.maximum(_dot(h_scr[...], wu_ref[:, f:f + FF_CHUNK]), 0.0)
        acc_scr[...] += _dot((u * u).astype(BF16), wd_ref[f:f + FF_CHUNK, :])
    out_ref[...] = acc_scr[...]


def _mlp(x2, g_row, wu, wd, tm=512):
    return pl.pallas_call(
        _mlp_kernel,
        out_shape=jax.ShapeDtypeStruct((TOKENS, D_MODEL), F32),
        grid=(TOKENS // tm,),
        in_specs=[
            pl.BlockSpec((tm, D_MODEL), lambda i: (i, 0)),
            pl.BlockSpec((1, D_MODEL), lambda i: (0, 0)),
            pl.BlockSpec((D_MODEL, D_FF), lambda i: (0, 0)),
            pl.BlockSpec((D_FF, D_MODEL), lambda i: (0, 0)),
        ],
        out_specs=pl.BlockSpec((tm, D_MODEL), lambda i: (i, 0)),
        scratch_shapes=[pltpu.VMEM((tm, D_MODEL), BF16), pltpu.VMEM((tm, D_MODEL), F32)],
        compiler_params=pltpu.CompilerParams(
            dimension_semantics=("parallel",), vmem_limit_bytes=VMEM_LIMIT),
        name="relu2_mlp",
    )(x2, g_row, wu, wd)


def _lane_row(vec, offset):
    return jnp.zeros((1, LANES), F32).at[0, offset:offset + vec.shape[0]].set(vec.astype(F32))


def _layer(x2d, mem2d, g_mix, w_in, conv_w, conv_b, dt_bias, a_log, d_skip, ssm_norm_w,
           g_q, g_k, f_bias, w_out, g_xattn, g_mem, xq_w, xkv_w, xg_q, xg_k, xo_w,
           g_mlp, w_up, w_down):
    z0, xbc0 = 0, SSM_D_INNER
    bc0 = xbc0 + SSM_D_INNER
    dt0 = bc0 + 2 * SSM_GROUPS * SSM_STATE
    q0 = dt0 + SSM_HEADS
    k0 = q0 + ATTN_WIDTH
    v0 = k0 + ATTN_WIDTH
    f0 = v0 + ATTN_WIDTH
    w_main = jnp.concatenate(
        [w_in[:, z0:xbc0], w_in[:, q0:f0], w_in[:, xbc0:dt0]], axis=1).astype(BF16)
    w_gate = jnp.concatenate(
        [w_in[:, f0:f0 + ATTN_HEADS], w_in[:, dt0:q0],
         jnp.zeros((D_MODEL, LANES - ATTN_HEADS - SSM_HEADS), F32)], axis=1).astype(BF16)
    row = lambda v: v.astype(F32).reshape(1, -1)

    proj, gates_raw = _inproj(x2d, row(g_mix), w_main, w_gate)

    aux = _gates(gates_raw, _lane_row(f_bias, 0), jnp.asarray(_gate_route_matrix(), BF16))

    y = _ssd(proj, gates_raw,
             conv_w[:, :SSM_D_INNER], conv_w[:, SSM_D_INNER:],
             row(conv_b[:SSM_D_INNER]), row(conv_b[SSM_D_INNER:]),
             _lane_row(dt_bias, DT_LANE), _lane_row(a_log, DT_LANE),
             jnp.asarray(_head_expand_matrix(), BF16), jnp.asarray(_conv_shift_matrix(), BF16),
             row(jnp.repeat(d_skip, SSM_HEAD_DIM)), row(ssm_norm_w))

    o = _attention(proj, aux, row(jnp.tile(g_q, 2)), row(jnp.tile(g_k, 2)))

    x1 = _outproj(y, o, x2d, w_out[:SSM_D_INNER].astype(BF16), w_out[SSM_D_INNER:].astype(BF16))

    knt, mem_v = _memkv(mem2d, row(g_mem), xkv_w.astype(BF16), row(xg_k))
    x2 = _xattn(x1, row(g_xattn), xq_w.astype(BF16), knt, mem_v, row(xg_q), xo_w.astype(BF16))

    return _mlp(x2, row(g_mlp), w_up.astype(BF16), w_down.astype(BF16))


def kernel(x, mem, g_mix, w_in, conv_w, conv_b, dt_bias, a_log, d_skip, ssm_norm_w, g_q, g_k,
           f_bias, w_out, g_xattn, g_mem, xq_w, xkv_w, xg_q, xg_k, xo_w, g_mlp, w_up, w_down):
    x2d = x.reshape(TOKENS, D_MODEL)
    mem2d = mem.reshape(BATCH * MEM_LEN, D_MODEL)
    depth = g_mix.shape[0]
    for l in range(depth):
        x2d = _layer(x2d, mem2d, g_mix[l], w_in[l], conv_w[l], conv_b[l], dt_bias[l], a_log[l],
                     d_skip[l], ssm_norm_w[l], g_q[l], g_k[l], f_bias[l], w_out[l], g_xattn[l],
                     g_mem[l], xq_w[l], xkv_w[l], xg_q[l], xg_k[l], xo_w[l], g_mlp[l], w_up[l],
                     w_down[l])
    return x2d.reshape(BATCH, SEQ, D_MODEL)
```

```python
import functools

import numpy as np
import jax
import jax.numpy as jnp
from jax import lax
from jax.experimental import pallas as pl
from jax.experimental.pallas import tpu as pltpu

F32 = jnp.float32
BF16 = jnp.bfloat16

D_MODEL = 1024
BATCH = 8
SEQ = 2048
TOKENS = BATCH * SEQ
MEM_LEN = 256
SSM_HEAD_DIM = 64
SSM_HEADS = 16
SSM_D_INNER = 1024
SSM_GROUPS = 2
SSM_STATE = 128
CONV_WIDTH = 4
CHUNK = 128
ATTN_HEAD_DIM = 64
ATTN_HEADS = 16
ATTN_WIDTH = 1024
XATTN_HEADS = 4
XATTN_HEAD_DIM = 256
D_FF = 4096
EPS = 1e-5

LANES = 128
N_MAIN = 5632
HEAD_PAIRS = ATTN_HEADS // 2
GROUP_W = SSM_D_INNER // SSM_GROUPS
DT_LANE = 16
VMEM_LIMIT = 56 * 1024 * 1024
NEG_BIG = -1e30
LOG2E = 1.4426950408889634


def _rms(xf, g_row):
    ms = jnp.mean(xf * xf, axis=-1, keepdims=True)
    return xf * lax.rsqrt(ms + EPS) * g_row


def _split2(a):
    hi = a.astype(BF16)
    mid = (a - hi.astype(F32)).astype(BF16)
    return jnp.concatenate([hi, mid], axis=-1)


def _split3(a):
    hi = a.astype(BF16)
    r1 = a - hi.astype(F32)
    mid = r1.astype(BF16)
    lo = (r1 - mid.astype(F32)).astype(BF16)
    return jnp.concatenate([hi, mid, lo], axis=-1)


def _softplus(x):
    return jnp.maximum(x, 0.0) + jnp.log1p(jnp.exp(-jnp.abs(x)))


def _silu(x):
    half = 0.5 * x
    return half + half * jnp.tanh(half)


def _dot(a, b):
    return jnp.dot(a, b, preferred_element_type=F32)


def _dot_nt(a, b):
    return lax.dot_general(a, b, (((1,), (1,)), ((), ())), preferred_element_type=F32)


IN_TM = 512
IN_TN = 512


def _inproj_kernel(x_ref, g_ref, w_ref, wg_ref, proj_ref, gate_ref):
    h = _rms(x_ref[...], g_ref[...]).astype(BF16)
    for n in range(0, N_MAIN, IN_TN):
        proj_ref[:, n:n + IN_TN] = _dot(h, w_ref[:, n:n + IN_TN]).astype(BF16)
    gate_ref[...] = _dot(h, wg_ref[...])


def _inproj(x2d, g_row, w_main, w_gate):
    return pl.pallas_call(
        _inproj_kernel,
        out_shape=(jax.ShapeDtypeStruct((TOKENS, N_MAIN), BF16),
                   jax.ShapeDtypeStruct((TOKENS, LANES), F32)),
        grid=(TOKENS // IN_TM,),
        in_specs=[
            pl.BlockSpec((IN_TM, D_MODEL), lambda i: (i, 0)),
            pl.BlockSpec((1, D_MODEL), lambda i: (0, 0)),
            pl.BlockSpec((D_MODEL, N_MAIN), lambda i: (0, 0)),
            pl.BlockSpec((D_MODEL, LANES), lambda i: (0, 0)),
        ],
        out_specs=(pl.BlockSpec((IN_TM, N_MAIN), lambda i: (i, 0)),
                   pl.BlockSpec((IN_TM, LANES), lambda i: (i, 0))),
        compiler_params=pltpu.CompilerParams(
            dimension_semantics=("parallel",), vmem_limit_bytes=VMEM_LIMIT),
        name="in_proj",
    )(x2d, g_row, w_main, w_gate)


AUX_PAIR_LANES = 16


def _gate_route_matrix():
    r = np.zeros((3 * LANES, LANES), np.float32)
    for h in range(ATTN_HEADS):
        for m in range(3):
            base = AUX_PAIR_LANES * (h // 2)
            r[m * LANES + h, base + 2 * m + (h % 2)] = 1.0
            r[m * LANES + h, base + 6 + 2 * m + (h % 2)] = -1.0
    return r


def _gates_kernel(g_ref, fb_ref, r_ref, aux_ref):
    row = lax.broadcasted_iota(jnp.int32, (CHUNK, CHUNK), 0)
    col = lax.broadcasted_iota(jnp.int32, (CHUNK, CHUNK), 1)
    tril = jnp.where(row >= col, 1.0, 0.0).astype(BF16)
    fb = fb_ref[...]
    rmat = r_ref[...]
    offset = jnp.zeros((1, LANES), F32)
    for blk in range(SEQ // CHUNK):
        rows = slice(blk * CHUNK, (blk + 1) * CHUNK)
        log_f = -_softplus(-(g_ref[rows, :] + fb))
        part = _dot(tril, _split3(log_f))
        local = part[:, :LANES] + part[:, LANES:2 * LANES] + part[:, 2 * LANES:]
        cum = local + offset
        offset = cum[CHUNK - 1:CHUNK, :]
        aux_ref[rows, :] = _dot(_split3(cum * LOG2E), rmat).astype(BF16)


def _gates(gates_raw, fb_row, rmat):
    return pl.pallas_call(
        _gates_kernel,
        out_shape=jax.ShapeDtypeStruct((TOKENS, LANES), BF16),
        grid=(BATCH,),
        in_specs=[
            pl.BlockSpec((SEQ, LANES), lambda b: (b, 0)),
            pl.BlockSpec((1, LANES), lambda b: (0, 0)),
            pl.BlockSpec((3 * LANES, LANES), lambda b: (0, 0)),
        ],
        out_specs=pl.BlockSpec((SEQ, LANES), lambda b: (b, 0)),
        compiler_params=pltpu.CompilerParams(dimension_semantics=("parallel",)),
        name="gates",
    )(gates_raw, fb_row, rmat)


def _head_expand_matrix():
    e = np.zeros((2 * LANES, SSM_D_INNER), np.float32)
    for h in range(SSM_HEADS):
        e[DT_LANE + h, h * SSM_HEAD_DIM:(h + 1) * SSM_HEAD_DIM] = 1.0
        e[LANES + DT_LANE + h, h * SSM_HEAD_DIM:(h + 1) * SSM_HEAD_DIM] = 1.0
    return e


CONV_TAIL = 16


def _conv_shift_matrix():
    s = np.zeros(((CONV_WIDTH - 1) * CHUNK, CHUNK + CONV_TAIL), np.float32)
    for k in range(CONV_WIDTH - 1):
        for t in range(CHUNK):
            src = t - 1 - k
            s[k * CHUNK + t, src if src >= 0 else CHUNK + CONV_TAIL + src] = 1.0
    return s


def _ssd_kernel(z_ref, xs_ref, bc_ref, g_ref, cwx_ref, cwb_ref, cbx_ref, cbb_ref,
                dtb_ref, alog_ref, ee_ref, shift_ref, dsk_ref, nw_ref, y_ref,
                xtail_ref, btail_ref, state_ref):
    c = pl.program_id(1)

    @pl.when(c == 0)
    def _():
        xtail_ref[...] = jnp.zeros_like(xtail_ref)
        btail_ref[...] = jnp.zeros_like(btail_ref)
        state_ref[...] = jnp.zeros_like(state_ref)

    def conv_silu(u_ref, tail_ref, w_ref, b_ref):
        cur = u_ref[...]
        shifted = _dot(shift_ref[...], jnp.concatenate([cur, tail_ref[...]], axis=0))
        tail_ref[...] = cur[CHUNK - CONV_TAIL:, :]
        acc = b_ref[...] + w_ref[3:4, :] * cur.astype(F32)
        for k in range(CONV_WIDTH - 1):
            acc = acc + w_ref[2 - k:3 - k, :] * shifted[k * CHUNK:(k + 1) * CHUNK, :]
        return _silu(acc)

    xs = conv_silu(xs_ref, xtail_ref, cwx_ref, cbx_ref)
    bc = conv_silu(bc_ref, btail_ref, cwb_ref, cbb_ref)

    dt = _softplus(g_ref[...] + dtb_ref[...])
    da = dt * (-jnp.exp(alog_ref[...]))
    row = lax.broadcasted_iota(jnp.int32, (CHUNK, CHUNK), 0)
    col = lax.broadcasted_iota(jnp.int32, (CHUNK, CHUNK), 1)
    causal = row >= col
    tril = jnp.where(causal, 1.0, 0.0).astype(BF16)
    part = _dot(tril, _split3(da))
    acs = part[:, :LANES] + part[:, LANES:2 * LANES] + part[:, 2 * LANES:]
    a_last = acs[CHUNK - 1:CHUNK, :]
    exp_a = jnp.exp(acs)
    dt_decay = dt * jnp.exp(a_last - acs)
    ee = ee_ref[...]
    e_dtdec = _dot(_split2(dt_decay), ee)
    e_expa = _dot(_split2(exp_a), ee)
    acs_t = acs.T
    dt_t = dt.T

    xs_b = xs.astype(BF16)
    xdec_b = (xs * e_dtdec).astype(BF16)
    lane = lax.broadcasted_iota(jnp.int32, (CHUNK, LANES), 1)
    first_head = lane < SSM_HEAD_DIM

    y_parts = []
    for g in range(SSM_GROUPS):
        b_g = bc[:, g * SSM_STATE:(g + 1) * SSM_STATE]
        c_g = bc[:, (SSM_GROUPS + g) * SSM_STATE:(SSM_GROUPS + g + 1) * SSM_STATE]
        c_gb = c_g.astype(BF16)
        cb = _dot_nt(c_gb, b_g.astype(BF16))
        state = state_ref[g]
        cols = slice(g * GROUP_W, (g + 1) * GROUP_W)
        y_off = _dot(c_gb, state.astype(BF16)) * e_expa[:, cols]
        st_new = _dot(b_g.T.astype(BF16), xdec_b[:, cols])
        state_ref[g] = state * e_expa[CHUNK - 1:CHUNK, cols] + st_new
        diag = []
        for pair in range(GROUP_W // LANES):
            x_pair = xs_b[:, g * GROUP_W + pair * LANES:g * GROUP_W + (pair + 1) * LANES]
            res = []
            for j in range(2):
                h = g * (SSM_HEADS // SSM_GROUPS) + 2 * pair + j
                hl = DT_LANE + h
                a_col = jnp.broadcast_to(acs[:, hl:hl + 1], (CHUNK, CHUNK))
                seg = jnp.where(causal, a_col - acs_t[hl:hl + 1, :], -jnp.inf)
                m_h = cb * jnp.exp(seg) * dt_t[hl:hl + 1, :]
                res.append(_dot(m_h.astype(BF16), x_pair))
            diag.append(jnp.where(first_head, res[0], res[1]))
        y_parts.append(jnp.concatenate(diag, axis=-1) + y_off)
    y = jnp.concatenate(y_parts, axis=-1) + dsk_ref[...] * xs
    y = y * _silu(z_ref[...].astype(F32))
    normed = []
    for g in range(SSM_GROUPS):
        y_g = y[:, g * GROUP_W:(g + 1) * GROUP_W]
        normed.append(y_g * lax.rsqrt(jnp.mean(y_g * y_g, axis=-1, keepdims=True) + EPS))
    y_ref[...] = (jnp.concatenate(normed, axis=-1) * nw_ref[...]).astype(BF16)


def _ssd(proj, gates_raw, cw_x, cw_bc, cb_x, cb_bc, dtb_row, alog_row, ee, shift, dsk_row, nw_row):
    nchunks = SEQ // CHUNK
    rowblk = lambda b, c: b * nchunks + c
    full = lambda shape: pl.BlockSpec(shape, lambda b, c: (0, 0))
    return pl.pallas_call(
        _ssd_kernel,
        out_shape=jax.ShapeDtypeStruct((TOKENS, SSM_D_INNER), BF16),
        grid=(BATCH, nchunks),
        in_specs=[
            pl.BlockSpec((CHUNK, SSM_D_INNER), lambda b, c: (rowblk(b, c), 0)),
            pl.BlockSpec((CHUNK, SSM_D_INNER), lambda b, c: (rowblk(b, c), 4)),
            pl.BlockSpec((CHUNK, GROUP_W), lambda b, c: (rowblk(b, c), 10)),
            pl.BlockSpec((CHUNK, LANES), lambda b, c: (rowblk(b, c), 0)),
            full((CONV_WIDTH, SSM_D_INNER)), full((CONV_WIDTH, GROUP_W)),
            full((1, SSM_D_INNER)), full((1, GROUP_W)),
            full((1, LANES)), full((1, LANES)),
            full((2 * LANES, SSM_D_INNER)),
            full(((CONV_WIDTH - 1) * CHUNK, CHUNK + CONV_TAIL)),
            full((1, SSM_D_INNER)), full((1, SSM_D_INNER)),
        ],
        out_specs=pl.BlockSpec((CHUNK, SSM_D_INNER), lambda b, c: (rowblk(b, c), 0)),
        scratch_shapes=[
            pltpu.VMEM((CONV_TAIL, SSM_D_INNER), BF16),
            pltpu.VMEM((CONV_TAIL, GROUP_W), BF16),
            pltpu.VMEM((SSM_GROUPS, SSM_STATE, GROUP_W), F32),
        ],
        compiler_params=pltpu.CompilerParams(
            dimension_semantics=("parallel", "arbitrary"),
            vmem_limit_bytes=VMEM_LIMIT),
        name="ssd",
    )(proj, proj, proj, gates_raw, cw_x, cw_bc, cb_x, cb_bc, dtb_row, alog_row, ee, shift, dsk_row,
      nw_row)


ATT_T = 256
ATT_K = 2 * LANES


ONES_ROWS = 16
VT_HEAD_ROWS = ATTN_HEAD_DIM + ONES_ROWS


def _attn_kernel(q_ref, k_ref, v_ref, aux_ref, gq_ref, gk_ref, o_ref, ka_ref, vt_ref, qt_ref, s_ref,
                 m_ref):
    nblk = SEQ // ATT_T
    pair = pl.program_id(1)
    lane = lax.broadcasted_iota(jnp.int32, (ATT_T, LANES), 1)
    sub = lane & (AUX_PAIR_LANES - 1)
    in_pair = (lane >> 4) == pair
    is_val = in_pair & (sub < 6)
    is_neg = in_pair & (sub >= 6) & (sub < 12)
    scale = ATTN_HEAD_DIM ** -0.5 * LOG2E
    krow = lax.broadcasted_iota(jnp.int32, (ATT_T, ATT_T), 0)
    qcol = lax.broadcasted_iota(jnp.int32, (ATT_T, ATT_T), 1)
    causal_t = qcol >= krow
    trow = lax.broadcasted_iota(jnp.int32, (LANES, ATT_T), 0)
    tsub = trow & (AUX_PAIR_LANES - 1)
    t_in_pair = (trow >> 4) == pair
    t_val = t_in_pair & (tsub < 6)
    t_neg = t_in_pair & (tsub >= 6) & (tsub < 12)

    def head_norm_t(u_t, g_t):
        sq = u_t * u_t
        halves = []
        for j in range(2):
            hrows = slice(j * ATTN_HEAD_DIM, (j + 1) * ATTN_HEAD_DIM)
            ms = jnp.sum(sq[hrows, :], axis=0, keepdims=True) * (1.0 / ATTN_HEAD_DIM)
            halves.append(u_t[hrows, :] * lax.rsqrt(ms + EPS))
        return jnp.concatenate(halves, axis=0) * g_t

    for j in range(2):
        vt_ref[j * VT_HEAD_ROWS + ATTN_HEAD_DIM:(j + 1) * VT_HEAD_ROWS, :] = jnp.ones(
            (ONES_ROWS, SEQ), BF16)

    def prepare(i):
        rows = slice(i * ATT_T, (i + 1) * ATT_T)
        qn_t = head_norm_t(q_ref[rows, :].astype(F32).T, gq_ref[...] * scale)
        kn = head_norm_t(k_ref[rows, :].astype(F32).T, gk_ref[...]).T
        aux = aux_ref[rows, :].astype(F32)
        aux_t = aux.T
        k_aux = jnp.where(is_neg, aux, jnp.where(is_val, 1.0, 0.0))
        ka_ref[rows, :] = jnp.concatenate([kn, k_aux], axis=-1).astype(BF16)
        v_t = v_ref[rows, :].astype(F32).T.astype(BF16)
        for j in range(2):
            vt_ref[j * VT_HEAD_ROWS:j * VT_HEAD_ROWS + ATTN_HEAD_DIM, rows] = (
                v_t[j * ATTN_HEAD_DIM:(j + 1) * ATTN_HEAD_DIM, :])
        for j in range(2):
            mine = (tsub & 1) == j
            q_aux_t = jnp.where(t_val & mine, aux_t, jnp.where(t_neg & mine, 1.0, 0.0))
            q_main_t = jnp.where((trow >> 6) == j, qn_t, 0.0)
            qt_ref[i % 2, j] = jnp.concatenate([q_main_t, q_aux_t], axis=0).astype(BF16)
        yield

    def scores(i):
        for j in range(2):
            qa_t = qt_ref[i % 2, j]
            tile_max = None
            for t in range(i + 1):
                s = _dot(ka_ref[t * ATT_T:(t + 1) * ATT_T, :], qa_t)
                if t == i:
                    s = jnp.where(causal_t, s, NEG_BIG)
                s_ref[i % 2, j, t] = s
                tile_max = s if tile_max is None else jnp.maximum(tile_max, s)
                yield
            m = jnp.max(tile_max, axis=0, keepdims=True)
            m_ref[i % 2, j] = jnp.broadcast_to(m, (8, ATT_T))

    def softmax_values(i):
        outs = []
        for j in range(2):
            m = m_ref[i % 2, j][0:1, :]
            acc = None
            for t in range(i + 1):
                pv = _dot(vt_ref[j * VT_HEAD_ROWS:(j + 1) * VT_HEAD_ROWS, t * ATT_T:(t + 1) * ATT_T],
                          jnp.exp2(s_ref[i % 2, j, t] - m).astype(BF16))
                acc = pv if acc is None else acc + pv
                yield
            outs.append(acc[:ATTN_HEAD_DIM, :] / acc[ATTN_HEAD_DIM:ATTN_HEAD_DIM + 1, :])
        o_t = jnp.concatenate(outs, axis=0)
        o_ref[i * ATT_T:(i + 1) * ATT_T, :] = o_t.T.astype(BF16)

    def interleave(*gens):
        live = list(gens)
        while live:
            live = [g for g in live if next(g, StopIteration) is not StopIteration]

    interleave(prepare(0))
    interleave(prepare(1), scores(0))
    for i in range(nblk):
        stage = [softmax_values(i)]
        if i + 1 < nblk:
            stage.append(scores(i + 1))
        if i + 2 < nblk:
            stage.append(prepare(i + 2))
        interleave(*stage)


def _attention(proj, aux, gq_row, gk_row):
    qcol = ATTN_WIDTH // LANES
    return pl.pallas_call(
        _attn_kernel,
        out_shape=jax.ShapeDtypeStruct((TOKENS, ATTN_WIDTH), BF16),
        grid=(BATCH, HEAD_PAIRS),
        in_specs=[
            pl.BlockSpec((SEQ, LANES), lambda b, hp: (b, qcol + hp)),
            pl.BlockSpec((SEQ, LANES), lambda b, hp: (b, 2 * qcol + hp)),
            pl.BlockSpec((SEQ, LANES), lambda b, hp: (b, 3 * qcol + hp)),
            pl.BlockSpec((SEQ, LANES), lambda b, hp: (b, 0)),
            pl.BlockSpec((LANES, ATT_T), lambda b, hp: (0, 0)),
            pl.BlockSpec((LANES, ATT_T), lambda b, hp: (0, 0)),
        ],
        out_specs=pl.BlockSpec((SEQ, LANES), lambda b, hp: (b, hp)),
        scratch_shapes=[
            pltpu.VMEM((SEQ, ATT_K), BF16),
            pltpu.VMEM((2 * VT_HEAD_ROWS, SEQ), BF16),
            pltpu.VMEM((2, 2, ATT_K, ATT_T), BF16),
            pltpu.VMEM((2, 2, SEQ // ATT_T, ATT_T, ATT_T), F32),
            pltpu.VMEM((2, 2, 8, ATT_T), F32),
        ],
        compiler_params=pltpu.CompilerParams(
            dimension_semantics=("parallel", "parallel"),
            vmem_limit_bytes=VMEM_LIMIT),
        name="fox_attention",
    )(proj, proj, proj, aux, gq_row, gk_row)


OUT_TN = 256


def _outproj_kernel(y_ref, o_ref, x_ref, w1_ref, w2_ref, out_ref):
    y = y_ref[...]
    o = o_ref[...]
    for n in range(0, D_MODEL, OUT_TN):
        cols = slice(n, n + OUT_TN)
        out_ref[:, cols] = x_ref[:, cols] + _dot(y, w1_ref[:, cols]) + _dot(o, w2_ref[:, cols])


def _outproj(y, o, x2d, w1, w2, tm=512):
    return pl.pallas_call(
        _outproj_kernel,
        out_shape=jax.ShapeDtypeStruct((TOKENS, D_MODEL), F32),
        grid=(TOKENS // tm,),
        in_specs=[
            pl.BlockSpec((tm, SSM_D_INNER), lambda i: (i, 0)),
            pl.BlockSpec((tm, ATTN_WIDTH), lambda i: (i, 0)),
            pl.BlockSpec((tm, D_MODEL), lambda i: (i, 0)),
            pl.BlockSpec((SSM_D_INNER, D_MODEL), lambda i: (0, 0)),
            pl.BlockSpec((ATTN_WIDTH, D_MODEL), lambda i: (0, 0)),
        ],
        out_specs=pl.BlockSpec((tm, D_MODEL), lambda i: (i, 0)),
        compiler_params=pltpu.CompilerParams(
            dimension_semantics=("parallel",), vmem_limit_bytes=VMEM_LIMIT),
        name="out_proj",
    )(y, o, x2d, w1, w2)


def _memkv_kernel(m_ref, g_ref, w_ref, gk_ref, knt_ref, v_ref):
    h = _rms(m_ref[...], g_ref[...]).astype(BF16)
    for a in range(XATTN_HEADS):
        cols = slice(a * XATTN_HEAD_DIM, (a + 1) * XATTN_HEAD_DIM)
        kn = _rms(_dot(h, w_ref[:, cols]), gk_ref[...])
        knt_ref[cols, :] = kn.T.astype(BF16)
    for n in range(0, D_MODEL, XATTN_HEAD_DIM):
        v_ref[:, n:n + XATTN_HEAD_DIM] = _dot(
            h, w_ref[:, D_MODEL + n:D_MODEL + n + XATTN_HEAD_DIM]).astype(BF16)


def _memkv(mem2d, g_row, wkv, gk_row):
    return pl.pallas_call(
        _memkv_kernel,
        out_shape=(jax.ShapeDtypeStruct((BATCH * D_MODEL, MEM_LEN), BF16),
                   jax.ShapeDtypeStruct((BATCH * MEM_LEN, D_MODEL), BF16)),
        grid=(BATCH,),
        in_specs=[
            pl.BlockSpec((MEM_LEN, D_MODEL), lambda b: (b, 0)),
            pl.BlockSpec((1, D_MODEL), lambda b: (0, 0)),
            pl.BlockSpec((D_MODEL, 2 * D_MODEL), lambda b: (0, 0)),
            pl.BlockSpec((1, XATTN_HEAD_DIM), lambda b: (0, 0)),
        ],
        out_specs=(pl.BlockSpec((D_MODEL, MEM_LEN), lambda b: (b, 0)),
                   pl.BlockSpec((MEM_LEN, D_MODEL), lambda b: (b, 0))),
        compiler_params=pltpu.CompilerParams(
            dimension_semantics=("parallel",), vmem_limit_bytes=VMEM_LIMIT),
        name="mem_kv",
    )(mem2d, g_row, wkv, gk_row)


def _xattn_kernel(x_ref, g_ref, wq_ref, knt_ref, v_ref, gq_ref, wo_ref, out_ref, o_scr):
    h = _rms(x_ref[...], g_ref[...]).astype(BF16)
    scale = XATTN_HEAD_DIM ** -0.5 * LOG2E
    for a in range(XATTN_HEADS):
        cols = slice(a * XATTN_HEAD_DIM, (a + 1) * XATTN_HEAD_DIM)
        q = _dot(h, wq_ref[:, cols])
        qn = (_rms(q, gq_ref[...]) * scale).astype(BF16)
        s = _dot(qn, knt_ref[cols, :])
        e = jnp.exp2(s - jnp.max(s, axis=-1, keepdims=True))
        p = e / jnp.sum(e, axis=-1, keepdims=True)
        o_scr[:, cols] = _dot(p.astype(BF16), v_ref[:, cols]).astype(BF16)
    o = o_scr[...]
    for n in range(0, D_MODEL, OUT_TN):
        cols = slice(n, n + OUT_TN)
        out_ref[:, cols] = x_ref[:, cols] + _dot(o, wo_ref[:, cols])


def _xattn(x1, g_row, wq, knt, v, gq_row, wo, tm=1024):
    nt = SEQ // tm
    return pl.pallas_call(
        _xattn_kernel,
        out_shape=jax.ShapeDtypeStruct((TOKENS, D_MODEL), F32),
        grid=(BATCH, nt),
        in_specs=[
            pl.BlockSpec((tm, D_MODEL), lambda b, i: (b * nt + i, 0)),
            pl.BlockSpec((1, D_MODEL), lambda b, i: (0, 0)),
            pl.BlockSpec((D_MODEL, D_MODEL), lambda b, i: (0, 0)),
            pl.BlockSpec((D_MODEL, MEM_LEN), lambda b, i: (b, 0)),
            pl.BlockSpec((MEM_LEN, D_MODEL), lambda b, i: (b, 0)),
            pl.BlockSpec((1, XATTN_HEAD_DIM), lambda b, i: (0, 0)),
            pl.BlockSpec((D_MODEL, D_MODEL), lambda b, i: (0, 0)),
        ],
        out_specs=pl.BlockSpec((tm, D_MODEL), lambda b, i: (b * nt + i, 0)),
        scratch_shapes=[pltpu.VMEM((tm, D_MODEL), BF16)],
        compiler_params=pltpu.CompilerParams(
            dimension_semantics=("parallel", "parallel"), vmem_limit_bytes=VMEM_LIMIT),
        name="mem_xattn",
    )(x1, g_row, wq, knt, v, gq_row, wo)


FF_CHUNK = 1024


def _mlp_kernel(x_ref, g_ref, wu_ref, wd_ref, out_ref, h_scr, acc_scr):
    h_scr[...] = _rms(x_ref[...], g_ref[...]).astype(BF16)
    acc_scr[...] = x_ref[...]
    for f in range(0, D_FF, FF_CHUNK):
        u = jnp.maximum(_dot(h_scr[...], wu_ref[:, f:f + FF_CHUNK]), 0.0)
        acc_scr[...] += _dot((u * u).astype(BF16), wd_ref[f:f + FF_CHUNK, :])
    out_ref[...] = acc_scr[...]


def _mlp(x2, g_row, wu, wd, tm=512):
    return pl.pallas_call(
        _mlp_kernel,
        out_shape=jax.ShapeDtypeStruct((TOKENS, D_MODEL), F32),
        grid=(TOKENS // tm,),
        in_specs=[
            pl.BlockSpec((tm, D_MODEL), lambda i: (i, 0)),
            pl.BlockSpec((1, D_MODEL), lambda i: (0, 0)),
            pl.BlockSpec((D_MODEL, D_FF), lambda i: (0, 0), pipeline_mode=pl.Buffered(1)),
            pl.BlockSpec((D_FF, D_MODEL), lambda i: (0, 0), pipeline_mode=pl.Buffered(1)),
        ],
        out_specs=pl.BlockSpec((tm, D_MODEL), lambda i: (i, 0)),
        scratch_shapes=[pltpu.VMEM((tm, D_MODEL), BF16), pltpu.VMEM((tm, D_MODEL), F32)],
        compiler_params=pltpu.CompilerParams(
            dimension_semantics=("parallel",), vmem_limit_bytes=VMEM_LIMIT),
        name="relu2_mlp",
    )(x2, g_row, wu, wd)


def _lane_row(vec, offset):
    return jnp.zeros((1, LANES), F32).at[0, offset:offset + vec.shape[0]].set(vec.astype(F32))


def _layer(x2d, mem2d, g_mix, w_in, conv_w, conv_b, dt_bias, a_log, d_skip, ssm_norm_w,
           g_q, g_k, f_bias, w_out, g_xattn, g_mem, xq_w, xkv_w, xg_q, xg_k, xo_w,
           g_mlp, w_up, w_down):
    z0, xbc0 = 0, SSM_D_INNER
    bc0 = xbc0 + SSM_D_INNER
    dt0 = bc0 + 2 * SSM_GROUPS * SSM_STATE
    q0 = dt0 + SSM_HEADS
    k0 = q0 + ATTN_WIDTH
    v0 = k0 + ATTN_WIDTH
    f0 = v0 + ATTN_WIDTH
    w_main = jnp.concatenate(
        [w_in[:, z0:xbc0], w_in[:, q0:f0], w_in[:, xbc0:dt0]], axis=1).astype(BF16)
    w_gate = jnp.concatenate(
        [w_in[:, f0:f0 + ATTN_HEADS], w_in[:, dt0:q0],
         jnp.zeros((D_MODEL, LANES - ATTN_HEADS - SSM_HEADS), F32)], axis=1).astype(BF16)
    row = lambda v: v.astype(F32).reshape(1, -1)

    proj, gates_raw = _inproj(x2d, row(g_mix), w_main, w_gate)

    aux = _gates(gates_raw, _lane_row(f_bias, 0), jnp.asarray(_gate_route_matrix(), BF16))

    y = _ssd(proj, gates_raw,
             conv_w[:, :SSM_D_INNER], conv_w[:, SSM_D_INNER:],
             row(conv_b[:SSM_D_INNER]), row(conv_b[SSM_D_INNER:]),
             _lane_row(dt_bias, DT_LANE), _lane_row(a_log, DT_LANE),
             jnp.asarray(_head_expand_matrix(), BF16), jnp.asarray(_conv_shift_matrix(), BF16),
             row(jnp.repeat(d_skip, SSM_HEAD_DIM)), row(ssm_norm_w))

    gain_t = lambda g: jnp.broadcast_to(jnp.tile(g.astype(F32), 2)[:, None], (LANES, ATT_T))
    o = _attention(proj, aux, gain_t(g_q), gain_t(g_k))

    x1 = _outproj(y, o, x2d, w_out[:SSM_D_INNER].astype(BF16), w_out[SSM_D_INNER:].astype(BF16))

    knt, mem_v = _memkv(mem2d, row(g_mem), xkv_w.astype(BF16), row(xg_k))
    x2 = _xattn(x1, row(g_xattn), xq_w.astype(BF16), knt, mem_v, row(xg_q), xo_w.astype(BF16))

    return _mlp(x2, row(g_mlp), w_up.astype(BF16), w_down.astype(BF16))


def kernel(x, mem, g_mix, w_in, conv_w, conv_b, dt_bias, a_log, d_skip, ssm_norm_w, g_q, g_k,
           f_bias, w_out, g_xattn, g_mem, xq_w, xkv_w, xg_q, xg_k, xo_w, g_mlp, w_up, w_down):
    x2d = x.reshape(TOKENS, D_MODEL)
    mem2d = mem.reshape(BATCH * MEM_LEN, D_MODEL)
    depth = g_mix.shape[0]
    for l in range(depth):
        x2d = _layer(x2d, mem2d, g_mix[l], w_in[l], conv_w[l], conv_b[l], dt_bias[l], a_log[l],
                     d_skip[l], ssm_norm_w[l], g_q[l], g_k[l], f_bias[l], w_out[l], g_xattn[l],
                     g_mem[l], xq_w[l], xkv_w[l], xg_q[l], xg_k[l], xo_w[l], g_mlp[l], w_up[l],
                     w_down[l])
    return x2d.reshape(BATCH, SEQ, D_MODEL)
```

```python
import functools

import numpy as np
import jax
import jax.numpy as jnp
from jax import lax
from jax.experimental import pallas as pl
from jax.experimental.pallas import tpu as pltpu

F32 = jnp.float32
BF16 = jnp.bfloat16

D_MODEL = 1024
BATCH = 8
SEQ = 2048
TOKENS = BATCH * SEQ
MEM_LEN = 256
SSM_HEAD_DIM = 64
SSM_HEADS = 16
SSM_D_INNER = 1024
SSM_GROUPS = 2
SSM_STATE = 128
CONV_WIDTH = 4
CHUNK = 128
ATTN_HEAD_DIM = 64
ATTN_HEADS = 16
ATTN_WIDTH = 1024
XATTN_HEADS = 4
XATTN_HEAD_DIM = 256
D_FF = 4096
EPS = 1e-5

LANES = 128
N_MAIN = 5632
HEAD_PAIRS = ATTN_HEADS // 2
GROUP_W = SSM_D_INNER // SSM_GROUPS
DT_LANE = 16
VMEM_LIMIT = 56 * 1024 * 1024
NEG_BIG = -1e30
LOG2E = 1.4426950408889634


def _rms(xf, g_row):
    ms = jnp.mean(xf * xf, axis=-1, keepdims=True)
    return xf * lax.rsqrt(ms + EPS) * g_row


def _split2(a):
    hi = a.astype(BF16)
    mid = (a - hi.astype(F32)).astype(BF16)
    return jnp.concatenate([hi, mid], axis=-1)


def _split3(a):
    hi = a.astype(BF16)
    r1 = a - hi.astype(F32)
    mid = r1.astype(BF16)
    lo = (r1 - mid.astype(F32)).astype(BF16)
    return jnp.concatenate([hi, mid, lo], axis=-1)


def _softplus(x):
    return jnp.maximum(x, 0.0) + jnp.log1p(jnp.exp(-jnp.abs(x)))


def _silu(x):
    half = 0.5 * x
    return half + half * jnp.tanh(half)


def _dot(a, b):
    return jnp.dot(a, b, preferred_element_type=F32)


def _dot_nt(a, b):
    return lax.dot_general(a, b, (((1,), (1,)), ((), ())), preferred_element_type=F32)


def _interleave(*gens):
    live = list(gens)
    while live:
        live = [g for g in live if next(g, StopIteration) is not StopIteration]


IN_TM = 512
IN_TN = 512


def _inproj_kernel(x_ref, g_ref, w_ref, wg_ref, proj_ref, gate_ref):
    h = _rms(x_ref[...], g_ref[...]).astype(BF16)
    for n in range(0, N_MAIN, IN_TN):
        proj_ref[:, n:n + IN_TN] = _dot(h, w_ref[:, n:n + IN_TN]).astype(BF16)
    gate_ref[...] = _dot(h, wg_ref[...])


def _inproj(x2d, g_row, w_main, w_gate):
    return pl.pallas_call(
        _inproj_kernel,
        out_shape=(jax.ShapeDtypeStruct((TOKENS, N_MAIN), BF16),
                   jax.ShapeDtypeStruct((TOKENS, LANES), F32)),
        grid=(TOKENS // IN_TM,),
        in_specs=[
            pl.BlockSpec((IN_TM, D_MODEL), lambda i: (i, 0)),
            pl.BlockSpec((1, D_MODEL), lambda i: (0, 0)),
            pl.BlockSpec((D_MODEL, N_MAIN), lambda i: (0, 0)),
            pl.BlockSpec((D_MODEL, LANES), lambda i: (0, 0)),
        ],
        out_specs=(pl.BlockSpec((IN_TM, N_MAIN), lambda i: (i, 0)),
                   pl.BlockSpec((IN_TM, LANES), lambda i: (i, 0))),
        compiler_params=pltpu.CompilerParams(
            dimension_semantics=("parallel",), vmem_limit_bytes=VMEM_LIMIT),
        name="in_proj",
    )(x2d, g_row, w_main, w_gate)


AUX_PAIR_LANES = 16


def _gate_route_matrix():
    r = np.zeros((3 * LANES, LANES), np.float32)
    for h in range(ATTN_HEADS):
        for m in range(3):
            base = AUX_PAIR_LANES * (h // 2)
            r[m * LANES + h, base + 2 * m + (h % 2)] = 1.0
            r[m * LANES + h, base + 6 + 2 * m + (h % 2)] = -1.0
    return r


def _gates_kernel(g_ref, fb_ref, r_ref, aux_ref):
    row = lax.broadcasted_iota(jnp.int32, (CHUNK, CHUNK), 0)
    col = lax.broadcasted_iota(jnp.int32, (CHUNK, CHUNK), 1)
    tril = jnp.where(row >= col, 1.0, 0.0).astype(BF16)
    fb = fb_ref[...]
    rmat = r_ref[...]
    offset = jnp.zeros((1, LANES), F32)
    for blk in range(SEQ // CHUNK):
        rows = slice(blk * CHUNK, (blk + 1) * CHUNK)
        log_f = -_softplus(-(g_ref[rows, :] + fb))
        part = _dot(tril, _split3(log_f))
        local = part[:, :LANES] + part[:, LANES:2 * LANES] + part[:, 2 * LANES:]
        cum = local + offset
        offset = cum[CHUNK - 1:CHUNK, :]
        aux_ref[rows, :] = _dot(_split3(cum * LOG2E), rmat).astype(BF16)


def _gates(gates_raw, fb_row, rmat):
    return pl.pallas_call(
        _gates_kernel,
        out_shape=jax.ShapeDtypeStruct((TOKENS, LANES), BF16),
        grid=(BATCH,),
        in_specs=[
            pl.BlockSpec((SEQ, LANES), lambda b: (b, 0)),
            pl.BlockSpec((1, LANES), lambda b: (0, 0)),
            pl.BlockSpec((3 * LANES, LANES), lambda b: (0, 0)),
        ],
        out_specs=pl.BlockSpec((SEQ, LANES), lambda b: (b, 0)),
        compiler_params=pltpu.CompilerParams(dimension_semantics=("parallel",)),
        name="gates",
    )(gates_raw, fb_row, rmat)


def _head_expand_matrix():
    e = np.zeros((2 * LANES, SSM_D_INNER), np.float32)
    for h in range(SSM_HEADS):
        e[DT_LANE + h, h * SSM_HEAD_DIM:(h + 1) * SSM_HEAD_DIM] = 1.0
        e[LANES + DT_LANE + h, h * SSM_HEAD_DIM:(h + 1) * SSM_HEAD_DIM] = 1.0
    return e


CONV_TAIL = 16


def _conv_shift_matrix():
    s = np.zeros(((CONV_WIDTH - 1) * CHUNK, CHUNK + CONV_TAIL), np.float32)
    for k in range(CONV_WIDTH - 1):
        for t in range(CHUNK):
            src = t - 1 - k
            s[k * CHUNK + t, src if src >= 0 else CHUNK + CONV_TAIL + src] = 1.0
    return s


def _ssd_chunk(z_ref, xs_ref, bc_ref, g_ref, cwx_ref, cwb_ref, cbx_ref, cbb_ref,
               dtb_ref, alog_ref, ee_ref, shift_ref, dsk_ref, nw_ref, y_ref,
               xtail, btail, states):
    def conv_silu(u_ref, tail, w_ref, b_ref):
        cur = u_ref[...]
        shifted = _dot(shift_ref[...], jnp.concatenate([cur, tail], axis=0))
        acc = b_ref[...] + w_ref[3:4, :] * cur.astype(F32)
        for k in range(CONV_WIDTH - 1):
            acc = acc + w_ref[2 - k:3 - k, :] * shifted[k * CHUNK:(k + 1) * CHUNK, :]
        return _silu(acc), cur[CHUNK - CONV_TAIL:, :]

    xs, xtail = conv_silu(xs_ref, xtail, cwx_ref, cbx_ref)
    bc, btail = conv_silu(bc_ref, btail, cwb_ref, cbb_ref)

    dt = _softplus(g_ref[...] + dtb_ref[...])
    da = dt * (-jnp.exp(alog_ref[...]))
    row = lax.broadcasted_iota(jnp.int32, (CHUNK, CHUNK), 0)
    col = lax.broadcasted_iota(jnp.int32, (CHUNK, CHUNK), 1)
    causal = row >= col
    tril = jnp.where(causal, 1.0, 0.0).astype(BF16)
    part = _dot(tril, _split3(da))
    acs = part[:, :LANES] + part[:, LANES:2 * LANES] + part[:, 2 * LANES:]
    a_last = acs[CHUNK - 1:CHUNK, :]
    exp_a = jnp.exp(acs)
    dt_decay = dt * jnp.exp(a_last - acs)
    ee = ee_ref[...]
    e_dtdec = _dot(_split2(dt_decay), ee)
    e_expa = _dot(_split2(exp_a), ee)
    acs_t = acs.T
    dt_t = dt.T

    xs_b = xs.astype(BF16)
    xdec_b = (xs * e_dtdec).astype(BF16)
    lane = lax.broadcasted_iota(jnp.int32, (CHUNK, LANES), 1)
    first_head = lane < SSM_HEAD_DIM

    y_parts = []
    new_states = []
    for g in range(SSM_GROUPS):
        b_g = bc[:, g * SSM_STATE:(g + 1) * SSM_STATE]
        c_g = bc[:, (SSM_GROUPS + g) * SSM_STATE:(SSM_GROUPS + g + 1) * SSM_STATE]
        c_gb = c_g.astype(BF16)
        cb = _dot_nt(c_gb, b_g.astype(BF16))
        state = states[g]
        cols = slice(g * GROUP_W, (g + 1) * GROUP_W)
        y_off = _dot(c_gb, state.astype(BF16)) * e_expa[:, cols]
        st_new = _dot(b_g.T.astype(BF16), xdec_b[:, cols])
        new_states.append(state * e_expa[CHUNK - 1:CHUNK, cols] + st_new)
        diag = []
        for pair in range(GROUP_W // LANES):
            x_pair = xs_b[:, g * GROUP_W + pair * LANES:g * GROUP_W + (pair + 1) * LANES]
            res = []
            for j in range(2):
                h = g * (SSM_HEADS // SSM_GROUPS) + 2 * pair + j
                hl = DT_LANE + h
                a_col = jnp.broadcast_to(acs[:, hl:hl + 1], (CHUNK, CHUNK))
                seg = jnp.where(causal, a_col - acs_t[hl:hl + 1, :], -jnp.inf)
                m_h = cb * jnp.exp(seg) * dt_t[hl:hl + 1, :]
                res.append(_dot(m_h.astype(BF16), x_pair))
            diag.append(jnp.where(first_head, res[0], res[1]))
        y_parts.append(jnp.concatenate(diag, axis=-1) + y_off)
    y = jnp.concatenate(y_parts, axis=-1) + dsk_ref[...] * xs
    y = y * _silu(z_ref[...].astype(F32))
    normed = []
    for g in range(SSM_GROUPS):
        y_g = y[:, g * GROUP_W:(g + 1) * GROUP_W]
        normed.append(y_g * lax.rsqrt(jnp.mean(y_g * y_g, axis=-1, keepdims=True) + EPS))
    y_ref[...] = (jnp.concatenate(normed, axis=-1) * nw_ref[...]).astype(BF16)
    return xtail, btail, new_states


SSD_CHUNKS_PER_STEP = 4


def _ssd_kernel(z_ref, xs_ref, bc_ref, g_ref, *rest):
    *consts, y_ref, xtail_ref, btail_ref, state_ref = rest

    @pl.when(pl.program_id(1) == 0)
    def _():
        xtail_ref[...] = jnp.zeros_like(xtail_ref)
        btail_ref[...] = jnp.zeros_like(btail_ref)
        state_ref[...] = jnp.zeros_like(state_ref)

    xtail, btail = xtail_ref[...], btail_ref[...]
    states = [state_ref[g] for g in range(SSM_GROUPS)]
    for sc in range(SSD_CHUNKS_PER_STEP):
        rows = pl.ds(sc * CHUNK, CHUNK)
        xtail, btail, states = _ssd_chunk(
            z_ref.at[rows], xs_ref.at[rows], bc_ref.at[rows], g_ref.at[rows], *consts,
            y_ref.at[rows], xtail, btail, states)
    xtail_ref[...] = xtail
    btail_ref[...] = btail
    for g in range(SSM_GROUPS):
        state_ref[g] = states[g]


def _ssd(proj, gates_raw, cw_x, cw_bc, cb_x, cb_bc, dtb_row, alog_row, ee, shift, dsk_row, nw_row):
    step_rows = SSD_CHUNKS_PER_STEP * CHUNK
    nsteps = SEQ // step_rows
    rowblk = lambda b, c: b * nsteps + c
    full = lambda shape: pl.BlockSpec(shape, lambda b, c: (0, 0))
    return pl.pallas_call(
        _ssd_kernel,
        out_shape=jax.ShapeDtypeStruct((TOKENS, SSM_D_INNER), BF16),
        grid=(BATCH, nsteps),
        in_specs=[
            pl.BlockSpec((step_rows, SSM_D_INNER), lambda b, c: (rowblk(b, c), 0)),
            pl.BlockSpec((step_rows, SSM_D_INNER), lambda b, c: (rowblk(b, c), 4)),
            pl.BlockSpec((step_rows, GROUP_W), lambda b, c: (rowblk(b, c), 10)),
            pl.BlockSpec((step_rows, LANES), lambda b, c: (rowblk(b, c), 0)),
            full((CONV_WIDTH, SSM_D_INNER)), full((CONV_WIDTH, GROUP_W)),
            full((1, SSM_D_INNER)), full((1, GROUP_W)),
            full((1, LANES)), full((1, LANES)),
            full((2 * LANES, SSM_D_INNER)),
            full(((CONV_WIDTH - 1) * CHUNK, CHUNK + CONV_TAIL)),
            full((1, SSM_D_INNER)), full((1, SSM_D_INNER)),
        ],
        out_specs=pl.BlockSpec((step_rows, SSM_D_INNER), lambda b, c: (rowblk(b, c), 0)),
        scratch_shapes=[
            pltpu.VMEM((CONV_TAIL, SSM_D_INNER), BF16),
            pltpu.VMEM((CONV_TAIL, GROUP_W), BF16),
            pltpu.VMEM((SSM_GROUPS, SSM_STATE, GROUP_W), F32),
        ],
        compiler_params=pltpu.CompilerParams(
            dimension_semantics=("parallel", "arbitrary"),
            vmem_limit_bytes=VMEM_LIMIT),
        name="ssd",
    )(proj, proj, proj, gates_raw, cw_x, cw_bc, cb_x, cb_bc, dtb_row, alog_row, ee, shift, dsk_row,
      nw_row)


ATT_T = 256
ATT_K = 2 * LANES


ONES_ROWS = 16
VT_HEAD_ROWS = ATTN_HEAD_DIM + ONES_ROWS


def _attn_kernel(q_ref, k_ref, v_ref, aux_ref, gq_ref, gk_ref, o_ref, ka_ref, vt_ref, qt_ref, s_ref,
                 m_ref):
    nblk = SEQ // ATT_T
    pair = pl.program_id(1)
    lane = lax.broadcasted_iota(jnp.int32, (ATT_T, LANES), 1)
    sub = lane & (AUX_PAIR_LANES - 1)
    in_pair = (lane >> 4) == pair
    is_val = in_pair & (sub < 6)
    is_neg = in_pair & (sub >= 6) & (sub < 12)
    scale = ATTN_HEAD_DIM ** -0.5 * LOG2E
    krow = lax.broadcasted_iota(jnp.int32, (ATT_T, ATT_T), 0)
    qcol = lax.broadcasted_iota(jnp.int32, (ATT_T, ATT_T), 1)
    causal_t = qcol >= krow
    trow = lax.broadcasted_iota(jnp.int32, (LANES, ATT_T), 0)
    tsub = trow & (AUX_PAIR_LANES - 1)
    t_in_pair = (trow >> 4) == pair
    t_val = t_in_pair & (tsub < 6)
    t_neg = t_in_pair & (tsub >= 6) & (tsub < 12)

    def head_norm_t(u_t, g_t):
        sq = u_t * u_t
        halves = []
        for j in range(2):
            hrows = slice(j * ATTN_HEAD_DIM, (j + 1) * ATTN_HEAD_DIM)
            ms = jnp.sum(sq[hrows, :], axis=0, keepdims=True) * (1.0 / ATTN_HEAD_DIM)
            halves.append(u_t[hrows, :] * lax.rsqrt(ms + EPS))
        return jnp.concatenate(halves, axis=0) * g_t

    for j in range(2):
        vt_ref[j * VT_HEAD_ROWS + ATTN_HEAD_DIM:(j + 1) * VT_HEAD_ROWS, :] = jnp.ones(
            (ONES_ROWS, SEQ), BF16)

    def prepare(i):
        rows = slice(i * ATT_T, (i + 1) * ATT_T)
        qn_t = head_norm_t(q_ref[rows, :].astype(F32).T, gq_ref[...] * scale)
        kn = head_norm_t(k_ref[rows, :].astype(F32).T, gk_ref[...]).T
        aux = aux_ref[rows, :].astype(F32)
        aux_t = aux.T
        k_aux = jnp.where(is_neg, aux, jnp.where(is_val, 1.0, 0.0))
        ka_ref[rows, :] = jnp.concatenate([kn, k_aux], axis=-1).astype(BF16)
        v_t = v_ref[rows, :].astype(F32).T.astype(BF16)
        for j in range(2):
            vt_ref[j * VT_HEAD_ROWS:j * VT_HEAD_ROWS + ATTN_HEAD_DIM, rows] = (
                v_t[j * ATTN_HEAD_DIM:(j + 1) * ATTN_HEAD_DIM, :])
        for j in range(2):
            mine = (tsub & 1) == j
            q_aux_t = jnp.where(t_val & mine, aux_t, jnp.where(t_neg & mine, 1.0, 0.0))
            q_main_t = jnp.where((trow >> 6) == j, qn_t, 0.0)
            qt_ref[i % 2, j] = jnp.concatenate([q_main_t, q_aux_t], axis=0).astype(BF16)
        yield

    def scores(i):
        for j in range(2):
            qa_t = qt_ref[i % 2, j]
            tile_max = None
            for t in range(i + 1):
                s = _dot(ka_ref[t * ATT_T:(t + 1) * ATT_T, :], qa_t)
                if t == i:
                    s = jnp.where(causal_t, s, NEG_BIG)
                s_ref[i % 2, j, t] = s
                tile_max = s if tile_max is None else jnp.maximum(tile_max, s)
                yield
            m = jnp.max(tile_max, axis=0, keepdims=True)
            m_ref[i % 2, j] = jnp.broadcast_to(m, (8, ATT_T))

    def softmax_values(i):
        outs = []
        for j in range(2):
            m = m_ref[i % 2, j][0:1, :]
            acc = None
            for t in range(i + 1):
                pv = _dot(vt_ref[j * VT_HEAD_ROWS:(j + 1) * VT_HEAD_ROWS, t * ATT_T:(t + 1) * ATT_T],
                          jnp.exp2(s_ref[i % 2, j, t] - m).astype(BF16))
                acc = pv if acc is None else acc + pv
                yield
            outs.append(acc[:ATTN_HEAD_DIM, :] / acc[ATTN_HEAD_DIM:ATTN_HEAD_DIM + 1, :])
        o_t = jnp.concatenate(outs, axis=0)
        o_ref[i * ATT_T:(i + 1) * ATT_T, :] = o_t.T.astype(BF16)

    _interleave(prepare(0))
    _interleave(prepare(1), scores(0))
    for i in range(nblk):
        stage = [softmax_values(i)]
        if i + 1 < nblk:
            stage.append(scores(i + 1))
        if i + 2 < nblk:
            stage.append(prepare(i + 2))
        _interleave(*stage)


def _attention(proj, aux, gq_row, gk_row):
    qcol = ATTN_WIDTH // LANES
    return pl.pallas_call(
        _attn_kernel,
        out_shape=jax.ShapeDtypeStruct((TOKENS, ATTN_WIDTH), BF16),
        grid=(BATCH, HEAD_PAIRS),
        in_specs=[
            pl.BlockSpec((SEQ, LANES), lambda b, hp: (b, qcol + hp)),
            pl.BlockSpec((SEQ, LANES), lambda b, hp: (b, 2 * qcol + hp)),
            pl.BlockSpec((SEQ, LANES), lambda b, hp: (b, 3 * qcol + hp)),
            pl.BlockSpec((SEQ, LANES), lambda b, hp: (b, 0)),
            pl.BlockSpec((LANES, ATT_T), lambda b, hp: (0, 0)),
            pl.BlockSpec((LANES, ATT_T), lambda b, hp: (0, 0)),
        ],
        out_specs=pl.BlockSpec((SEQ, LANES), lambda b, hp: (b, hp)),
        scratch_shapes=[
            pltpu.VMEM((SEQ, ATT_K), BF16),
            pltpu.VMEM((2 * VT_HEAD_ROWS, SEQ), BF16),
            pltpu.VMEM((2, 2, ATT_K, ATT_T), BF16),
            pltpu.VMEM((2, 2, SEQ // ATT_T, ATT_T, ATT_T), F32),
            pltpu.VMEM((2, 2, 8, ATT_T), F32),
        ],
        compiler_params=pltpu.CompilerParams(
            dimension_semantics=("parallel", "parallel"),
            vmem_limit_bytes=VMEM_LIMIT),
        name="fox_attention",
    )(proj, proj, proj, aux, gq_row, gk_row)


OUT_TN = 256


def _outproj_kernel(y_ref, o_ref, x_ref, w_ref, out_ref):
    y = y_ref[...]
    o = o_ref[...]
    for n in range(0, D_MODEL, OUT_TN):
        cols = slice(n, n + OUT_TN)
        out_ref[:, cols] = (x_ref[:, cols]
                            + _dot(y, w_ref[:SSM_D_INNER, cols].astype(BF16))
                            + _dot(o, w_ref[SSM_D_INNER:, cols].astype(BF16)))


def _outproj(y, o, x2d, w_out, tm=512):
    return pl.pallas_call(
        _outproj_kernel,
        out_shape=jax.ShapeDtypeStruct((TOKENS, D_MODEL), F32),
        grid=(TOKENS // tm,),
        in_specs=[
            pl.BlockSpec((tm, SSM_D_INNER), lambda i: (i, 0)),
            pl.BlockSpec((tm, ATTN_WIDTH), lambda i: (i, 0)),
            pl.BlockSpec((tm, D_MODEL), lambda i: (i, 0)),
            pl.BlockSpec((SSM_D_INNER + ATTN_WIDTH, D_MODEL), lambda i: (0, 0),
                         pipeline_mode=pl.Buffered(1)),
        ],
        out_specs=pl.BlockSpec((tm, D_MODEL), lambda i: (i, 0)),
        compiler_params=pltpu.CompilerParams(
            dimension_semantics=("parallel",), vmem_limit_bytes=VMEM_LIMIT),
        name="out_proj",
    )(y, o, x2d, w_out)


def _memkv_kernel(m_ref, g_ref, w_ref, gk_ref, knt_ref, v_ref):
    h = _rms(m_ref[...], g_ref[...]).astype(BF16)
    for a in range(XATTN_HEADS):
        cols = slice(a * XATTN_HEAD_DIM, (a + 1) * XATTN_HEAD_DIM)
        kn = _rms(_dot(h, w_ref[:, cols].astype(BF16)), gk_ref[...])
        knt_ref[cols, :] = kn.T.astype(BF16)
    for n in range(0, D_MODEL, XATTN_HEAD_DIM):
        v_ref[:, n:n + XATTN_HEAD_DIM] = _dot(
            h, w_ref[:, D_MODEL + n:D_MODEL + n + XATTN_HEAD_DIM].astype(BF16)).astype(BF16)


def _memkv(mem2d, g_row, wkv, gk_row):
    return pl.pallas_call(
        _memkv_kernel,
        out_shape=(jax.ShapeDtypeStruct((BATCH * D_MODEL, MEM_LEN), BF16),
                   jax.ShapeDtypeStruct((BATCH * MEM_LEN, D_MODEL), BF16)),
        grid=(BATCH,),
        in_specs=[
            pl.BlockSpec((MEM_LEN, D_MODEL), lambda b: (b, 0)),
            pl.BlockSpec((1, D_MODEL), lambda b: (0, 0)),
            pl.BlockSpec((D_MODEL, 2 * D_MODEL), lambda b: (0, 0), pipeline_mode=pl.Buffered(1)),
            pl.BlockSpec((1, XATTN_HEAD_DIM), lambda b: (0, 0)),
        ],
        out_specs=(pl.BlockSpec((D_MODEL, MEM_LEN), lambda b: (b, 0)),
                   pl.BlockSpec((MEM_LEN, D_MODEL), lambda b: (b, 0))),
        compiler_params=pltpu.CompilerParams(
            dimension_semantics=("parallel",), vmem_limit_bytes=VMEM_LIMIT),
        name="mem_kv",
    )(mem2d, g_row, wkv, gk_row)


XATTN_SUB = 256


def _xattn_kernel(x_ref, g_ref, wq_ref, knt_ref, v_ref, gq_ref, wo_ref, out_ref, q_scr, o_scr):
    nsub = x_ref.shape[0] // XATTN_SUB
    scale = XATTN_HEAD_DIM ** -0.5 * LOG2E
    head_cols = [slice(a * XATTN_HEAD_DIM, (a + 1) * XATTN_HEAD_DIM) for a in range(XATTN_HEADS)]

    def project(k):
        rows = slice(k * XATTN_SUB, (k + 1) * XATTN_SUB)
        h = _rms(x_ref[rows, :], g_ref[...]).astype(BF16)
        for cols in head_cols:
            q_scr[k % 2, :, cols] = _dot(h, wq_ref[:, cols])
            yield

    def attend(k):
        for cols in head_cols:
            qn = (_rms(q_scr[k % 2, :, cols], gq_ref[...]) * scale).astype(BF16)
            s = _dot(qn, knt_ref[cols, :])
            e = jnp.exp2(s - jnp.max(s, axis=-1, keepdims=True))
            p = e / jnp.sum(e, axis=-1, keepdims=True)
            o_scr[k % 2, :, cols] = _dot(p.astype(BF16), v_ref[:, cols]).astype(BF16)
            yield

    def output(k):
        rows = slice(k * XATTN_SUB, (k + 1) * XATTN_SUB)
        o = o_scr[k % 2]
        for n in range(0, D_MODEL, OUT_TN):
            cols = slice(n, n + OUT_TN)
            out_ref[rows, cols] = x_ref[rows, cols] + _dot(o, wo_ref[:, cols])
            yield

    for step in range(nsub + 2):
        stage = []
        if 0 <= step - 2 < nsub:
            stage.append(output(step - 2))
        if 0 <= step - 1 < nsub:
            stage.append(attend(step - 1))
        if step < nsub:
            stage.append(project(step))
        _interleave(*stage)


def _xattn(x1, g_row, wq, knt, v, gq_row, wo, tm=SEQ):
    nt = SEQ // tm
    return pl.pallas_call(
        _xattn_kernel,
        out_shape=jax.ShapeDtypeStruct((TOKENS, D_MODEL), F32),
        grid=(BATCH, nt),
        in_specs=[
            pl.BlockSpec((tm, D_MODEL), lambda b, i: (b * nt + i, 0)),
            pl.BlockSpec((1, D_MODEL), lambda b, i: (0, 0)),
            pl.BlockSpec((D_MODEL, D_MODEL), lambda b, i: (0, 0)),
            pl.BlockSpec((D_MODEL, MEM_LEN), lambda b, i: (b, 0)),
            pl.BlockSpec((MEM_LEN, D_MODEL), lambda b, i: (b, 0)),
            pl.BlockSpec((1, XATTN_HEAD_DIM), lambda b, i: (0, 0)),
            pl.BlockSpec((D_MODEL, D_MODEL), lambda b, i: (0, 0)),
        ],
        out_specs=pl.BlockSpec((tm, D_MODEL), lambda b, i: (b * nt + i, 0)),
        scratch_shapes=[pltpu.VMEM((2, XATTN_SUB, D_MODEL), F32),
                        pltpu.VMEM((2, XATTN_SUB, D_MODEL), BF16)],
        compiler_params=pltpu.CompilerParams(
            dimension_semantics=("parallel", "parallel"), vmem_limit_bytes=VMEM_LIMIT),
        name="mem_xattn",
    )(x1, g_row, wq, knt, v, gq_row, wo)


FF_CHUNK = 1024


def _mlp_kernel(x_ref, g_ref, wu_ref, wd_ref, out_ref, h_scr, acc_scr):
    h_scr[...] = _rms(x_ref[...], g_ref[...]).astype(BF16)
    acc_scr[...] = x_ref[...]
    for f in range(0, D_FF, FF_CHUNK):
        u = jnp.maximum(_dot(h_scr[...], wu_ref[:, f:f + FF_CHUNK].astype(BF16)), 0.0)
        acc_scr[...] += _dot((u * u).astype(BF16), wd_ref[f:f + FF_CHUNK, :].astype(BF16))
    out_ref[...] = acc_scr[...]


def _mlp(x2, g_row, wu, wd, tm=512):
    return pl.pallas_call(
        _mlp_kernel,
        out_shape=jax.ShapeDtypeStruct((TOKENS, D_MODEL), F32),
        grid=(TOKENS // tm,),
        in_specs=[
            pl.BlockSpec((tm, D_MODEL), lambda i: (i, 0)),
            pl.BlockSpec((1, D_MODEL), lambda i: (0, 0)),
            pl.BlockSpec((D_MODEL, D_FF), lambda i: (0, 0), pipeline_mode=pl.Buffered(1)),
            pl.BlockSpec((D_FF, D_MODEL), lambda i: (0, 0), pipeline_mode=pl.Buffered(1)),
        ],
        out_specs=pl.BlockSpec((tm, D_MODEL), lambda i: (i, 0)),
        scratch_shapes=[pltpu.VMEM((tm, D_MODEL), BF16), pltpu.VMEM((tm, D_MODEL), F32)],
        compiler_params=pltpu.CompilerParams(
            dimension_semantics=("parallel",), vmem_limit_bytes=VMEM_LIMIT),
        name="relu2_mlp",
    )(x2, g_row, wu, wd)


def _lane_row(vec, offset):
    return jnp.zeros((1, LANES), F32).at[0, offset:offset + vec.shape[0]].set(vec.astype(F32))


def _layer(x2d, mem2d, g_mix, w_in, conv_w, conv_b, dt_bias, a_log, d_skip, ssm_norm_w,
           g_q, g_k, f_bias, w_out, g_xattn, g_mem, xq_w, xkv_w, xg_q, xg_k, xo_w,
           g_mlp, w_up, w_down):
    z0, xbc0 = 0, SSM_D_INNER
    bc0 = xbc0 + SSM_D_INNER
    dt0 = bc0 + 2 * SSM_GROUPS * SSM_STATE
    q0 = dt0 + SSM_HEADS
    k0 = q0 + ATTN_WIDTH
    v0 = k0 + ATTN_WIDTH
    f0 = v0 + ATTN_WIDTH
    w_main = jnp.concatenate(
        [w_in[:, z0:xbc0], w_in[:, q0:f0], w_in[:, xbc0:dt0]], axis=1).astype(BF16)
    w_gate = jnp.concatenate(
        [w_in[:, f0:f0 + ATTN_HEADS], w_in[:, dt0:q0],
         jnp.zeros((D_MODEL, LANES - ATTN_HEADS - SSM_HEADS), F32)], axis=1).astype(BF16)
    row = lambda v: v.astype(F32).reshape(1, -1)

    proj, gates_raw = _inproj(x2d, row(g_mix), w_main, w_gate)

    aux = _gates(gates_raw, _lane_row(f_bias, 0), jnp.asarray(_gate_route_matrix(), BF16))

    y = _ssd(proj, gates_raw,
             conv_w[:, :SSM_D_INNER], conv_w[:, SSM_D_INNER:],
             row(conv_b[:SSM_D_INNER]), row(conv_b[SSM_D_INNER:]),
             _lane_row(dt_bias, DT_LANE), _lane_row(a_log, DT_LANE),
             jnp.asarray(_head_expand_matrix(), BF16), jnp.asarray(_conv_shift_matrix(), BF16),
             row(jnp.repeat(d_skip, SSM_HEAD_DIM)), row(ssm_norm_w))

    gain_t = lambda g: jnp.broadcast_to(jnp.tile(g.astype(F32), 2)[:, None], (LANES, ATT_T))
    o = _attention(proj, aux, gain_t(g_q), gain_t(g_k))

    x1 = _outproj(y, o, x2d, w_out)

    knt, mem_v = _memkv(mem2d, row(g_mem), xkv_w, row(xg_k))
    x2 = _xattn(x1, row(g_xattn), xq_w.astype(BF16), knt, mem_v, row(xg_q), xo_w.astype(BF16))

    return _mlp(x2, row(g_mlp), w_up, w_down)


def kernel(x, mem, g_mix, w_in, conv_w, conv_b, dt_bias, a_log, d_skip, ssm_norm_w, g_q, g_k,
           f_bias, w_out, g_xattn, g_mem, xq_w, xkv_w, xg_q, xg_k, xo_w, g_mlp, w_up, w_down):
    x2d = x.reshape(TOKENS, D_MODEL)
    mem2d = mem.reshape(BATCH * MEM_LEN, D_MODEL)
    depth = g_mix.shape[0]
    for l in range(depth):
        x2d = _layer(x2d, mem2d, g_mix[l], w_in[l], conv_w[l], conv_b[l], dt_bias[l], a_log[l],
                     d_skip[l], ssm_norm_w[l], g_q[l], g_k[l], f_bias[l], w_out[l], g_xattn[l],
                     g_mem[l], xq_w[l], xkv_w[l], xg_q[l], xg_k[l], xo_w[l], g_mlp[l], w_up[l],
                     w_down[l])
    return x2d.reshape(BATCH, SEQ, D_MODEL)
```

```python
import functools

import numpy as np
import jax
import jax.numpy as jnp
from jax import lax
from jax.experimental import pallas as pl
from jax.experimental.pallas import tpu as pltpu

F32 = jnp.float32
BF16 = jnp.bfloat16

D_MODEL = 1024
BATCH = 8
SEQ = 2048
TOKENS = BATCH * SEQ
MEM_LEN = 256
SSM_HEAD_DIM = 64
SSM_HEADS = 16
SSM_D_INNER = 1024
SSM_GROUPS = 2
SSM_STATE = 128
CONV_WIDTH = 4
CHUNK = 128
ATTN_HEAD_DIM = 64
ATTN_HEADS = 16
ATTN_WIDTH = 1024
XATTN_HEADS = 4
XATTN_HEAD_DIM = 256
D_FF = 4096
EPS = 1e-5

LANES = 128
N_MAIN = 5632
HEAD_PAIRS = ATTN_HEADS // 2
GROUP_W = SSM_D_INNER // SSM_GROUPS
DT_LANE = 16
VMEM_LIMIT = 56 * 1024 * 1024
NEG_BIG = -1e30
LOG2E = 1.4426950408889634


def _rms(xf, g_row):
    ms = jnp.mean(xf * xf, axis=-1, keepdims=True)
    return xf * lax.rsqrt(ms + EPS) * g_row


def _split2(a):
    hi = a.astype(BF16)
    mid = (a - hi.astype(F32)).astype(BF16)
    return jnp.concatenate([hi, mid], axis=-1)


def _split3(a):
    hi = a.astype(BF16)
    r1 = a - hi.astype(F32)
    mid = r1.astype(BF16)
    lo = (r1 - mid.astype(F32)).astype(BF16)
    return jnp.concatenate([hi, mid, lo], axis=-1)


def _softplus(x):
    return jnp.maximum(x, 0.0) + jnp.log1p(jnp.exp(-jnp.abs(x)))


def _silu(x):
    half = 0.5 * x
    return half + half * jnp.tanh(half)


def _dot(a, b):
    return jnp.dot(a, b, preferred_element_type=F32)


def _dot_nt(a, b):
    return lax.dot_general(a, b, (((1,), (1,)), ((), ())), preferred_element_type=F32)


def _interleave(*gens):
    live = list(gens)
    while live:
        live = [g for g in live if next(g, StopIteration) is not StopIteration]


IN_TM = 1024
IN_TN = 512


def _inproj_kernel(x_ref, g_ref, w_ref, wg_ref, proj_ref, gate_ref):
    h = _rms(x_ref[...], g_ref[...]).astype(BF16)
    for n in range(0, N_MAIN, IN_TN):
        proj_ref[:, n:n + IN_TN] = _dot(h, w_ref[:, n:n + IN_TN]).astype(BF16)
    gate_ref[...] = _dot(h, wg_ref[...])


def _inproj(x2d, g_row, w_main, w_gate):
    return pl.pallas_call(
        _inproj_kernel,
        out_shape=(jax.ShapeDtypeStruct((TOKENS, N_MAIN), BF16),
                   jax.ShapeDtypeStruct((TOKENS, LANES), F32)),
        grid=(TOKENS // IN_TM,),
        in_specs=[
            pl.BlockSpec((IN_TM, D_MODEL), lambda i: (i, 0)),
            pl.BlockSpec((1, D_MODEL), lambda i: (0, 0)),
            pl.BlockSpec((D_MODEL, N_MAIN), lambda i: (0, 0), pipeline_mode=pl.Buffered(1)),
            pl.BlockSpec((D_MODEL, LANES), lambda i: (0, 0)),
        ],
        out_specs=(pl.BlockSpec((IN_TM, N_MAIN), lambda i: (i, 0)),
                   pl.BlockSpec((IN_TM, LANES), lambda i: (i, 0))),
        compiler_params=pltpu.CompilerParams(
            dimension_semantics=("parallel",), vmem_limit_bytes=VMEM_LIMIT),
        name="in_proj",
    )(x2d, g_row, w_main, w_gate)


AUX_PAIR_LANES = 16


def _gate_route_matrix():
    r = np.zeros((3 * LANES, LANES), np.float32)
    for h in range(ATTN_HEADS):
        for m in range(3):
            base = AUX_PAIR_LANES * (h // 2)
            r[m * LANES + h, base + 2 * m + (h % 2)] = 1.0
            r[m * LANES + h, base + 6 + 2 * m + (h % 2)] = -1.0
    return r


def _gates_kernel(g_ref, fb_ref, r_ref, aux_ref):
    row = lax.broadcasted_iota(jnp.int32, (CHUNK, CHUNK), 0)
    col = lax.broadcasted_iota(jnp.int32, (CHUNK, CHUNK), 1)
    tril = jnp.where(row >= col, 1.0, 0.0).astype(BF16)
    fb = fb_ref[...]
    rmat = r_ref[...]
    offset = jnp.zeros((1, LANES), F32)
    for blk in range(SEQ // CHUNK):
        rows = slice(blk * CHUNK, (blk + 1) * CHUNK)
        log_f = -_softplus(-(g_ref[rows, :] + fb))
        part = _dot(tril, _split3(log_f))
        local = part[:, :LANES] + part[:, LANES:2 * LANES] + part[:, 2 * LANES:]
        cum = local + offset
        offset = cum[CHUNK - 1:CHUNK, :]
        aux_ref[rows, :] = _dot(_split3(cum * LOG2E), rmat).astype(BF16)


def _gates(gates_raw, fb_row, rmat):
    return pl.pallas_call(
        _gates_kernel,
        out_shape=jax.ShapeDtypeStruct((TOKENS, LANES), BF16),
        grid=(BATCH,),
        in_specs=[
            pl.BlockSpec((SEQ, LANES), lambda b: (b, 0)),
            pl.BlockSpec((1, LANES), lambda b: (0, 0)),
            pl.BlockSpec((3 * LANES, LANES), lambda b: (0, 0)),
        ],
        out_specs=pl.BlockSpec((SEQ, LANES), lambda b: (b, 0)),
        compiler_params=pltpu.CompilerParams(dimension_semantics=("parallel",)),
        name="gates",
    )(gates_raw, fb_row, rmat)


def _head_expand_matrix():
    e = np.zeros((2 * LANES, SSM_D_INNER), np.float32)
    for h in range(SSM_HEADS):
        e[DT_LANE + h, h * SSM_HEAD_DIM:(h + 1) * SSM_HEAD_DIM] = 1.0
        e[LANES + DT_LANE + h, h * SSM_HEAD_DIM:(h + 1) * SSM_HEAD_DIM] = 1.0
    return e


CONV_TAIL = 16


def _conv_shift_matrix():
    s = np.zeros(((CONV_WIDTH - 1) * CHUNK, CHUNK + CONV_TAIL), np.float32)
    for k in range(CONV_WIDTH - 1):
        for t in range(CHUNK):
            src = t - 1 - k
            s[k * CHUNK + t, src if src >= 0 else CHUNK + CONV_TAIL + src] = 1.0
    return s


def _ssd_chunk(z_ref, xs_ref, bc_ref, g_ref, cwx_ref, cwb_ref, cbx_ref, cbb_ref,
               dtb_ref, alog_ref, ee_ref, shift_ref, dsk_ref, nw_ref, y_ref,
               xtail, btail, states):
    def conv_silu(u_ref, tail, w_ref, b_ref):
        cur = u_ref[...]
        shifted = _dot(shift_ref[...], jnp.concatenate([cur, tail], axis=0))
        acc = b_ref[...] + w_ref[3:4, :] * cur.astype(F32)
        for k in range(CONV_WIDTH - 1):
            acc = acc + w_ref[2 - k:3 - k, :] * shifted[k * CHUNK:(k + 1) * CHUNK, :]
        return _silu(acc), cur[CHUNK - CONV_TAIL:, :]

    xs, xtail = conv_silu(xs_ref, xtail, cwx_ref, cbx_ref)
    bc, btail = conv_silu(bc_ref, btail, cwb_ref, cbb_ref)

    dt = _softplus(g_ref[...] + dtb_ref[...])
    da = dt * (-jnp.exp(alog_ref[...]))
    row = lax.broadcasted_iota(jnp.int32, (CHUNK, CHUNK), 0)
    col = lax.broadcasted_iota(jnp.int32, (CHUNK, CHUNK), 1)
    causal = row >= col
    tril = jnp.where(causal, 1.0, 0.0).astype(BF16)
    part = _dot(tril, _split3(da))
    acs = part[:, :LANES] + part[:, LANES:2 * LANES] + part[:, 2 * LANES:]
    a_last = acs[CHUNK - 1:CHUNK, :]
    exp_a = jnp.exp(acs)
    dt_decay = dt * jnp.exp(a_last - acs)
    ee = ee_ref[...]
    e_dtdec = _dot(_split2(dt_decay), ee)
    e_expa = _dot(_split2(exp_a), ee)
    acs_t = acs.T
    dt_t = dt.T

    xs_b = xs.astype(BF16)
    xdec_b = (xs * e_dtdec).astype(BF16)
    lane = lax.broadcasted_iota(jnp.int32, (CHUNK, LANES), 1)
    first_head = lane < SSM_HEAD_DIM

    y_parts = []
    new_states = []
    for g in range(SSM_GROUPS):
        b_g = bc[:, g * SSM_STATE:(g + 1) * SSM_STATE]
        c_g = bc[:, (SSM_GROUPS + g) * SSM_STATE:(SSM_GROUPS + g + 1) * SSM_STATE]
        c_gb = c_g.astype(BF16)
        cb = _dot_nt(c_gb, b_g.astype(BF16))
        state = states[g]
        cols = slice(g * GROUP_W, (g + 1) * GROUP_W)
        y_off = _dot(c_gb, state.astype(BF16)) * e_expa[:, cols]
        st_new = _dot(b_g.T.astype(BF16), xdec_b[:, cols])
        new_states.append(state * e_expa[CHUNK - 1:CHUNK, cols] + st_new)
        diag = []
        for pair in range(GROUP_W // LANES):
            x_pair = xs_b[:, g * GROUP_W + pair * LANES:g * GROUP_W + (pair + 1) * LANES]
            res = []
            for j in range(2):
                h = g * (SSM_HEADS // SSM_GROUPS) + 2 * pair + j
                hl = DT_LANE + h
                a_col = jnp.broadcast_to(acs[:, hl:hl + 1], (CHUNK, CHUNK))
                seg = jnp.where(causal, a_col - acs_t[hl:hl + 1, :], -jnp.inf)
                m_h = cb * jnp.exp(seg) * dt_t[hl:hl + 1, :]
                res.append(_dot(m_h.astype(BF16), x_pair))
            diag.append(jnp.where(first_head, res[0], res[1]))
        y_parts.append(jnp.concatenate(diag, axis=-1) + y_off)
    y = jnp.concatenate(y_parts, axis=-1) + dsk_ref[...] * xs
    y = y * _silu(z_ref[...].astype(F32))
    normed = []
    for g in range(SSM_GROUPS):
        y_g = y[:, g * GROUP_W:(g + 1) * GROUP_W]
        normed.append(y_g * lax.rsqrt(jnp.mean(y_g * y_g, axis=-1, keepdims=True) + EPS))
    y_ref[...] = (jnp.concatenate(normed, axis=-1) * nw_ref[...]).astype(BF16)
    return xtail, btail, new_states


SSD_CHUNKS_PER_STEP = 4


def _ssd_kernel(z_ref, xs_ref, bc_ref, g_ref, *rest):
    *consts, y_ref, xtail_ref, btail_ref, state_ref = rest

    @pl.when(pl.program_id(1) == 0)
    def _():
        xtail_ref[...] = jnp.zeros_like(xtail_ref)
        btail_ref[...] = jnp.zeros_like(btail_ref)
        state_ref[...] = jnp.zeros_like(state_ref)

    xtail, btail = xtail_ref[...], btail_ref[...]
    states = [state_ref[g] for g in range(SSM_GROUPS)]
    for sc in range(SSD_CHUNKS_PER_STEP):
        rows = pl.ds(sc * CHUNK, CHUNK)
        xtail, btail, states = _ssd_chunk(
            z_ref.at[rows], xs_ref.at[rows], bc_ref.at[rows], g_ref.at[rows], *consts,
            y_ref.at[rows], xtail, btail, states)
    xtail_ref[...] = xtail
    btail_ref[...] = btail
    for g in range(SSM_GROUPS):
        state_ref[g] = states[g]


def _ssd(proj, gates_raw, cw_x, cw_bc, cb_x, cb_bc, dtb_row, alog_row, ee, shift, dsk_row, nw_row):
    step_rows = SSD_CHUNKS_PER_STEP * CHUNK
    nsteps = SEQ // step_rows
    rowblk = lambda b, c: b * nsteps + c
    full = lambda shape: pl.BlockSpec(shape, lambda b, c: (0, 0))
    return pl.pallas_call(
        _ssd_kernel,
        out_shape=jax.ShapeDtypeStruct((TOKENS, SSM_D_INNER), BF16),
        grid=(BATCH, nsteps),
        in_specs=[
            pl.BlockSpec((step_rows, SSM_D_INNER), lambda b, c: (rowblk(b, c), 0)),
            pl.BlockSpec((step_rows, SSM_D_INNER), lambda b, c: (rowblk(b, c), 4)),
            pl.BlockSpec((step_rows, GROUP_W), lambda b, c: (rowblk(b, c), 10)),
            pl.BlockSpec((step_rows, LANES), lambda b, c: (rowblk(b, c), 0)),
            full((CONV_WIDTH, SSM_D_INNER)), full((CONV_WIDTH, GROUP_W)),
            full((1, SSM_D_INNER)), full((1, GROUP_W)),
            full((1, LANES)), full((1, LANES)),
            full((2 * LANES, SSM_D_INNER)),
            full(((CONV_WIDTH - 1) * CHUNK, CHUNK + CONV_TAIL)),
            full((1, SSM_D_INNER)), full((1, SSM_D_INNER)),
        ],
        out_specs=pl.BlockSpec((step_rows, SSM_D_INNER), lambda b, c: (rowblk(b, c), 0)),
        scratch_shapes=[
            pltpu.VMEM((CONV_TAIL, SSM_D_INNER), BF16),
            pltpu.VMEM((CONV_TAIL, GROUP_W), BF16),
            pltpu.VMEM((SSM_GROUPS, SSM_STATE, GROUP_W), F32),
        ],
        compiler_params=pltpu.CompilerParams(
            dimension_semantics=("parallel", "arbitrary"),
            vmem_limit_bytes=VMEM_LIMIT),
        name="ssd",
    )(proj, proj, proj, gates_raw, cw_x, cw_bc, cb_x, cb_bc, dtb_row, alog_row, ee, shift, dsk_row,
      nw_row)


ATT_T = 256
ATT_K = 2 * LANES


ONES_ROWS = 16
VT_HEAD_ROWS = ATTN_HEAD_DIM + ONES_ROWS


def _attn_kernel(q_ref, k_ref, v_ref, aux_ref, gq_ref, gk_ref, o_ref, ka_ref, vt_ref, qt_ref, s_ref,
                 m_ref):
    nblk = SEQ // ATT_T
    pair = pl.program_id(1)
    lane = lax.broadcasted_iota(jnp.int32, (ATT_T, LANES), 1)
    sub = lane & (AUX_PAIR_LANES - 1)
    in_pair = (lane >> 4) == pair
    is_val = in_pair & (sub < 6)
    is_neg = in_pair & (sub >= 6) & (sub < 12)
    scale = ATTN_HEAD_DIM ** -0.5 * LOG2E
    krow = lax.broadcasted_iota(jnp.int32, (ATT_T, ATT_T), 0)
    qcol = lax.broadcasted_iota(jnp.int32, (ATT_T, ATT_T), 1)
    causal_t = qcol >= krow
    trow = lax.broadcasted_iota(jnp.int32, (LANES, ATT_T), 0)
    tsub = trow & (AUX_PAIR_LANES - 1)
    t_in_pair = (trow >> 4) == pair
    t_val = t_in_pair & (tsub < 6)
    t_neg = t_in_pair & (tsub >= 6) & (tsub < 12)

    def head_norm_t(u_t, g_t):
        sq = u_t * u_t
        halves = []
        for j in range(2):
            hrows = slice(j * ATTN_HEAD_DIM, (j + 1) * ATTN_HEAD_DIM)
            ms = jnp.sum(sq[hrows, :], axis=0, keepdims=True) * (1.0 / ATTN_HEAD_DIM)
            halves.append(u_t[hrows, :] * lax.rsqrt(ms + EPS))
        return jnp.concatenate(halves, axis=0) * g_t

    for j in range(2):
        vt_ref[j * VT_HEAD_ROWS + ATTN_HEAD_DIM:(j + 1) * VT_HEAD_ROWS, :] = jnp.ones(
            (ONES_ROWS, SEQ), BF16)

    def prepare(i):
        rows = slice(i * ATT_T, (i + 1) * ATT_T)
        qn_t = head_norm_t(q_ref[rows, :].astype(F32).T, gq_ref[...] * scale)
        kn = head_norm_t(k_ref[rows, :].astype(F32).T, gk_ref[...]).T
        aux = aux_ref[rows, :].astype(F32)
        aux_t = aux.T
        k_aux = jnp.where(is_neg, aux, jnp.where(is_val, 1.0, 0.0))
        ka_ref[rows, :] = jnp.concatenate([kn, k_aux], axis=-1).astype(BF16)
        v_t = v_ref[rows, :].astype(F32).T.astype(BF16)
        for j in range(2):
            vt_ref[j * VT_HEAD_ROWS:j * VT_HEAD_ROWS + ATTN_HEAD_DIM, rows] = (
                v_t[j * ATTN_HEAD_DIM:(j + 1) * ATTN_HEAD_DIM, :])
        for j in range(2):
            mine = (tsub & 1) == j
            q_aux_t = jnp.where(t_val & mine, aux_t, jnp.where(t_neg & mine, 1.0, 0.0))
            q_main_t = jnp.where((trow >> 6) == j, qn_t, 0.0)
            qt_ref[i, j] = jnp.concatenate([q_main_t, q_aux_t], axis=0).astype(BF16)
        yield

    def scores(i):
        for j in range(2):
            qa_t = qt_ref[i, j]
            tile_max = None
            for t in range(i + 1):
                s = _dot(ka_ref[t * ATT_T:(t + 1) * ATT_T, :], qa_t)
                if t == i:
                    s = jnp.where(causal_t, s, NEG_BIG)
                s_ref[i % 2, j, t] = s
                tile_max = s if tile_max is None else jnp.maximum(tile_max, s)
                yield
            m = jnp.max(tile_max, axis=0, keepdims=True)
            m_ref[i % 2, j] = jnp.broadcast_to(m, (8, ATT_T))

    def softmax_values(i):
        outs = []
        for j in range(2):
            m = m_ref[i % 2, j][0:1, :]
            acc = None
            for t in range(i + 1):
                pv = _dot(vt_ref[j * VT_HEAD_ROWS:(j + 1) * VT_HEAD_ROWS, t * ATT_T:(t + 1) * ATT_T],
                          jnp.exp2(s_ref[i % 2, j, t] - m).astype(BF16))
                acc = pv if acc is None else acc + pv
                yield
            outs.append(acc[:ATTN_HEAD_DIM, :] / acc[ATTN_HEAD_DIM:ATTN_HEAD_DIM + 1, :])
        o_t = jnp.concatenate(outs, axis=0)
        o_ref[i * ATT_T:(i + 1) * ATT_T, :] = o_t.T.astype(BF16)

    def chain(*gens):
        for g in gens:
            yield from g

    last = nblk - 1
    _interleave(chain(prepare(last), prepare(0)))
    _interleave(scores(last), chain(*[prepare(t) for t in range(1, last)]))
    for i in range(last, -1, -1):
        _interleave(softmax_values(i), *([scores(i - 1)] if i > 0 else []))


def _attention(proj, aux, gq_row, gk_row):
    qcol = ATTN_WIDTH // LANES
    return pl.pallas_call(
        _attn_kernel,
        out_shape=jax.ShapeDtypeStruct((TOKENS, ATTN_WIDTH), BF16),
        grid=(BATCH, HEAD_PAIRS),
        in_specs=[
            pl.BlockSpec((SEQ, LANES), lambda b, hp: (b, qcol + hp)),
            pl.BlockSpec((SEQ, LANES), lambda b, hp: (b, 2 * qcol + hp)),
            pl.BlockSpec((SEQ, LANES), lambda b, hp: (b, 3 * qcol + hp)),
            pl.BlockSpec((SEQ, LANES), lambda b, hp: (b, 0)),
            pl.BlockSpec((LANES, ATT_T), lambda b, hp: (0, 0)),
            pl.BlockSpec((LANES, ATT_T), lambda b, hp: (0, 0)),
        ],
        out_specs=pl.BlockSpec((SEQ, LANES), lambda b, hp: (b, hp)),
        scratch_shapes=[
            pltpu.VMEM((SEQ, ATT_K), BF16),
            pltpu.VMEM((2 * VT_HEAD_ROWS, SEQ), BF16),
            pltpu.VMEM((SEQ // ATT_T, 2, ATT_K, ATT_T), BF16),
            pltpu.VMEM((2, 2, SEQ // ATT_T, ATT_T, ATT_T), F32),
            pltpu.VMEM((2, 2, 8, ATT_T), F32),
        ],
        compiler_params=pltpu.CompilerParams(
            dimension_semantics=("parallel", "parallel"),
            vmem_limit_bytes=VMEM_LIMIT),
        name="fox_attention",
    )(proj, proj, proj, aux, gq_row, gk_row)


OUT_TN = 256


def _outproj_kernel(y_ref, o_ref, x_ref, w_ref, out_ref):
    y = y_ref[...]
    o = o_ref[...]
    for n in range(0, D_MODEL, OUT_TN):
        cols = slice(n, n + OUT_TN)
        out_ref[:, cols] = (x_ref[:, cols]
                            + _dot(y, w_ref[:SSM_D_INNER, cols].astype(BF16))
                            + _dot(o, w_ref[SSM_D_INNER:, cols].astype(BF16)))


def _outproj(y, o, x2d, w_out, tm=1024):
    return pl.pallas_call(
        _outproj_kernel,
        out_shape=jax.ShapeDtypeStruct((TOKENS, D_MODEL), F32),
        grid=(TOKENS // tm,),
        in_specs=[
            pl.BlockSpec((tm, SSM_D_INNER), lambda i: (i, 0)),
            pl.BlockSpec((tm, ATTN_WIDTH), lambda i: (i, 0)),
            pl.BlockSpec((tm, D_MODEL), lambda i: (i, 0)),
            pl.BlockSpec((SSM_D_INNER + ATTN_WIDTH, D_MODEL), lambda i: (0, 0),
                         pipeline_mode=pl.Buffered(1)),
        ],
        out_specs=pl.BlockSpec((tm, D_MODEL), lambda i: (i, 0)),
        compiler_params=pltpu.CompilerParams(
            dimension_semantics=("parallel",), vmem_limit_bytes=VMEM_LIMIT),
        name="out_proj",
    )(y, o, x2d, w_out)


def _memkv_kernel(m_ref, g_ref, w_ref, gk_ref, knt_ref, v_ref):
    h = _rms(m_ref[...], g_ref[...]).astype(BF16)
    for a in range(XATTN_HEADS):
        cols = slice(a * XATTN_HEAD_DIM, (a + 1) * XATTN_HEAD_DIM)
        kn = _rms(_dot(h, w_ref[:, cols].astype(BF16)), gk_ref[...])
        knt_ref[cols, :] = kn.T.astype(BF16)
    for n in range(0, D_MODEL, XATTN_HEAD_DIM):
        v_ref[:, n:n + XATTN_HEAD_DIM] = _dot(
            h, w_ref[:, D_MODEL + n:D_MODEL + n + XATTN_HEAD_DIM].astype(BF16)).astype(BF16)


def _memkv(mem2d, g_row, wkv, gk_row):
    return pl.pallas_call(
        _memkv_kernel,
        out_shape=(jax.ShapeDtypeStruct((BATCH * D_MODEL, MEM_LEN), BF16),
                   jax.ShapeDtypeStruct((BATCH * MEM_LEN, D_MODEL), BF16)),
        grid=(BATCH,),
        in_specs=[
            pl.BlockSpec((MEM_LEN, D_MODEL), lambda b: (b, 0)),
            pl.BlockSpec((1, D_MODEL), lambda b: (0, 0)),
            pl.BlockSpec((D_MODEL, 2 * D_MODEL), lambda b: (0, 0), pipeline_mode=pl.Buffered(1)),
            pl.BlockSpec((1, XATTN_HEAD_DIM), lambda b: (0, 0)),
        ],
        out_specs=(pl.BlockSpec((D_MODEL, MEM_LEN), lambda b: (b, 0)),
                   pl.BlockSpec((MEM_LEN, D_MODEL), lambda b: (b, 0))),
        compiler_params=pltpu.CompilerParams(
            dimension_semantics=("parallel",), vmem_limit_bytes=VMEM_LIMIT),
        name="mem_kv",
    )(mem2d, g_row, wkv, gk_row)


XATTN_SUB = 512


def _xattn_kernel(x_ref, g_ref, wq_ref, knt_ref, v_ref, gq_ref, wo_ref, out_ref, q_scr, o_scr):
    nsub = x_ref.shape[0] // XATTN_SUB
    scale = XATTN_HEAD_DIM ** -0.5 * LOG2E
    head_cols = [slice(a * XATTN_HEAD_DIM, (a + 1) * XATTN_HEAD_DIM) for a in range(XATTN_HEADS)]

    def project(k):
        rows = slice(k * XATTN_SUB, (k + 1) * XATTN_SUB)
        h = _rms(x_ref[rows, :], g_ref[...]).astype(BF16)
        for cols in head_cols:
            q_scr[k % 2, :, cols] = _dot(h, wq_ref[:, cols])
            yield

    def attend(k):
        for cols in head_cols:
            qn = (_rms(q_scr[k % 2, :, cols], gq_ref[...]) * scale).astype(BF16)
            s = _dot(qn, knt_ref[cols, :])
            e = jnp.exp2(s - jnp.max(s, axis=-1, keepdims=True))
            p = e / jnp.sum(e, axis=-1, keepdims=True)
            o_scr[k % 2, :, cols] = _dot(p.astype(BF16), v_ref[:, cols]).astype(BF16)
            yield

    def output(k):
        rows = slice(k * XATTN_SUB, (k + 1) * XATTN_SUB)
        o = o_scr[k % 2]
        for n in range(0, D_MODEL, OUT_TN):
            cols = slice(n, n + OUT_TN)
            out_ref[rows, cols] = x_ref[rows, cols] + _dot(o, wo_ref[:, cols])
            yield

    for step in range(nsub + 2):
        stage = []
        if 0 <= step - 2 < nsub:
            stage.append(output(step - 2))
        if 0 <= step - 1 < nsub:
            stage.append(attend(step - 1))
        if step < nsub:
            stage.append(project(step))
        _interleave(*stage)


def _xattn(x1, g_row, wq, knt, v, gq_row, wo, tm=SEQ):
    nt = SEQ // tm
    return pl.pallas_call(
        _xattn_kernel,
        out_shape=jax.ShapeDtypeStruct((TOKENS, D_MODEL), F32),
        grid=(BATCH, nt),
        in_specs=[
            pl.BlockSpec((tm, D_MODEL), lambda b, i: (b * nt + i, 0)),
            pl.BlockSpec((1, D_MODEL), lambda b, i: (0, 0)),
            pl.BlockSpec((D_MODEL, D_MODEL), lambda b, i: (0, 0)),
            pl.BlockSpec((D_MODEL, MEM_LEN), lambda b, i: (b, 0)),
            pl.BlockSpec((MEM_LEN, D_MODEL), lambda b, i: (b, 0)),
            pl.BlockSpec((1, XATTN_HEAD_DIM), lambda b, i: (0, 0)),
            pl.BlockSpec((D_MODEL, D_MODEL), lambda b, i: (0, 0)),
        ],
        out_specs=pl.BlockSpec((tm, D_MODEL), lambda b, i: (b * nt + i, 0)),
        scratch_shapes=[pltpu.VMEM((2, XATTN_SUB, D_MODEL), F32),
                        pltpu.VMEM((2, XATTN_SUB, D_MODEL), BF16)],
        compiler_params=pltpu.CompilerParams(
            dimension_semantics=("parallel", "parallel"), vmem_limit_bytes=VMEM_LIMIT),
        name="mem_xattn",
    )(x1, g_row, wq, knt, v, gq_row, wo)


FF_CHUNK = 1024


def _mlp_kernel(x_ref, g_ref, wu_ref, wd_ref, out_ref, h_scr, acc_scr):
    h_scr[...] = _rms(x_ref[...], g_ref[...]).astype(BF16)
    acc_scr[...] = x_ref[...]
    for f in range(0, D_FF, FF_CHUNK):
        u = jnp.maximum(_dot(h_scr[...], wu_ref[:, f:f + FF_CHUNK].astype(BF16)), 0.0)
        acc_scr[...] += _dot((u * u).astype(BF16), wd_ref[f:f + FF_CHUNK, :].astype(BF16))
    out_ref[...] = acc_scr[...]


def _mlp(x2, g_row, wu, wd, tm=512):
    return pl.pallas_call(
        _mlp_kernel,
        out_shape=jax.ShapeDtypeStruct((TOKENS, D_MODEL), F32),
        grid=(TOKENS // tm,),
        in_specs=[
            pl.BlockSpec((tm, D_MODEL), lambda i: (i, 0)),
            pl.BlockSpec((1, D_MODEL), lambda i: (0, 0)),
            pl.BlockSpec((D_MODEL, D_FF), lambda i: (0, 0), pipeline_mode=pl.Buffered(1)),
            pl.BlockSpec((D_FF, D_MODEL), lambda i: (0, 0), pipeline_mode=pl.Buffered(1)),
        ],
        out_specs=pl.BlockSpec((tm, D_MODEL), lambda i: (i, 0)),
        scratch_shapes=[pltpu.VMEM((tm, D_MODEL), BF16), pltpu.VMEM((tm, D_MODEL), F32)],
        compiler_params=pltpu.CompilerParams(
            dimension_semantics=("parallel",), vmem_limit_bytes=VMEM_LIMIT),
        name="relu2_mlp",
    )(x2, g_row, wu, wd)


def _lane_row(vec, offset):
    return jnp.zeros((1, LANES), F32).at[0, offset:offset + vec.shape[0]].set(vec.astype(F32))


def _layer(x2d, mem2d, g_mix, w_in, conv_w, conv_b, dt_bias, a_log, d_skip, ssm_norm_w,
           g_q, g_k, f_bias, w_out, g_xattn, g_mem, xq_w, xkv_w, xg_q, xg_k, xo_w,
           g_mlp, w_up, w_down):
    z0, xbc0 = 0, SSM_D_INNER
    bc0 = xbc0 + SSM_D_INNER
    dt0 = bc0 + 2 * SSM_GROUPS * SSM_STATE
    q0 = dt0 + SSM_HEADS
    k0 = q0 + ATTN_WIDTH
    v0 = k0 + ATTN_WIDTH
    f0 = v0 + ATTN_WIDTH
    w_main = jnp.concatenate(
        [w_in[:, z0:xbc0], w_in[:, q0:f0], w_in[:, xbc0:dt0]], axis=1).astype(BF16)
    w_gate = jnp.concatenate(
        [w_in[:, f0:f0 + ATTN_HEADS], w_in[:, dt0:q0],
         jnp.zeros((D_MODEL, LANES - ATTN_HEADS - SSM_HEADS), F32)], axis=1).astype(BF16)
    row = lambda v: v.astype(F32).reshape(1, -1)

    proj, gates_raw = _inproj(x2d, row(g_mix), w_main, w_gate)

    aux = _gates(gates_raw, _lane_row(f_bias, 0), jnp.asarray(_gate_route_matrix(), BF16))

    y = _ssd(proj, gates_raw,
             conv_w[:, :SSM_D_INNER], conv_w[:, SSM_D_INNER:],
             row(conv_b[:SSM_D_INNER]), row(conv_b[SSM_D_INNER:]),
             _lane_row(dt_bias, DT_LANE), _lane_row(a_log, DT_LANE),
             jnp.asarray(_head_expand_matrix(), BF16), jnp.asarray(_conv_shift_matrix(), BF16),
             row(jnp.repeat(d_skip, SSM_HEAD_DIM)), row(ssm_norm_w))

    gain_t = lambda g: jnp.broadcast_to(jnp.tile(g.astype(F32), 2)[:, None], (LANES, ATT_T))
    o = _attention(proj, aux, gain_t(g_q), gain_t(g_k))

    x1 = _outproj(y, o, x2d, w_out)

    knt, mem_v = _memkv(mem2d, row(g_mem), xkv_w, row(xg_k))
    x2 = _xattn(x1, row(g_xattn), xq_w.astype(BF16), knt, mem_v, row(xg_q), xo_w.astype(BF16))

    return _mlp(x2, row(g_mlp), w_up, w_down)


def kernel(x, mem, g_mix, w_in, conv_w, conv_b, dt_bias, a_log, d_skip, ssm_norm_w, g_q, g_k,
           f_bias, w_out, g_xattn, g_mem, xq_w, xkv_w, xg_q, xg_k, xo_w, g_mlp, w_up, w_down):
    x2d = x.reshape(TOKENS, D_MODEL)
    mem2d = mem.reshape(BATCH * MEM_LEN, D_MODEL)
    depth = g_mix.shape[0]
    for l in range(depth):
        x2d = _layer(x2d, mem2d, g_mix[l], w_in[l], conv_w[l], conv_b[l], dt_bias[l], a_log[l],
                     d_skip[l], ssm_norm_w[l], g_q[l], g_k[l], f_bias[l], w_out[l], g_xattn[l],
                     g_mem[l], xq_w[l], xkv_w[l], xg_q[l], xg_k[l], xo_w[l], g_mlp[l], w_up[l],
                     w_down[l])
    return x2d.reshape(BATCH, SEQ, D_MODEL)
```

```python
import functools

import numpy as np
import jax
import jax.numpy as jnp
from jax import lax
from jax.experimental import pallas as pl
from jax.experimental.pallas import tpu as pltpu

F32 = jnp.float32
BF16 = jnp.bfloat16

D_MODEL = 1024
BATCH = 8
SEQ = 2048
TOKENS = BATCH * SEQ
MEM_LEN = 256
SSM_HEAD_DIM = 64
SSM_HEADS = 16
SSM_D_INNER = 1024
SSM_GROUPS = 2
SSM_STATE = 128
CONV_WIDTH = 4
CHUNK = 128
ATTN_HEAD_DIM = 64
ATTN_HEADS = 16
ATTN_WIDTH = 1024
XATTN_HEADS = 4
XATTN_HEAD_DIM = 256
D_FF = 4096
EPS = 1e-5

LANES = 128
N_MAIN = 5632
HEAD_PAIRS = ATTN_HEADS // 2
GROUP_W = SSM_D_INNER // SSM_GROUPS
DT_LANE = 16
VMEM_LIMIT = 56 * 1024 * 1024
NEG_BIG = -1e30
LOG2E = 1.4426950408889634


def _rms(xf, g_row):
    ms = jnp.mean(xf * xf, axis=-1, keepdims=True)
    return xf * lax.rsqrt(ms + EPS) * g_row


def _split2(a):
    hi = a.astype(BF16)
    mid = (a - hi.astype(F32)).astype(BF16)
    return jnp.concatenate([hi, mid], axis=-1)


def _split3(a):
    hi = a.astype(BF16)
    r1 = a - hi.astype(F32)
    mid = r1.astype(BF16)
    lo = (r1 - mid.astype(F32)).astype(BF16)
    return jnp.concatenate([hi, mid, lo], axis=-1)


def _softplus(x):
    return jnp.maximum(x, 0.0) + jnp.log1p(jnp.exp(-jnp.abs(x)))


def _silu(x):
    half = 0.5 * x
    return half + half * jnp.tanh(half)


def _dot(a, b):
    return jnp.dot(a, b, preferred_element_type=F32)


def _dot_nt(a, b):
    return lax.dot_general(a, b, (((1,), (1,)), ((), ())), preferred_element_type=F32)


def _interleave(*gens):
    live = list(gens)
    while live:
        live = [g for g in live if next(g, StopIteration) is not StopIteration]


IN_TM = 1024
IN_TN = 512


def _inproj_kernel(x_ref, g_ref, w_ref, wg_ref, proj_ref, gate_ref):
    h = _rms(x_ref[...], g_ref[...]).astype(BF16)
    for n in range(0, N_MAIN, IN_TN):
        proj_ref[:, n:n + IN_TN] = _dot(h, w_ref[:, n:n + IN_TN]).astype(BF16)
    gate_ref[...] = _dot(h, wg_ref[...])


def _inproj(x2d, g_row, w_main, w_gate):
    return pl.pallas_call(
        _inproj_kernel,
        out_shape=(jax.ShapeDtypeStruct((TOKENS, N_MAIN), BF16),
                   jax.ShapeDtypeStruct((TOKENS, LANES), F32)),
        grid=(TOKENS // IN_TM,),
        in_specs=[
            pl.BlockSpec((IN_TM, D_MODEL), lambda i: (i, 0)),
            pl.BlockSpec((1, D_MODEL), lambda i: (0, 0)),
            pl.BlockSpec((D_MODEL, N_MAIN), lambda i: (0, 0), pipeline_mode=pl.Buffered(1)),
            pl.BlockSpec((D_MODEL, LANES), lambda i: (0, 0)),
        ],
        out_specs=(pl.BlockSpec((IN_TM, N_MAIN), lambda i: (i, 0)),
                   pl.BlockSpec((IN_TM, LANES), lambda i: (i, 0))),
        compiler_params=pltpu.CompilerParams(
            dimension_semantics=("parallel",), vmem_limit_bytes=VMEM_LIMIT),
        name="in_proj",
    )(x2d, g_row, w_main, w_gate)


AUX_PAIR_LANES = 16


GATE_ROWS = 16


def _gate_route_matrix():
    r = np.zeros((LANES, 3 * GATE_ROWS), np.float32)
    for h in range(ATTN_HEADS):
        for m in range(3):
            base = AUX_PAIR_LANES * (h // 2)
            r[base + 2 * m + (h % 2), GATE_ROWS * m + h] = 1.0
            r[base + 6 + 2 * m + (h % 2), GATE_ROWS * m + h] = -1.0
    return r


def _split3_rows(a):
    hi = a.astype(BF16)
    r1 = a - hi.astype(F32)
    mid = r1.astype(BF16)
    lo = (r1 - mid.astype(F32)).astype(BF16)
    return jnp.concatenate([hi, mid, lo], axis=0)


def _gates_kernel(g_ref, fb_ref, r_ref, aux_ref):
    row = lax.broadcasted_iota(jnp.int32, (CHUNK, CHUNK), 0)
    col = lax.broadcasted_iota(jnp.int32, (CHUNK, CHUNK), 1)
    triu = jnp.where(row <= col, 1.0, 0.0).astype(BF16)
    fb = fb_ref[...]
    rmat = r_ref[...]
    offset = jnp.zeros((GATE_ROWS, CHUNK), F32)
    for blk in range(SEQ // CHUNK):
        rows = slice(blk * CHUNK, (blk + 1) * CHUNK)
        f_t = g_ref[rows, :].T[:GATE_ROWS, :]
        log_f = -_softplus(-(f_t + fb))
        part = _dot(_split3_rows(log_f), triu)
        cum = (part[:GATE_ROWS] + part[GATE_ROWS:2 * GATE_ROWS] + part[2 * GATE_ROWS:]) + offset
        offset = jnp.broadcast_to(cum[:, CHUNK - 1:CHUNK], (GATE_ROWS, CHUNK))
        aux_t = _dot(rmat, _split3_rows(cum * LOG2E))
        aux_ref[rows, :] = aux_t.T.astype(BF16)


def _gates(gates_raw, fb_t, rmat):
    return pl.pallas_call(
        _gates_kernel,
        out_shape=jax.ShapeDtypeStruct((TOKENS, LANES), BF16),
        grid=(BATCH,),
        in_specs=[
            pl.BlockSpec((SEQ, LANES), lambda b: (b, 0)),
            pl.BlockSpec((GATE_ROWS, CHUNK), lambda b: (0, 0)),
            pl.BlockSpec((LANES, 3 * GATE_ROWS), lambda b: (0, 0)),
        ],
        out_specs=pl.BlockSpec((SEQ, LANES), lambda b: (b, 0)),
        compiler_params=pltpu.CompilerParams(dimension_semantics=("parallel",)),
        name="gates",
    )(gates_raw, fb_t, rmat)


def _head_expand_matrix():
    e = np.zeros((2 * LANES, SSM_D_INNER), np.float32)
    for h in range(SSM_HEADS):
        e[DT_LANE + h, h * SSM_HEAD_DIM:(h + 1) * SSM_HEAD_DIM] = 1.0
        e[LANES + DT_LANE + h, h * SSM_HEAD_DIM:(h + 1) * SSM_HEAD_DIM] = 1.0
    return e


CONV_TAIL = 16


def _conv_shift_matrix():
    s = np.zeros(((CONV_WIDTH - 1) * CHUNK, CHUNK + CONV_TAIL), np.float32)
    for k in range(CONV_WIDTH - 1):
        for t in range(CHUNK):
            src = t - 1 - k
            s[k * CHUNK + t, src if src >= 0 else CHUNK + CONV_TAIL + src] = 1.0
    return s


def _ssd_chunk(z_ref, xs_ref, bc_ref, g_ref, cwx_ref, cwb_ref, cbx_ref, cbb_ref,
               dtb_ref, alog_ref, ee_ref, shift_ref, dsk_ref, nw_ref, y_ref,
               xtail, btail, states):
    def conv_silu(u_ref, tail, w_ref, b_ref):
        cur = u_ref[...]
        shifted = _dot(shift_ref[...], jnp.concatenate([cur, tail], axis=0))
        acc = b_ref[...] + w_ref[3:4, :] * cur.astype(F32)
        for k in range(CONV_WIDTH - 1):
            acc = acc + w_ref[2 - k:3 - k, :] * shifted[k * CHUNK:(k + 1) * CHUNK, :]
        return _silu(acc), cur[CHUNK - CONV_TAIL:, :]

    xs, xtail = conv_silu(xs_ref, xtail, cwx_ref, cbx_ref)
    bc, btail = conv_silu(bc_ref, btail, cwb_ref, cbb_ref)

    dt = _softplus(g_ref[...] + dtb_ref[...])
    da = dt * (-jnp.exp(alog_ref[...]))
    row = lax.broadcasted_iota(jnp.int32, (CHUNK, CHUNK), 0)
    col = lax.broadcasted_iota(jnp.int32, (CHUNK, CHUNK), 1)
    causal = row >= col
    tril = jnp.where(causal, 1.0, 0.0).astype(BF16)
    part = _dot(tril, _split3(da))
    acs = part[:, :LANES] + part[:, LANES:2 * LANES] + part[:, 2 * LANES:]
    a_last = acs[CHUNK - 1:CHUNK, :]
    exp_a = jnp.exp(acs)
    dt_decay = dt * jnp.exp(a_last - acs)
    ee = ee_ref[...]
    e_dtdec = _dot(_split2(dt_decay), ee)
    e_expa = _dot(_split2(exp_a), ee)
    acs_t = acs.T
    dt_t = dt.T

    xs_b = xs.astype(BF16)
    xdec_b = (xs * e_dtdec).astype(BF16)
    lane = lax.broadcasted_iota(jnp.int32, (CHUNK, LANES), 1)
    first_head = lane < SSM_HEAD_DIM

    y_parts = []
    new_states = []
    for g in range(SSM_GROUPS):
        b_g = bc[:, g * SSM_STATE:(g + 1) * SSM_STATE]
        c_g = bc[:, (SSM_GROUPS + g) * SSM_STATE:(SSM_GROUPS + g + 1) * SSM_STATE]
        c_gb = c_g.astype(BF16)
        cb = _dot_nt(c_gb, b_g.astype(BF16))
        state = states[g]
        cols = slice(g * GROUP_W, (g + 1) * GROUP_W)
        y_off = _dot(c_gb, state.astype(BF16)) * e_expa[:, cols]
        st_new = _dot(b_g.T.astype(BF16), xdec_b[:, cols])
        new_states.append(state * e_expa[CHUNK - 1:CHUNK, cols] + st_new)
        diag = []
        for pair in range(GROUP_W // LANES):
            x_pair = xs_b[:, g * GROUP_W + pair * LANES:g * GROUP_W + (pair + 1) * LANES]
            res = []
            for j in range(2):
                h = g * (SSM_HEADS // SSM_GROUPS) + 2 * pair + j
                hl = DT_LANE + h
                a_col = jnp.broadcast_to(acs[:, hl:hl + 1], (CHUNK, CHUNK))
                seg = jnp.where(causal, a_col - acs_t[hl:hl + 1, :], -jnp.inf)
                m_h = cb * jnp.exp(seg) * dt_t[hl:hl + 1, :]
                res.append(_dot(m_h.astype(BF16), x_pair))
            diag.append(jnp.where(first_head, res[0], res[1]))
        y_parts.append(jnp.concatenate(diag, axis=-1) + y_off)
    y = jnp.concatenate(y_parts, axis=-1) + dsk_ref[...] * xs
    y = y * _silu(z_ref[...].astype(F32))
    normed = []
    for g in range(SSM_GROUPS):
        y_g = y[:, g * GROUP_W:(g + 1) * GROUP_W]
        normed.append(y_g * lax.rsqrt(jnp.mean(y_g * y_g, axis=-1, keepdims=True) + EPS))
    y_ref[...] = (jnp.concatenate(normed, axis=-1) * nw_ref[...]).astype(BF16)
    return xtail, btail, new_states


SSD_CHUNKS_PER_STEP = 8


def _ssd_kernel(z_ref, xs_ref, bc_ref, g_ref, *rest):
    *consts, y_ref, xtail_ref, btail_ref, state_ref = rest

    @pl.when(pl.program_id(1) == 0)
    def _():
        xtail_ref[...] = jnp.zeros_like(xtail_ref)
        btail_ref[...] = jnp.zeros_like(btail_ref)
        state_ref[...] = jnp.zeros_like(state_ref)

    xtail, btail = xtail_ref[...], btail_ref[...]
    states = [state_ref[g] for g in range(SSM_GROUPS)]
    for sc in range(SSD_CHUNKS_PER_STEP):
        rows = pl.ds(sc * CHUNK, CHUNK)
        xtail, btail, states = _ssd_chunk(
            z_ref.at[rows], xs_ref.at[rows], bc_ref.at[rows], g_ref.at[rows], *consts,
            y_ref.at[rows], xtail, btail, states)
    xtail_ref[...] = xtail
    btail_ref[...] = btail
    for g in range(SSM_GROUPS):
        state_ref[g] = states[g]


def _ssd(proj, gates_raw, cw_x, cw_bc, cb_x, cb_bc, dtb_row, alog_row, ee, shift, dsk_row, nw_row):
    step_rows = SSD_CHUNKS_PER_STEP * CHUNK
    nsteps = SEQ // step_rows
    rowblk = lambda b, c: b * nsteps + c
    full = lambda shape: pl.BlockSpec(shape, lambda b, c: (0, 0))
    return pl.pallas_call(
        _ssd_kernel,
        out_shape=jax.ShapeDtypeStruct((TOKENS, SSM_D_INNER), BF16),
        grid=(BATCH, nsteps),
        in_specs=[
            pl.BlockSpec((step_rows, SSM_D_INNER), lambda b, c: (rowblk(b, c), 0)),
            pl.BlockSpec((step_rows, SSM_D_INNER), lambda b, c: (rowblk(b, c), 4)),
            pl.BlockSpec((step_rows, GROUP_W), lambda b, c: (rowblk(b, c), 10)),
            pl.BlockSpec((step_rows, LANES), lambda b, c: (rowblk(b, c), 0)),
            full((CONV_WIDTH, SSM_D_INNER)), full((CONV_WIDTH, GROUP_W)),
            full((1, SSM_D_INNER)), full((1, GROUP_W)),
            full((1, LANES)), full((1, LANES)),
            full((2 * LANES, SSM_D_INNER)),
            full(((CONV_WIDTH - 1) * CHUNK, CHUNK + CONV_TAIL)),
            full((1, SSM_D_INNER)), full((1, SSM_D_INNER)),
        ],
        out_specs=pl.BlockSpec((step_rows, SSM_D_INNER), lambda b, c: (rowblk(b, c), 0)),
        scratch_shapes=[
            pltpu.VMEM((CONV_TAIL, SSM_D_INNER), BF16),
            pltpu.VMEM((CONV_TAIL, GROUP_W), BF16),
            pltpu.VMEM((SSM_GROUPS, SSM_STATE, GROUP_W), F32),
        ],
        compiler_params=pltpu.CompilerParams(
            dimension_semantics=("parallel", "arbitrary"),
            vmem_limit_bytes=VMEM_LIMIT),
        name="ssd",
    )(proj, proj, proj, gates_raw, cw_x, cw_bc, cb_x, cb_bc, dtb_row, alog_row, ee, shift, dsk_row,
      nw_row)


ATT_T = 256
ATT_K = 2 * LANES


ONES_ROWS = 16
VT_HEAD_ROWS = ATTN_HEAD_DIM + ONES_ROWS


def _attn_kernel(q_ref, k_ref, v_ref, aux_ref, gq_ref, gk_ref, o_ref, ka_ref, vt_ref, qt_ref, s_ref,
                 m_ref):
    nblk = SEQ // ATT_T
    pair = pl.program_id(1)
    lane = lax.broadcasted_iota(jnp.int32, (ATT_T, LANES), 1)
    sub = lane & (AUX_PAIR_LANES - 1)
    in_pair = (lane >> 4) == pair
    is_val = in_pair & (sub < 6)
    is_neg = in_pair & (sub >= 6) & (sub < 12)
    scale = ATTN_HEAD_DIM ** -0.5 * LOG2E
    krow = lax.broadcasted_iota(jnp.int32, (ATT_T, ATT_T), 0)
    qcol = lax.broadcasted_iota(jnp.int32, (ATT_T, ATT_T), 1)
    causal_t = qcol >= krow
    trow = lax.broadcasted_iota(jnp.int32, (LANES, ATT_T), 0)
    tsub = trow & (AUX_PAIR_LANES - 1)
    t_in_pair = (trow >> 4) == pair
    t_val = t_in_pair & (tsub < 6)
    t_neg = t_in_pair & (tsub >= 6) & (tsub < 12)

    def head_norm_t(u_t, g_t):
        sq = u_t * u_t
        halves = []
        for j in range(2):
            hrows = slice(j * ATTN_HEAD_DIM, (j + 1) * ATTN_HEAD_DIM)
            ms = jnp.sum(sq[hrows, :], axis=0, keepdims=True) * (1.0 / ATTN_HEAD_DIM)
            halves.append(u_t[hrows, :] * lax.rsqrt(ms + EPS))
        return jnp.concatenate(halves, axis=0) * g_t

    for j in range(2):
        vt_ref[j * VT_HEAD_ROWS + ATTN_HEAD_DIM:(j + 1) * VT_HEAD_ROWS, :] = jnp.ones(
            (ONES_ROWS, SEQ), BF16)

    def prepare(i):
        rows = slice(i * ATT_T, (i + 1) * ATT_T)
        qn_t = head_norm_t(q_ref[rows, :].astype(F32).T, gq_ref[...] * scale)
        kn = head_norm_t(k_ref[rows, :].astype(F32).T, gk_ref[...]).T
        aux = aux_ref[rows, :].astype(F32)
        aux_t = aux.T
        k_aux = jnp.where(is_neg, aux, jnp.where(is_val, 1.0, 0.0))
        ka_ref[rows, :] = jnp.concatenate([kn, k_aux], axis=-1).astype(BF16)
        v_t = v_ref[rows, :].astype(F32).T.astype(BF16)
        for j in range(2):
            vt_ref[j * VT_HEAD_ROWS:j * VT_HEAD_ROWS + ATTN_HEAD_DIM, rows] = (
                v_t[j * ATTN_HEAD_DIM:(j + 1) * ATTN_HEAD_DIM, :])
        for j in range(2):
            mine = (tsub & 1) == j
            q_aux_t = jnp.where(t_val & mine, aux_t, jnp.where(t_neg & mine, 1.0, 0.0))
            q_main_t = jnp.where((trow >> 6) == j, qn_t, 0.0)
            qt_ref[i, j] = jnp.concatenate([q_main_t, q_aux_t], axis=0).astype(BF16)
        yield

    def scores(i):
        for j in range(2):
            qa_t = qt_ref[i, j]
            tile_max = None
            for t in range(i + 1):
                s = _dot(ka_ref[t * ATT_T:(t + 1) * ATT_T, :], qa_t)
                if t == i:
                    s = jnp.where(causal_t, s, NEG_BIG)
                s_ref[i % 2, j, t] = s
                tile_max = s if tile_max is None else jnp.maximum(tile_max, s)
                yield
            m = jnp.max(tile_max, axis=0, keepdims=True)
            m_ref[i % 2, j] = jnp.broadcast_to(m, (8, ATT_T))

    def softmax_values(i):
        outs = []
        for j in range(2):
            m = m_ref[i % 2, j][0:1, :]
            acc = None
            for t in range(i + 1):
                pv = _dot(vt_ref[j * VT_HEAD_ROWS:(j + 1) * VT_HEAD_ROWS, t * ATT_T:(t + 1) * ATT_T],
                          jnp.exp2(s_ref[i % 2, j, t] - m).astype(BF16))
                acc = pv if acc is None else acc + pv
                yield
            outs.append(acc[:ATTN_HEAD_DIM, :] / acc[ATTN_HEAD_DIM:ATTN_HEAD_DIM + 1, :])
        o_t = jnp.concatenate(outs, axis=0)
        o_ref[i * ATT_T:(i + 1) * ATT_T, :] = o_t.T.astype(BF16)

    def chain(*gens):
        for g in gens:
            yield from g

    last = nblk - 1
    _interleave(chain(prepare(last), prepare(0)))
    _interleave(scores(last), chain(*[prepare(t) for t in range(1, last)]))
    for i in range(last, -1, -1):
        _interleave(softmax_values(i), *([scores(i - 1)] if i > 0 else []))


def _attention(proj, aux, gq_row, gk_row):
    qcol = ATTN_WIDTH // LANES
    return pl.pallas_call(
        _attn_kernel,
        out_shape=jax.ShapeDtypeStruct((TOKENS, ATTN_WIDTH), BF16),
        grid=(BATCH, HEAD_PAIRS),
        in_specs=[
            pl.BlockSpec((SEQ, LANES), lambda b, hp: (b, qcol + hp)),
            pl.BlockSpec((SEQ, LANES), lambda b, hp: (b, 2 * qcol + hp)),
            pl.BlockSpec((SEQ, LANES), lambda b, hp: (b, 3 * qcol + hp)),
            pl.BlockSpec((SEQ, LANES), lambda b, hp: (b, 0)),
            pl.BlockSpec((LANES, ATT_T), lambda b, hp: (0, 0)),
            pl.BlockSpec((LANES, ATT_T), lambda b, hp: (0, 0)),
        ],
        out_specs=pl.BlockSpec((SEQ, LANES), lambda b, hp: (b, hp)),
        scratch_shapes=[
            pltpu.VMEM((SEQ, ATT_K), BF16),
            pltpu.VMEM((2 * VT_HEAD_ROWS, SEQ), BF16),
            pltpu.VMEM((SEQ // ATT_T, 2, ATT_K, ATT_T), BF16),
            pltpu.VMEM((2, 2, SEQ // ATT_T, ATT_T, ATT_T), F32),
            pltpu.VMEM((2, 2, 8, ATT_T), F32),
        ],
        compiler_params=pltpu.CompilerParams(
            dimension_semantics=("parallel", "parallel"),
            vmem_limit_bytes=VMEM_LIMIT),
        name="fox_attention",
    )(proj, proj, proj, aux, gq_row, gk_row)


OUT_TN = 256


def _outproj_kernel(y_ref, o_ref, x_ref, w_ref, out_ref):
    y = y_ref[...]
    o = o_ref[...]
    for n in range(0, D_MODEL, OUT_TN):
        cols = slice(n, n + OUT_TN)
        out_ref[:, cols] = (x_ref[:, cols]
                            + _dot(y, w_ref[:SSM_D_INNER, cols].astype(BF16))
                            + _dot(o, w_ref[SSM_D_INNER:, cols].astype(BF16)))


def _outproj(y, o, x2d, w_out, tm=1024):
    return pl.pallas_call(
        _outproj_kernel,
        out_shape=jax.ShapeDtypeStruct((TOKENS, D_MODEL), F32),
        grid=(TOKENS // tm,),
        in_specs=[
            pl.BlockSpec((tm, SSM_D_INNER), lambda i: (i, 0)),
            pl.BlockSpec((tm, ATTN_WIDTH), lambda i: (i, 0)),
            pl.BlockSpec((tm, D_MODEL), lambda i: (i, 0)),
            pl.BlockSpec((SSM_D_INNER + ATTN_WIDTH, D_MODEL), lambda i: (0, 0),
                         pipeline_mode=pl.Buffered(1)),
        ],
        out_specs=pl.BlockSpec((tm, D_MODEL), lambda i: (i, 0)),
        compiler_params=pltpu.CompilerParams(
            dimension_semantics=("parallel",), vmem_limit_bytes=VMEM_LIMIT),
        name="out_proj",
    )(y, o, x2d, w_out)


def _memkv_kernel(m_ref, g_ref, w_ref, gk_ref, knt_ref, v_ref):
    h = _rms(m_ref[...], g_ref[...]).astype(BF16)
    for a in range(XATTN_HEADS):
        cols = slice(a * XATTN_HEAD_DIM, (a + 1) * XATTN_HEAD_DIM)
        kn = _rms(_dot(h, w_ref[:, cols].astype(BF16)), gk_ref[...])
        knt_ref[cols, :] = kn.T.astype(BF16)
    for n in range(0, D_MODEL, XATTN_HEAD_DIM):
        v_ref[:, n:n + XATTN_HEAD_DIM] = _dot(
            h, w_ref[:, D_MODEL + n:D_MODEL + n + XATTN_HEAD_DIM].astype(BF16)).astype(BF16)


def _memkv(mem2d, g_row, wkv, gk_row):
    return pl.pallas_call(
        _memkv_kernel,
        out_shape=(jax.ShapeDtypeStruct((BATCH * D_MODEL, MEM_LEN), BF16),
                   jax.ShapeDtypeStruct((BATCH * MEM_LEN, D_MODEL), BF16)),
        grid=(BATCH,),
        in_specs=[
            pl.BlockSpec((MEM_LEN, D_MODEL), lambda b: (b, 0)),
            pl.BlockSpec((1, D_MODEL), lambda b: (0, 0)),
            pl.BlockSpec((D_MODEL, 2 * D_MODEL), lambda b: (0, 0), pipeline_mode=pl.Buffered(1)),
            pl.BlockSpec((1, XATTN_HEAD_DIM), lambda b: (0, 0)),
        ],
        out_specs=(pl.BlockSpec((D_MODEL, MEM_LEN), lambda b: (b, 0)),
                   pl.BlockSpec((MEM_LEN, D_MODEL), lambda b: (b, 0))),
        compiler_params=pltpu.CompilerParams(
            dimension_semantics=("parallel",), vmem_limit_bytes=VMEM_LIMIT),
        name="mem_kv",
    )(mem2d, g_row, wkv, gk_row)


XATTN_SUB = 512


def _xattn_kernel(x_ref, g_ref, wq_ref, knt_ref, v_ref, gq_ref, wo_ref, out_ref, q_scr, o_scr):
    nsub = x_ref.shape[0] // XATTN_SUB
    scale = XATTN_HEAD_DIM ** -0.5 * LOG2E
    head_cols = [slice(a * XATTN_HEAD_DIM, (a + 1) * XATTN_HEAD_DIM) for a in range(XATTN_HEADS)]

    def project(k):
        rows = slice(k * XATTN_SUB, (k + 1) * XATTN_SUB)
        h = _rms(x_ref[rows, :], g_ref[...]).astype(BF16)
        for cols in head_cols:
            q_scr[k % 2, :, cols] = _dot(h, wq_ref[:, cols])
            yield

    def attend(k):
        for cols in head_cols:
            qn = (_rms(q_scr[k % 2, :, cols], gq_ref[...]) * scale).astype(BF16)
            s = _dot(qn, knt_ref[cols, :])
            e = jnp.exp2(s - jnp.max(s, axis=-1, keepdims=True))
            p = e / jnp.sum(e, axis=-1, keepdims=True)
            o_scr[k % 2, :, cols] = _dot(p.astype(BF16), v_ref[:, cols]).astype(BF16)
            yield

    def output(k):
        rows = slice(k * XATTN_SUB, (k + 1) * XATTN_SUB)
        o = o_scr[k % 2]
        for n in range(0, D_MODEL, OUT_TN):
            cols = slice(n, n + OUT_TN)
            out_ref[rows, cols] = x_ref[rows, cols] + _dot(o, wo_ref[:, cols])
            yield

    for step in range(nsub + 2):
        stage = []
        if 0 <= step - 2 < nsub:
            stage.append(output(step - 2))
        if 0 <= step - 1 < nsub:
            stage.append(attend(step - 1))
        if step < nsub:
            stage.append(project(step))
        _interleave(*stage)


def _xattn(x1, g_row, wq, knt, v, gq_row, wo, tm=SEQ):
    nt = SEQ // tm
    return pl.pallas_call(
        _xattn_kernel,
        out_shape=jax.ShapeDtypeStruct((TOKENS, D_MODEL), F32),
        grid=(BATCH, nt),
        in_specs=[
            pl.BlockSpec((tm, D_MODEL), lambda b, i: (b * nt + i, 0)),
            pl.BlockSpec((1, D_MODEL), lambda b, i: (0, 0)),
            pl.BlockSpec((D_MODEL, D_MODEL), lambda b, i: (0, 0)),
            pl.BlockSpec((D_MODEL, MEM_LEN), lambda b, i: (b, 0)),
            pl.BlockSpec((MEM_LEN, D_MODEL), lambda b, i: (b, 0)),
            pl.BlockSpec((1, XATTN_HEAD_DIM), lambda b, i: (0, 0)),
            pl.BlockSpec((D_MODEL, D_MODEL), lambda b, i: (0, 0)),
        ],
        out_specs=pl.BlockSpec((tm, D_MODEL), lambda b, i: (b * nt + i, 0)),
        scratch_shapes=[pltpu.VMEM((2, XATTN_SUB, D_MODEL), F32),
                        pltpu.VMEM((2, XATTN_SUB, D_MODEL), BF16)],
        compiler_params=pltpu.CompilerParams(
            dimension_semantics=("parallel", "parallel"), vmem_limit_bytes=VMEM_LIMIT),
        name="mem_xattn",
    )(x1, g_row, wq, knt, v, gq_row, wo)


FF_CHUNK = 1024


def _mlp_kernel(x_ref, g_ref, wu_ref, wd_ref, out_ref, h_scr, acc_scr):
    h_scr[...] = _rms(x_ref[...], g_ref[...]).astype(BF16)
    acc_scr[...] = x_ref[...]
    for f in range(0, D_FF, FF_CHUNK):
        u = jnp.maximum(_dot(h_scr[...], wu_ref[:, f:f + FF_CHUNK].astype(BF16)), 0.0)
        acc_scr[...] += _dot((u * u).astype(BF16), wd_ref[f:f + FF_CHUNK, :].astype(BF16))
    out_ref[...] = acc_scr[...]


def _mlp(x2, g_row, wu, wd, tm=512):
    return pl.pallas_call(
        _mlp_kernel,
        out_shape=jax.ShapeDtypeStruct((TOKENS, D_MODEL), F32),
        grid=(TOKENS // tm,),
        in_specs=[
            pl.BlockSpec((tm, D_MODEL), lambda i: (i, 0)),
            pl.BlockSpec((1, D_MODEL), lambda i: (0, 0)),
            pl.BlockSpec((D_MODEL, D_FF), lambda i: (0, 0), pipeline_mode=pl.Buffered(1)),
            pl.BlockSpec((D_FF, D_MODEL), lambda i: (0, 0), pipeline_mode=pl.Buffered(1)),
        ],
        out_specs=pl.BlockSpec((tm, D_MODEL), lambda i: (i, 0)),
        scratch_shapes=[pltpu.VMEM((tm, D_MODEL), BF16), pltpu.VMEM((tm, D_MODEL), F32)],
        compiler_params=pltpu.CompilerParams(
            dimension_semantics=("parallel",), vmem_limit_bytes=VMEM_LIMIT),
        name="relu2_mlp",
    )(x2, g_row, wu, wd)


def _lane_row(vec, offset):
    return jnp.zeros((1, LANES), F32).at[0, offset:offset + vec.shape[0]].set(vec.astype(F32))


def _layer(x2d, mem2d, g_mix, w_in, conv_w, conv_b, dt_bias, a_log, d_skip, ssm_norm_w,
           g_q, g_k, f_bias, w_out, g_xattn, g_mem, xq_w, xkv_w, xg_q, xg_k, xo_w,
           g_mlp, w_up, w_down):
    z0, xbc0 = 0, SSM_D_INNER
    bc0 = xbc0 + SSM_D_INNER
    dt0 = bc0 + 2 * SSM_GROUPS * SSM_STATE
    q0 = dt0 + SSM_HEADS
    k0 = q0 + ATTN_WIDTH
    v0 = k0 + ATTN_WIDTH
    f0 = v0 + ATTN_WIDTH
    w_main = jnp.concatenate(
        [w_in[:, z0:xbc0], w_in[:, q0:f0], w_in[:, xbc0:dt0]], axis=1).astype(BF16)
    w_gate = jnp.concatenate(
        [w_in[:, f0:f0 + ATTN_HEADS], w_in[:, dt0:q0],
         jnp.zeros((D_MODEL, LANES - ATTN_HEADS - SSM_HEADS), F32)], axis=1).astype(BF16)
    row = lambda v: v.astype(F32).reshape(1, -1)

    proj, gates_raw = _inproj(x2d, row(g_mix), w_main, w_gate)

    fb_t = jnp.broadcast_to(f_bias.astype(F32)[:, None], (GATE_ROWS, CHUNK))
    aux = _gates(gates_raw, fb_t, jnp.asarray(_gate_route_matrix(), BF16))

    y = _ssd(proj, gates_raw,
             conv_w[:, :SSM_D_INNER], conv_w[:, SSM_D_INNER:],
             row(conv_b[:SSM_D_INNER]), row(conv_b[SSM_D_INNER:]),
             _lane_row(dt_bias, DT_LANE), _lane_row(a_log, DT_LANE),
             jnp.asarray(_head_expand_matrix(), BF16), jnp.asarray(_conv_shift_matrix(), BF16),
             row(jnp.repeat(d_skip, SSM_HEAD_DIM)), row(ssm_norm_w))

    gain_t = lambda g: jnp.broadcast_to(jnp.tile(g.astype(F32), 2)[:, None], (LANES, ATT_T))
    o = _attention(proj, aux, gain_t(g_q), gain_t(g_k))

    x1 = _outproj(y, o, x2d, w_out)

    knt, mem_v = _memkv(mem2d, row(g_mem), xkv_w, row(xg_k))
    x2 = _xattn(x1, row(g_xattn), xq_w.astype(BF16), knt, mem_v, row(xg_q), xo_w.astype(BF16))

    return _mlp(x2, row(g_mlp), w_up, w_down)


def kernel(x, mem, g_mix, w_in, conv_w, conv_b, dt_bias, a_log, d_skip, ssm_norm_w, g_q, g_k,
           f_bias, w_out, g_xattn, g_mem, xq_w, xkv_w, xg_q, xg_k, xo_w, g_mlp, w_up, w_down):
    x2d = x.reshape(TOKENS, D_MODEL)
    mem2d = mem.reshape(BATCH * MEM_LEN, D_MODEL)
    depth = g_mix.shape[0]
    for l in range(depth):
        x2d = _layer(x2d, mem2d, g_mix[l], w_in[l], conv_w[l], conv_b[l], dt_bias[l], a_log[l],
                     d_skip[l], ssm_norm_w[l], g_q[l], g_k[l], f_bias[l], w_out[l], g_xattn[l],
                     g_mem[l], xq_w[l], xkv_w[l], xg_q[l], xg_k[l], xo_w[l], g_mlp[l], w_up[l],
                     w_down[l])
    return x2d.reshape(BATCH, SEQ, D_MODEL)
```

```python
import functools

import numpy as np
import jax
import jax.numpy as jnp
from jax import lax
from jax.experimental import pallas as pl
from jax.experimental.pallas import tpu as pltpu

F32 = jnp.float32
BF16 = jnp.bfloat16

D_MODEL = 1024
BATCH = 8
SEQ = 2048
TOKENS = BATCH * SEQ
MEM_LEN = 256
SSM_HEAD_DIM = 64
SSM_HEADS = 16
SSM_D_INNER = 1024
SSM_GROUPS = 2
SSM_STATE = 128
CONV_WIDTH = 4
CHUNK = 128
ATTN_HEAD_DIM = 64
ATTN_HEADS = 16
ATTN_WIDTH = 1024
XATTN_HEADS = 4
XATTN_HEAD_DIM = 256
D_FF = 4096
EPS = 1e-5

LANES = 128
N_MAIN = 5632
HEAD_PAIRS = ATTN_HEADS // 2
GROUP_W = SSM_D_INNER // SSM_GROUPS
DT_LANE = 16
VMEM_LIMIT = 56 * 1024 * 1024
NEG_BIG = -1e30
LOG2E = 1.4426950408889634


def _rms(xf, g_row):
    ms = jnp.mean(xf * xf, axis=-1, keepdims=True)
    return xf * lax.rsqrt(ms + EPS) * g_row


def _split2(a):
    hi = a.astype(BF16)
    mid = (a - hi.astype(F32)).astype(BF16)
    return jnp.concatenate([hi, mid], axis=-1)


def _split3(a):
    hi = a.astype(BF16)
    r1 = a - hi.astype(F32)
    mid = r1.astype(BF16)
    lo = (r1 - mid.astype(F32)).astype(BF16)
    return jnp.concatenate([hi, mid, lo], axis=-1)


def _softplus(x):
    return jnp.maximum(x, 0.0) + jnp.log1p(jnp.exp(-jnp.abs(x)))


def _silu(x):
    half = 0.5 * x
    return half + half * jnp.tanh(half)


def _dot(a, b):
    return jnp.dot(a, b, preferred_element_type=F32)


def _dot_nt(a, b):
    return lax.dot_general(a, b, (((1,), (1,)), ((), ())), preferred_element_type=F32)


def _interleave(*gens):
    live = list(gens)
    while live:
        live = [g for g in live if next(g, StopIteration) is not StopIteration]


IN_TM = 512
IN_TN = 512


def _proj_layout():
    widths = dict(z=SSM_D_INNER, xs=SSM_D_INNER, bc=2 * SSM_GROUPS * SSM_STATE, dt=SSM_HEADS,
                  qkv=3 * ATTN_WIDTH, f=ATTN_HEADS)
    starts, col = {}, 0
    for name, width in widths.items():
        starts[name] = col
        col += width
    pieces, dst = [], 0
    for name in ("z", "qkv", "xs", "bc"):
        pieces.append((dst, starts[name], widths[name]))
        dst += widths[name]
    return pieces, starts, col


PROJ_PIECES, W_IN_STARTS, W_IN_COLS = _proj_layout()


def _inproj_kernel(x_ref, g_ref, w_ref, wg_ref, proj_ref, gate_ref):
    h = _rms(x_ref[...], g_ref[...]).astype(BF16)
    for dst, src, width in PROJ_PIECES:
        for n in range(0, width, IN_TN):
            proj_ref[:, dst + n:dst + n + IN_TN] = _dot(
                h, w_ref[:, src + n:src + n + IN_TN].astype(BF16)).astype(BF16)
    gate_ref[...] = _dot(h, wg_ref[...])


def _inproj(x2d, g_row, w_in, w_gate):
    return pl.pallas_call(
        _inproj_kernel,
        out_shape=(jax.ShapeDtypeStruct((TOKENS, N_MAIN), BF16),
                   jax.ShapeDtypeStruct((TOKENS, LANES), F32)),
        grid=(TOKENS // IN_TM,),
        in_specs=[
            pl.BlockSpec((IN_TM, D_MODEL), lambda i: (i, 0)),
            pl.BlockSpec((1, D_MODEL), lambda i: (0, 0)),
            pl.BlockSpec((D_MODEL, W_IN_COLS), lambda i: (0, 0), pipeline_mode=pl.Buffered(1)),
            pl.BlockSpec((D_MODEL, LANES), lambda i: (0, 0)),
        ],
        out_specs=(pl.BlockSpec((IN_TM, N_MAIN), lambda i: (i, 0)),
                   pl.BlockSpec((IN_TM, LANES), lambda i: (i, 0))),
        compiler_params=pltpu.CompilerParams(
            dimension_semantics=("parallel",), vmem_limit_bytes=VMEM_LIMIT),
        name="in_proj",
    )(x2d, g_row, w_in, w_gate)


AUX_PAIR_LANES = 16


GATE_ROWS = 16


def _gate_route_matrix():
    r = np.zeros((LANES, 3 * GATE_ROWS), np.float32)
    for h in range(ATTN_HEADS):
        for m in range(3):
            base = AUX_PAIR_LANES * (h // 2)
            r[base + 2 * m + (h % 2), GATE_ROWS * m + h] = 1.0
            r[base + 6 + 2 * m + (h % 2), GATE_ROWS * m + h] = -1.0
    return r


def _split3_rows(a):
    hi = a.astype(BF16)
    r1 = a - hi.astype(F32)
    mid = r1.astype(BF16)
    lo = (r1 - mid.astype(F32)).astype(BF16)
    return jnp.concatenate([hi, mid, lo], axis=0)


def _gates_kernel(g_ref, fb_ref, r_ref, aux_ref):
    row = lax.broadcasted_iota(jnp.int32, (CHUNK, CHUNK), 0)
    col = lax.broadcasted_iota(jnp.int32, (CHUNK, CHUNK), 1)
    triu = jnp.where(row <= col, 1.0, 0.0).astype(BF16)
    fb = fb_ref[...]
    rmat = r_ref[...]
    offset = jnp.zeros((GATE_ROWS, CHUNK), F32)
    for blk in range(SEQ // CHUNK):
        rows = slice(blk * CHUNK, (blk + 1) * CHUNK)
        f_t = g_ref[rows, :].T[:GATE_ROWS, :]
        log_f = -_softplus(-(f_t + fb))
        part = _dot(_split3_rows(log_f), triu)
        cum = (part[:GATE_ROWS] + part[GATE_ROWS:2 * GATE_ROWS] + part[2 * GATE_ROWS:]) + offset
        offset = jnp.broadcast_to(cum[:, CHUNK - 1:CHUNK], (GATE_ROWS, CHUNK))
        aux_t = _dot(rmat, _split3_rows(cum * LOG2E))
        aux_ref[rows, :] = aux_t.T.astype(BF16)


def _gates(gates_raw, fb_t, rmat):
    return pl.pallas_call(
        _gates_kernel,
        out_shape=jax.ShapeDtypeStruct((TOKENS, LANES), BF16),
        grid=(BATCH,),
        in_specs=[
            pl.BlockSpec((SEQ, LANES), lambda b: (b, 0)),
            pl.BlockSpec((GATE_ROWS, CHUNK), lambda b: (0, 0)),
            pl.BlockSpec((LANES, 3 * GATE_ROWS), lambda b: (0, 0)),
        ],
        out_specs=pl.BlockSpec((SEQ, LANES), lambda b: (b, 0)),
        compiler_params=pltpu.CompilerParams(dimension_semantics=("parallel",)),
        name="gates",
    )(gates_raw, fb_t, rmat)


def _head_expand_matrix():
    e = np.zeros((2 * LANES, SSM_D_INNER), np.float32)
    for h in range(SSM_HEADS):
        e[DT_LANE + h, h * SSM_HEAD_DIM:(h + 1) * SSM_HEAD_DIM] = 1.0
        e[LANES + DT_LANE + h, h * SSM_HEAD_DIM:(h + 1) * SSM_HEAD_DIM] = 1.0
    return e


CONV_TAIL = 16


def _conv_shift_matrix():
    s = np.zeros(((CONV_WIDTH - 1) * CHUNK, CHUNK + CONV_TAIL), np.float32)
    for k in range(CONV_WIDTH - 1):
        for t in range(CHUNK):
            src = t - 1 - k
            s[k * CHUNK + t, src if src >= 0 else CHUNK + CONV_TAIL + src] = 1.0
    return s


def _ssd_chunk(z_ref, xs_ref, bc_ref, g_ref, cwx_ref, cwb_ref, cbx_ref, cbb_ref,
               dtb_ref, alog_ref, ee_ref, shift_ref, dsk_ref, nw_ref, y_ref,
               xtail, btail, states):
    def conv_silu(u_ref, tail, w_ref, b_ref):
        cur = u_ref[...]
        shifted = _dot(shift_ref[...], jnp.concatenate([cur, tail], axis=0))
        acc = b_ref[...] + w_ref[3:4, :] * cur.astype(F32)
        for k in range(CONV_WIDTH - 1):
            acc = acc + w_ref[2 - k:3 - k, :] * shifted[k * CHUNK:(k + 1) * CHUNK, :]
        return _silu(acc), cur[CHUNK - CONV_TAIL:, :]

    xs, xtail = conv_silu(xs_ref, xtail, cwx_ref, cbx_ref)
    bc, btail = conv_silu(bc_ref, btail, cwb_ref, cbb_ref)

    dt = _softplus(g_ref[...] + dtb_ref[...])
    da = dt * (-jnp.exp(alog_ref[...]))
    row = lax.broadcasted_iota(jnp.int32, (CHUNK, CHUNK), 0)
    col = lax.broadcasted_iota(jnp.int32, (CHUNK, CHUNK), 1)
    causal = row >= col
    tril = jnp.where(causal, 1.0, 0.0).astype(BF16)
    part = _dot(tril, _split3(da))
    acs = part[:, :LANES] + part[:, LANES:2 * LANES] + part[:, 2 * LANES:]
    a_last = acs[CHUNK - 1:CHUNK, :]
    exp_a = jnp.exp(acs)
    dt_decay = dt * jnp.exp(a_last - acs)
    ee = ee_ref[...]
    e_dtdec = _dot(_split2(dt_decay), ee)
    e_expa = _dot(_split2(exp_a), ee)
    acs_t = acs.T
    dt_t = dt.T

    xs_b = xs.astype(BF16)
    xdec_b = (xs * e_dtdec).astype(BF16)
    lane = lax.broadcasted_iota(jnp.int32, (CHUNK, LANES), 1)
    first_head = lane < SSM_HEAD_DIM

    y_parts = []
    new_states = []
    for g in range(SSM_GROUPS):
        b_g = bc[:, g * SSM_STATE:(g + 1) * SSM_STATE]
        c_g = bc[:, (SSM_GROUPS + g) * SSM_STATE:(SSM_GROUPS + g + 1) * SSM_STATE]
        c_gb = c_g.astype(BF16)
        cb = _dot_nt(c_gb, b_g.astype(BF16))
        state = states[g]
        cols = slice(g * GROUP_W, (g + 1) * GROUP_W)
        y_off = _dot(c_gb, state.astype(BF16)) * e_expa[:, cols]
        st_new = _dot(b_g.T.astype(BF16), xdec_b[:, cols])
        new_states.append(state * e_expa[CHUNK - 1:CHUNK, cols] + st_new)
        diag = []
        for pair in range(GROUP_W // LANES):
            x_pair = xs_b[:, g * GROUP_W + pair * LANES:g * GROUP_W + (pair + 1) * LANES]
            res = []
            for j in range(2):
                h = g * (SSM_HEADS // SSM_GROUPS) + 2 * pair + j
                hl = DT_LANE + h
                a_col = jnp.broadcast_to(acs[:, hl:hl + 1], (CHUNK, CHUNK))
                seg = jnp.where(causal, a_col - acs_t[hl:hl + 1, :], -jnp.inf)
                m_h = cb * jnp.exp(seg) * dt_t[hl:hl + 1, :]
                res.append(_dot(m_h.astype(BF16), x_pair))
            diag.append(jnp.where(first_head, res[0], res[1]))
        y_parts.append(jnp.concatenate(diag, axis=-1) + y_off)
    y = jnp.concatenate(y_parts, axis=-1) + dsk_ref[...] * xs
    y = y * _silu(z_ref[...].astype(F32))
    normed = []
    for g in range(SSM_GROUPS):
        y_g = y[:, g * GROUP_W:(g + 1) * GROUP_W]
        normed.append(y_g * lax.rsqrt(jnp.mean(y_g * y_g, axis=-1, keepdims=True) + EPS))
    y_ref[...] = (jnp.concatenate(normed, axis=-1) * nw_ref[...]).astype(BF16)
    return xtail, btail, new_states


SSD_CHUNKS_PER_STEP = 8


def _ssd_kernel(z_ref, xs_ref, bc_ref, g_ref, *rest):
    *consts, y_ref, xtail_ref, btail_ref, state_ref = rest

    @pl.when(pl.program_id(1) == 0)
    def _():
        xtail_ref[...] = jnp.zeros_like(xtail_ref)
        btail_ref[...] = jnp.zeros_like(btail_ref)
        state_ref[...] = jnp.zeros_like(state_ref)

    xtail, btail = xtail_ref[...], btail_ref[...]
    states = [state_ref[g] for g in range(SSM_GROUPS)]
    for sc in range(SSD_CHUNKS_PER_STEP):
        rows = pl.ds(sc * CHUNK, CHUNK)
        xtail, btail, states = _ssd_chunk(
            z_ref.at[rows], xs_ref.at[rows], bc_ref.at[rows], g_ref.at[rows], *consts,
            y_ref.at[rows], xtail, btail, states)
    xtail_ref[...] = xtail
    btail_ref[...] = btail
    for g in range(SSM_GROUPS):
        state_ref[g] = states[g]


def _ssd(proj, gates_raw, cw_x, cw_bc, cb_x, cb_bc, dtb_row, alog_row, ee, shift, dsk_row, nw_row):
    step_rows = SSD_CHUNKS_PER_STEP * CHUNK
    nsteps = SEQ // step_rows
    rowblk = lambda b, c: b * nsteps + c
    full = lambda shape: pl.BlockSpec(shape, lambda b, c: (0, 0))
    return pl.pallas_call(
        _ssd_kernel,
        out_shape=jax.ShapeDtypeStruct((TOKENS, SSM_D_INNER), BF16),
        grid=(BATCH, nsteps),
        in_specs=[
            pl.BlockSpec((step_rows, SSM_D_INNER), lambda b, c: (rowblk(b, c), 0)),
            pl.BlockSpec((step_rows, SSM_D_INNER), lambda b, c: (rowblk(b, c), 4)),
            pl.BlockSpec((step_rows, GROUP_W), lambda b, c: (rowblk(b, c), 10)),
            pl.BlockSpec((step_rows, LANES), lambda b, c: (rowblk(b, c), 0)),
            full((CONV_WIDTH, SSM_D_INNER)), full((CONV_WIDTH, GROUP_W)),
            full((1, SSM_D_INNER)), full((1, GROUP_W)),
            full((1, LANES)), full((1, LANES)),
            full((2 * LANES, SSM_D_INNER)),
            full(((CONV_WIDTH - 1) * CHUNK, CHUNK + CONV_TAIL)),
            full((1, SSM_D_INNER)), full((1, SSM_D_INNER)),
        ],
        out_specs=pl.BlockSpec((step_rows, SSM_D_INNER), lambda b, c: (rowblk(b, c), 0)),
        scratch_shapes=[
            pltpu.VMEM((CONV_TAIL, SSM_D_INNER), BF16),
            pltpu.VMEM((CONV_TAIL, GROUP_W), BF16),
            pltpu.VMEM((SSM_GROUPS, SSM_STATE, GROUP_W), F32),
        ],
        compiler_params=pltpu.CompilerParams(
            dimension_semantics=("parallel", "arbitrary"),
            vmem_limit_bytes=VMEM_LIMIT),
        name="ssd",
    )(proj, proj, proj, gates_raw, cw_x, cw_bc, cb_x, cb_bc, dtb_row, alog_row, ee, shift, dsk_row,
      nw_row)


ATT_T = 256
ATT_K = 2 * LANES


ONES_ROWS = 16
VT_HEAD_ROWS = ATTN_HEAD_DIM + ONES_ROWS


def _attn_kernel(q_ref, k_ref, v_ref, aux_ref, gq_ref, gk_ref, o_ref, ka_ref, vt_ref, qt_ref, s_ref,
                 m_ref):
    nblk = SEQ // ATT_T
    pair = pl.program_id(1)
    lane = lax.broadcasted_iota(jnp.int32, (ATT_T, LANES), 1)
    sub = lane & (AUX_PAIR_LANES - 1)
    in_pair = (lane >> 4) == pair
    is_val = in_pair & (sub < 6)
    is_neg = in_pair & (sub >= 6) & (sub < 12)
    scale = ATTN_HEAD_DIM ** -0.5 * LOG2E
    krow = lax.broadcasted_iota(jnp.int32, (ATT_T, ATT_T), 0)
    qcol = lax.broadcasted_iota(jnp.int32, (ATT_T, ATT_T), 1)
    causal_t = qcol >= krow
    trow = lax.broadcasted_iota(jnp.int32, (LANES, ATT_T), 0)
    tsub = trow & (AUX_PAIR_LANES - 1)
    t_in_pair = (trow >> 4) == pair
    t_val = t_in_pair & (tsub < 6)
    t_neg = t_in_pair & (tsub >= 6) & (tsub < 12)

    def head_norm_t(u_t, g_t):
        sq = u_t * u_t
        halves = []
        for j in range(2):
            hrows = slice(j * ATTN_HEAD_DIM, (j + 1) * ATTN_HEAD_DIM)
            ms = jnp.sum(sq[hrows, :], axis=0, keepdims=True) * (1.0 / ATTN_HEAD_DIM)
            halves.append(u_t[hrows, :] * lax.rsqrt(ms + EPS))
        return jnp.concatenate(halves, axis=0) * g_t

    for j in range(2):
        vt_ref[j * VT_HEAD_ROWS + ATTN_HEAD_DIM:(j + 1) * VT_HEAD_ROWS, :] = jnp.ones(
            (ONES_ROWS, SEQ), BF16)

    def prepare(i):
        rows = slice(i * ATT_T, (i + 1) * ATT_T)
        qn_t = head_norm_t(q_ref[rows, :].astype(F32).T, gq_ref[...] * scale)
        kn = head_norm_t(k_ref[rows, :].astype(F32).T, gk_ref[...]).T
        aux = aux_ref[rows, :].astype(F32)
        aux_t = aux.T
        k_aux = jnp.where(is_neg, aux, jnp.where(is_val, 1.0, 0.0))
        ka_ref[rows, :] = jnp.concatenate([kn, k_aux], axis=-1).astype(BF16)
        v_t = v_ref[rows, :].astype(F32).T.astype(BF16)
        for j in range(2):
            vt_ref[j * VT_HEAD_ROWS:j * VT_HEAD_ROWS + ATTN_HEAD_DIM, rows] = (
                v_t[j * ATTN_HEAD_DIM:(j + 1) * ATTN_HEAD_DIM, :])
        for j in range(2):
            mine = (tsub & 1) == j
            q_aux_t = jnp.where(t_val & mine, aux_t, jnp.where(t_neg & mine, 1.0, 0.0))
            q_main_t = jnp.where((trow >> 6) == j, qn_t, 0.0)
            qt_ref[i, j] = jnp.concatenate([q_main_t, q_aux_t], axis=0).astype(BF16)
        yield

    def scores(i):
        for j in range(2):
            qa_t = qt_ref[i, j]
            tile_max = None
            for t in range(i + 1):
                s = _dot(ka_ref[t * ATT_T:(t + 1) * ATT_T, :], qa_t)
                if t == i:
                    s = jnp.where(causal_t, s, NEG_BIG)
                s_ref[i % 2, j, t] = s
                tile_max = s if tile_max is None else jnp.maximum(tile_max, s)
                yield
            m = jnp.max(tile_max, axis=0, keepdims=True)
            m_ref[i % 2, j] = jnp.broadcast_to(m, (8, ATT_T))

    def softmax_values(i):
        outs = []
        for j in range(2):
            m = m_ref[i % 2, j][0:1, :]
            acc = None
            for t in range(i + 1):
                pv = _dot(vt_ref[j * VT_HEAD_ROWS:(j + 1) * VT_HEAD_ROWS, t * ATT_T:(t + 1) * ATT_T],
                          jnp.exp2(s_ref[i % 2, j, t] - m).astype(BF16))
                acc = pv if acc is None else acc + pv
                yield
            outs.append(acc[:ATTN_HEAD_DIM, :] / acc[ATTN_HEAD_DIM:ATTN_HEAD_DIM + 1, :])
        o_t = jnp.concatenate(outs, axis=0)
        o_ref[i * ATT_T:(i + 1) * ATT_T, :] = o_t.T.astype(BF16)

    def chain(*gens):
        for g in gens:
            yield from g

    last = nblk - 1
    _interleave(chain(prepare(last), prepare(0)))
    _interleave(scores(last), chain(*[prepare(t) for t in range(1, last)]))
    for i in range(last, -1, -1):
        _interleave(softmax_values(i), *([scores(i - 1)] if i > 0 else []))


def _attention(proj, aux, gq_row, gk_row):
    qcol = ATTN_WIDTH // LANES
    return pl.pallas_call(
        _attn_kernel,
        out_shape=jax.ShapeDtypeStruct((TOKENS, ATTN_WIDTH), BF16),
        grid=(BATCH, HEAD_PAIRS),
        in_specs=[
            pl.BlockSpec((SEQ, LANES), lambda b, hp: (b, qcol + hp)),
            pl.BlockSpec((SEQ, LANES), lambda b, hp: (b, 2 * qcol + hp)),
            pl.BlockSpec((SEQ, LANES), lambda b, hp: (b, 3 * qcol + hp)),
            pl.BlockSpec((SEQ, LANES), lambda b, hp: (b, 0)),
            pl.BlockSpec((LANES, ATT_T), lambda b, hp: (0, 0)),
            pl.BlockSpec((LANES, ATT_T), lambda b, hp: (0, 0)),
        ],
        out_specs=pl.BlockSpec((SEQ, LANES), lambda b, hp: (b, hp)),
        scratch_shapes=[
            pltpu.VMEM((SEQ, ATT_K), BF16),
            pltpu.VMEM((2 * VT_HEAD_ROWS, SEQ), BF16),
            pltpu.VMEM((SEQ // ATT_T, 2, ATT_K, ATT_T), BF16),
            pltpu.VMEM((2, 2, SEQ // ATT_T, ATT_T, ATT_T), F32),
            pltpu.VMEM((2, 2, 8, ATT_T), F32),
        ],
        compiler_params=pltpu.CompilerParams(
            dimension_semantics=("parallel", "parallel"),
            vmem_limit_bytes=VMEM_LIMIT),
        name="fox_attention",
    )(proj, proj, proj, aux, gq_row, gk_row)


OUT_TN = 256


def _outproj_kernel(y_ref, o_ref, x_ref, w_ref, out_ref):
    y = y_ref[...]
    o = o_ref[...]
    for n in range(0, D_MODEL, OUT_TN):
        cols = slice(n, n + OUT_TN)
        out_ref[:, cols] = (x_ref[:, cols]
                            + _dot(y, w_ref[:SSM_D_INNER, cols].astype(BF16))
                            + _dot(o, w_ref[SSM_D_INNER:, cols].astype(BF16)))


def _outproj(y, o, x2d, w_out, tm=1024):
    return pl.pallas_call(
        _outproj_kernel,
        out_shape=jax.ShapeDtypeStruct((TOKENS, D_MODEL), F32),
        grid=(TOKENS // tm,),
        in_specs=[
            pl.BlockSpec((tm, SSM_D_INNER), lambda i: (i, 0)),
            pl.BlockSpec((tm, ATTN_WIDTH), lambda i: (i, 0)),
            pl.BlockSpec((tm, D_MODEL), lambda i: (i, 0)),
            pl.BlockSpec((SSM_D_INNER + ATTN_WIDTH, D_MODEL), lambda i: (0, 0),
                         pipeline_mode=pl.Buffered(1)),
        ],
        out_specs=pl.BlockSpec((tm, D_MODEL), lambda i: (i, 0)),
        compiler_params=pltpu.CompilerParams(
            dimension_semantics=("parallel",), vmem_limit_bytes=VMEM_LIMIT),
        name="out_proj",
    )(y, o, x2d, w_out)


def _memkv_kernel(m_ref, g_ref, w_ref, gk_ref, knt_ref, v_ref):
    h = _rms(m_ref[...], g_ref[...]).astype(BF16)
    for a in range(XATTN_HEADS):
        cols = slice(a * XATTN_HEAD_DIM, (a + 1) * XATTN_HEAD_DIM)
        kn = _rms(_dot(h, w_ref[:, cols].astype(BF16)), gk_ref[...])
        knt_ref[cols, :] = kn.T.astype(BF16)
    for n in range(0, D_MODEL, XATTN_HEAD_DIM):
        v_ref[:, n:n + XATTN_HEAD_DIM] = _dot(
            h, w_ref[:, D_MODEL + n:D_MODEL + n + XATTN_HEAD_DIM].astype(BF16)).astype(BF16)


def _memkv(mem2d, g_row, wkv, gk_row):
    return pl.pallas_call(
        _memkv_kernel,
        out_shape=(jax.ShapeDtypeStruct((BATCH * D_MODEL, MEM_LEN), BF16),
                   jax.ShapeDtypeStruct((BATCH * MEM_LEN, D_MODEL), BF16)),
        grid=(BATCH,),
        in_specs=[
            pl.BlockSpec((MEM_LEN, D_MODEL), lambda b: (b, 0)),
            pl.BlockSpec((1, D_MODEL), lambda b: (0, 0)),
            pl.BlockSpec((D_MODEL, 2 * D_MODEL), lambda b: (0, 0), pipeline_mode=pl.Buffered(1)),
            pl.BlockSpec((1, XATTN_HEAD_DIM), lambda b: (0, 0)),
        ],
        out_specs=(pl.BlockSpec((D_MODEL, MEM_LEN), lambda b: (b, 0)),
                   pl.BlockSpec((MEM_LEN, D_MODEL), lambda b: (b, 0))),
        compiler_params=pltpu.CompilerParams(
            dimension_semantics=("parallel",), vmem_limit_bytes=VMEM_LIMIT),
        name="mem_kv",
    )(mem2d, g_row, wkv, gk_row)


XATTN_SUB = 512


def _xattn_kernel(x_ref, g_ref, wq_ref, knt_ref, v_ref, gq_ref, wo_ref, out_ref, q_scr, o_scr):
    nsub = x_ref.shape[0] // XATTN_SUB
    scale = XATTN_HEAD_DIM ** -0.5 * LOG2E
    head_cols = [slice(a * XATTN_HEAD_DIM, (a + 1) * XATTN_HEAD_DIM) for a in range(XATTN_HEADS)]

    def project(k):
        rows = slice(k * XATTN_SUB, (k + 1) * XATTN_SUB)
        h = _rms(x_ref[rows, :], g_ref[...]).astype(BF16)
        for cols in head_cols:
            q_scr[k % 2, :, cols] = _dot(h, wq_ref[:, cols])
            yield

    def attend(k):
        for cols in head_cols:
            qn = (_rms(q_scr[k % 2, :, cols], gq_ref[...]) * scale).astype(BF16)
            s = _dot(qn, knt_ref[cols, :])
            e = jnp.exp2(s - jnp.max(s, axis=-1, keepdims=True))
            p = e / jnp.sum(e, axis=-1, keepdims=True)
            o_scr[k % 2, :, cols] = _dot(p.astype(BF16), v_ref[:, cols]).astype(BF16)
            yield

    def output(k):
        rows = slice(k * XATTN_SUB, (k + 1) * XATTN_SUB)
        o = o_scr[k % 2]
        for n in range(0, D_MODEL, OUT_TN):
            cols = slice(n, n + OUT_TN)
            out_ref[rows, cols] = x_ref[rows, cols] + _dot(o, wo_ref[:, cols])
            yield

    for step in range(nsub + 2):
        stage = []
        if 0 <= step - 2 < nsub:
            stage.append(output(step - 2))
        if 0 <= step - 1 < nsub:
            stage.append(attend(step - 1))
        if step < nsub:
            stage.append(project(step))
        _interleave(*stage)


def _xattn(x1, g_row, wq, knt, v, gq_row, wo, tm=SEQ):
    nt = SEQ // tm
    return pl.pallas_call(
        _xattn_kernel,
        out_shape=jax.ShapeDtypeStruct((TOKENS, D_MODEL), F32),
        grid=(BATCH, nt),
        in_specs=[
            pl.BlockSpec((tm, D_MODEL), lambda b, i: (b * nt + i, 0)),
            pl.BlockSpec((1, D_MODEL), lambda b, i: (0, 0)),
            pl.BlockSpec((D_MODEL, D_MODEL), lambda b, i: (0, 0)),
            pl.BlockSpec((D_MODEL, MEM_LEN), lambda b, i: (b, 0)),
            pl.BlockSpec((MEM_LEN, D_MODEL), lambda b, i: (b, 0)),
            pl.BlockSpec((1, XATTN_HEAD_DIM), lambda b, i: (0, 0)),
            pl.BlockSpec((D_MODEL, D_MODEL), lambda b, i: (0, 0)),
        ],
        out_specs=pl.BlockSpec((tm, D_MODEL), lambda b, i: (b * nt + i, 0)),
        scratch_shapes=[pltpu.VMEM((2, XATTN_SUB, D_MODEL), F32),
                        pltpu.VMEM((2, XATTN_SUB, D_MODEL), BF16)],
        compiler_params=pltpu.CompilerParams(
            dimension_semantics=("parallel", "parallel"), vmem_limit_bytes=VMEM_LIMIT),
        name="mem_xattn",
    )(x1, g_row, wq, knt, v, gq_row, wo)


FF_CHUNK = 1024


def _mlp_kernel(x_ref, g_ref, wu_ref, wd_ref, out_ref, h_scr, acc_scr):
    h_scr[...] = _rms(x_ref[...], g_ref[...]).astype(BF16)
    acc_scr[...] = x_ref[...]
    for f in range(0, D_FF, FF_CHUNK):
        u = jnp.maximum(_dot(h_scr[...], wu_ref[:, f:f + FF_CHUNK].astype(BF16)), 0.0)
        acc_scr[...] += _dot((u * u).astype(BF16), wd_ref[f:f + FF_CHUNK, :].astype(BF16))
    out_ref[...] = acc_scr[...]


def _mlp(x2, g_row, wu, wd, tm=512):
    return pl.pallas_call(
        _mlp_kernel,
        out_shape=jax.ShapeDtypeStruct((TOKENS, D_MODEL), F32),
        grid=(TOKENS // tm,),
        in_specs=[
            pl.BlockSpec((tm, D_MODEL), lambda i: (i, 0)),
            pl.BlockSpec((1, D_MODEL), lambda i: (0, 0)),
            pl.BlockSpec((D_MODEL, D_FF), lambda i: (0, 0), pipeline_mode=pl.Buffered(1)),
            pl.BlockSpec((D_FF, D_MODEL), lambda i: (0, 0), pipeline_mode=pl.Buffered(1)),
        ],
        out_specs=pl.BlockSpec((tm, D_MODEL), lambda i: (i, 0)),
        scratch_shapes=[pltpu.VMEM((tm, D_MODEL), BF16), pltpu.VMEM((tm, D_MODEL), F32)],
        compiler_params=pltpu.CompilerParams(
            dimension_semantics=("parallel",), vmem_limit_bytes=VMEM_LIMIT),
        name="relu2_mlp",
    )(x2, g_row, wu, wd)


def _lane_row(vec, offset):
    return jnp.zeros((1, LANES), F32).at[0, offset:offset + vec.shape[0]].set(vec.astype(F32))


def _layer(x2d, mem2d, g_mix, w_in, conv_w, conv_b, dt_bias, a_log, d_skip, ssm_norm_w,
           g_q, g_k, f_bias, w_out, g_xattn, g_mem, xq_w, xkv_w, xg_q, xg_k, xo_w,
           g_mlp, w_up, w_down):
    f0, dt0 = W_IN_STARTS["f"], W_IN_STARTS["dt"]
    w_gate = jnp.concatenate(
        [w_in[:, f0:f0 + ATTN_HEADS], w_in[:, dt0:dt0 + SSM_HEADS],
         jnp.zeros((D_MODEL, LANES - ATTN_HEADS - SSM_HEADS), F32)], axis=1).astype(BF16)
    row = lambda v: v.astype(F32).reshape(1, -1)

    proj, gates_raw = _inproj(x2d, row(g_mix), w_in, w_gate)

    fb_t = jnp.broadcast_to(f_bias.astype(F32)[:, None], (GATE_ROWS, CHUNK))
    aux = _gates(gates_raw, fb_t, jnp.asarray(_gate_route_matrix(), BF16))

    y = _ssd(proj, gates_raw,
             conv_w[:, :SSM_D_INNER], conv_w[:, SSM_D_INNER:],
             row(conv_b[:SSM_D_INNER]), row(conv_b[SSM_D_INNER:]),
             _lane_row(dt_bias, DT_LANE), _lane_row(a_log, DT_LANE),
             jnp.asarray(_head_expand_matrix(), BF16), jnp.asarray(_conv_shift_matrix(), BF16),
             row(jnp.repeat(d_skip, SSM_HEAD_DIM)), row(ssm_norm_w))

    gain_t = lambda g: jnp.broadcast_to(jnp.tile(g.astype(F32), 2)[:, None], (LANES, ATT_T))
    o = _attention(proj, aux, gain_t(g_q), gain_t(g_k))

    x1 = _outproj(y, o, x2d, w_out)

    knt, mem_v = _memkv(mem2d, row(g_mem), xkv_w, row(xg_k))
    x2 = _xattn(x1, row(g_xattn), xq_w.astype(BF16), knt, mem_v, row(xg_q), xo_w.astype(BF16))

    return _mlp(x2, row(g_mlp), w_up, w_down)


def kernel(x, mem, g_mix, w_in, conv_w, conv_b, dt_bias, a_log, d_skip, ssm_norm_w, g_q, g_k,
           f_bias, w_out, g_xattn, g_mem, xq_w, xkv_w, xg_q, xg_k, xo_w, g_mlp, w_up, w_down):
    x2d = x.reshape(TOKENS, D_MODEL)
    mem2d = mem.reshape(BATCH * MEM_LEN, D_MODEL)
    depth = g_mix.shape[0]
    for l in range(depth):
        x2d = _layer(x2d, mem2d, g_mix[l], w_in[l], conv_w[l], conv_b[l], dt_bias[l], a_log[l],
                     d_skip[l], ssm_norm_w[l], g_q[l], g_k[l], f_bias[l], w_out[l], g_xattn[l],
                     g_mem[l], xq_w[l], xkv_w[l], xg_q[l], xg_k[l], xo_w[l], g_mlp[l], w_up[l],
                     w_down[l])
    return x2d.reshape(BATCH, SEQ, D_MODEL)
```

```python
import numpy as np
import jax
import jax.numpy as jnp
from jax import lax
from jax.experimental import pallas as pl
from jax.experimental.pallas import tpu as pltpu

F32 = jnp.float32
BF16 = jnp.bfloat16

D_MODEL = 1024
BATCH = 8
SEQ = 2048
TOKENS = BATCH * SEQ
MEM_LEN = 256
SSM_HEAD_DIM = 64
SSM_HEADS = 16
SSM_D_INNER = 1024
SSM_GROUPS = 2
SSM_STATE = 128
CONV_WIDTH = 4
CHUNK = 128
ATTN_HEAD_DIM = 64
ATTN_HEADS = 16
ATTN_WIDTH = 1024
XATTN_HEADS = 4
XATTN_HEAD_DIM = 256
D_FF = 4096
EPS = 1e-5

LANES = 128
N_MAIN = 5632
HEAD_PAIRS = ATTN_HEADS // 2
GROUP_W = SSM_D_INNER // SSM_GROUPS
DT_LANE = 16
VMEM_LIMIT = 56 * 1024 * 1024
NEG_BIG = -1e30
LOG2E = 1.4426950408889634


def _rms(xf, g_row):
    ms = jnp.mean(xf * xf, axis=-1, keepdims=True)
    return xf * lax.rsqrt(ms + EPS) * g_row


def _split2(a):
    hi = a.astype(BF16)
    mid = (a - hi.astype(F32)).astype(BF16)
    return jnp.concatenate([hi, mid], axis=-1)


def _split3(a):
    hi = a.astype(BF16)
    r1 = a - hi.astype(F32)
    mid = r1.astype(BF16)
    lo = (r1 - mid.astype(F32)).astype(BF16)
    return jnp.concatenate([hi, mid, lo], axis=-1)


def _softplus(x):
    return jnp.maximum(x, 0.0) + jnp.log1p(jnp.exp(-jnp.abs(x)))


def _silu(x):
    half = 0.5 * x
    return half + half * jnp.tanh(half)


def _dot(a, b):
    return jnp.dot(a, b, preferred_element_type=F32)


def _dot_nt(a, b):
    return lax.dot_general(a, b, (((1,), (1,)), ((), ())), preferred_element_type=F32)


def _interleave(*gens):
    live = list(gens)
    while live:
        live = [g for g in live if next(g, StopIteration) is not StopIteration]


IN_TM = 1024
IN_TN = 512


def _inproj_kernel(x_ref, g_ref, w_ref, wg_ref, proj_ref, gate_ref):
    h = _rms(x_ref[...], g_ref[...]).astype(BF16)
    for n in range(0, N_MAIN, IN_TN):
        proj_ref[:, n:n + IN_TN] = _dot(h, w_ref[:, n:n + IN_TN]).astype(BF16)
    gate_ref[...] = _dot(h, wg_ref[...])


def _inproj(x2d, g_row, w_main, w_gate):
    return pl.pallas_call(
        _inproj_kernel,
        out_shape=(jax.ShapeDtypeStruct((TOKENS, N_MAIN), BF16),
                   jax.ShapeDtypeStruct((TOKENS, LANES), F32)),
        grid=(TOKENS // IN_TM,),
        in_specs=[
            pl.BlockSpec((IN_TM, D_MODEL), lambda i: (i, 0)),
            pl.BlockSpec((1, D_MODEL), lambda i: (0, 0)),
            pl.BlockSpec((D_MODEL, N_MAIN), lambda i: (0, 0), pipeline_mode=pl.Buffered(1)),
            pl.BlockSpec((D_MODEL, LANES), lambda i: (0, 0)),
        ],
        out_specs=(pl.BlockSpec((IN_TM, N_MAIN), lambda i: (i, 0)),
                   pl.BlockSpec((IN_TM, LANES), lambda i: (i, 0))),
        compiler_params=pltpu.CompilerParams(
            dimension_semantics=("parallel",), vmem_limit_bytes=VMEM_LIMIT),
        name="in_proj",
    )(x2d, g_row, w_main, w_gate)


AUX_PAIR_LANES = 16
GATE_ROWS = 16


def _gate_route_matrix():
    r = np.zeros((LANES, 3 * GATE_ROWS), np.float32)
    for h in range(ATTN_HEADS):
        for m in range(3):
            base = AUX_PAIR_LANES * (h // 2)
            r[base + 2 * m + (h % 2), GATE_ROWS * m + h] = 1.0
            r[base + 6 + 2 * m + (h % 2), GATE_ROWS * m + h] = -1.0
    return r


def _split3_rows(a):
    hi = a.astype(BF16)
    r1 = a - hi.astype(F32)
    mid = r1.astype(BF16)
    lo = (r1 - mid.astype(F32)).astype(BF16)
    return jnp.concatenate([hi, mid, lo], axis=0)


def _gates_kernel(g_ref, fb_ref, r_ref, aux_ref):
    row = lax.broadcasted_iota(jnp.int32, (CHUNK, CHUNK), 0)
    col = lax.broadcasted_iota(jnp.int32, (CHUNK, CHUNK), 1)
    triu = jnp.where(row <= col, 1.0, 0.0).astype(BF16)
    fb = fb_ref[...]
    rmat = r_ref[...]
    offset = jnp.zeros((GATE_ROWS, CHUNK), F32)
    for blk in range(SEQ // CHUNK):
        rows = slice(blk * CHUNK, (blk + 1) * CHUNK)
        f_t = g_ref[rows, :].T[:GATE_ROWS, :]
        log_f = -_softplus(-(f_t + fb))
        part = _dot(_split3_rows(log_f), triu)
        cum = (part[:GATE_ROWS] + part[GATE_ROWS:2 * GATE_ROWS] + part[2 * GATE_ROWS:]) + offset
        offset = jnp.broadcast_to(cum[:, CHUNK - 1:CHUNK], (GATE_ROWS, CHUNK))
        aux_t = _dot(rmat, _split3_rows(cum * LOG2E))
        aux_ref[rows, :] = aux_t.T.astype(BF16)


def _gates(gates_raw, fb_t, rmat):
    return pl.pallas_call(
        _gates_kernel,
        out_shape=jax.ShapeDtypeStruct((TOKENS, LANES), BF16),
        grid=(BATCH,),
        in_specs=[
            pl.BlockSpec((SEQ, LANES), lambda b: (b, 0)),
            pl.BlockSpec((GATE_ROWS, CHUNK), lambda b: (0, 0)),
            pl.BlockSpec((LANES, 3 * GATE_ROWS), lambda b: (0, 0)),
        ],
        out_specs=pl.BlockSpec((SEQ, LANES), lambda b: (b, 0)),
        compiler_params=pltpu.CompilerParams(dimension_semantics=("parallel",)),
        name="gates",
    )(gates_raw, fb_t, rmat)


def _head_expand_matrix():
    e = np.zeros((2 * LANES, SSM_D_INNER), np.float32)
    for h in range(SSM_HEADS):
        e[DT_LANE + h, h * SSM_HEAD_DIM:(h + 1) * SSM_HEAD_DIM] = 1.0
        e[LANES + DT_LANE + h, h * SSM_HEAD_DIM:(h + 1) * SSM_HEAD_DIM] = 1.0
    return e


CONV_TAIL = 16


def _conv_shift_matrix():
    s = np.zeros(((CONV_WIDTH - 1) * CHUNK, CHUNK + CONV_TAIL), np.float32)
    for k in range(CONV_WIDTH - 1):
        for t in range(CHUNK):
            src = t - 1 - k
            s[k * CHUNK + t, src if src >= 0 else CHUNK + CONV_TAIL + src] = 1.0
    return s


def _ssd_chunk(z_ref, xs_ref, bc_ref, g_ref, cwx_ref, cwb_ref, cbx_ref, cbb_ref,
               dtb_ref, alog_ref, ee_ref, shift_ref, dsk_ref, nw_ref, y_ref,
               xtail, btail, states):
    def conv_silu(u_ref, tail, w_ref, b_ref):
        cur = u_ref[...]
        shifted = _dot(shift_ref[...], jnp.concatenate([cur, tail], axis=0))
        acc = b_ref[...] + w_ref[3:4, :] * cur.astype(F32)
        for k in range(CONV_WIDTH - 1):
            acc = acc + w_ref[2 - k:3 - k, :] * shifted[k * CHUNK:(k + 1) * CHUNK, :]
        return _silu(acc), cur[CHUNK - CONV_TAIL:, :]

    xs, xtail = conv_silu(xs_ref, xtail, cwx_ref, cbx_ref)
    bc, btail = conv_silu(bc_ref, btail, cwb_ref, cbb_ref)

    dt = _softplus(g_ref[...] + dtb_ref[...])
    da = dt * (-jnp.exp(alog_ref[...]))
    row = lax.broadcasted_iota(jnp.int32, (CHUNK, CHUNK), 0)
    col = lax.broadcasted_iota(jnp.int32, (CHUNK, CHUNK), 1)
    causal = row >= col
    tril = jnp.where(causal, 1.0, 0.0).astype(BF16)
    part = _dot(tril, _split3(da))
    acs = part[:, :LANES] + part[:, LANES:2 * LANES] + part[:, 2 * LANES:]
    a_last = acs[CHUNK - 1:CHUNK, :]
    exp_a = jnp.exp(acs)
    dt_decay = dt * jnp.exp(a_last - acs)
    ee = ee_ref[...]
    e_dtdec = _dot(_split2(dt_decay), ee)
    e_expa = _dot(_split2(exp_a), ee)
    acs_t = acs.T
    dt_t = dt.T

    xs_b = xs.astype(BF16)
    xdec_b = (xs * e_dtdec).astype(BF16)
    lane = lax.broadcasted_iota(jnp.int32, (CHUNK, LANES), 1)
    first_head = lane < SSM_HEAD_DIM

    y_parts = []
    new_states = []
    for g in range(SSM_GROUPS):
        b_g = bc[:, g * SSM_STATE:(g + 1) * SSM_STATE]
        c_g = bc[:, (SSM_GROUPS + g) * SSM_STATE:(SSM_GROUPS + g + 1) * SSM_STATE]
        c_gb = c_g.astype(BF16)
        cb = _dot_nt(c_gb, b_g.astype(BF16))
        state = states[g]
        cols = slice(g * GROUP_W, (g + 1) * GROUP_W)
        y_off = _dot(c_gb, state.astype(BF16)) * e_expa[:, cols]
        st_new = _dot(b_g.T.astype(BF16), xdec_b[:, cols])
        new_states.append(state * e_expa[CHUNK - 1:CHUNK, cols] + st_new)
        diag = []
        for pair in range(GROUP_W // LANES):
            x_pair = xs_b[:, g * GROUP_W + pair * LANES:g * GROUP_W + (pair + 1) * LANES]
            res = []
            for j in range(2):
                h = g * (SSM_HEADS // SSM_GROUPS) + 2 * pair + j
                hl = DT_LANE + h
                a_col = jnp.broadcast_to(acs[:, hl:hl + 1], (CHUNK, CHUNK))
                seg = jnp.where(causal, a_col - acs_t[hl:hl + 1, :], -jnp.inf)
                m_h = cb * jnp.exp(seg) * dt_t[hl:hl + 1, :]
                res.append(_dot(m_h.astype(BF16), x_pair))
            diag.append(jnp.where(first_head, res[0], res[1]))
        y_parts.append(jnp.concatenate(diag, axis=-1) + y_off)
    y = jnp.concatenate(y_parts, axis=-1) + dsk_ref[...] * xs
    y = y * _silu(z_ref[...].astype(F32))
    normed = []
    for g in range(SSM_GROUPS):
        y_g = y[:, g * GROUP_W:(g + 1) * GROUP_W]
        normed.append(y_g * lax.rsqrt(jnp.mean(y_g * y_g, axis=-1, keepdims=True) + EPS))
    y_ref[...] = (jnp.concatenate(normed, axis=-1) * nw_ref[...]).astype(BF16)
    return xtail, btail, new_states


SSD_CHUNKS_PER_STEP = 8


def _ssd_kernel(z_ref, xs_ref, bc_ref, g_ref, *rest):
    *consts, y_ref, xtail_ref, btail_ref, state_ref = rest

    @pl.when(pl.program_id(1) == 0)
    def _():
        xtail_ref[...] = jnp.zeros_like(xtail_ref)
        btail_ref[...] = jnp.zeros_like(btail_ref)
        state_ref[...] = jnp.zeros_like(state_ref)

    xtail, btail = xtail_ref[...], btail_ref[...]
    states = [state_ref[g] for g in range(SSM_GROUPS)]
    for sc in range(SSD_CHUNKS_PER_STEP):
        rows = pl.ds(sc * CHUNK, CHUNK)
        xtail, btail, states = _ssd_chunk(
            z_ref.at[rows], xs_ref.at[rows], bc_ref.at[rows], g_ref.at[rows], *consts,
            y_ref.at[rows], xtail, btail, states)
    xtail_ref[...] = xtail
    btail_ref[...] = btail
    for g in range(SSM_GROUPS):
        state_ref[g] = states[g]


def _ssd(proj, gates_raw, cw_x, cw_bc, cb_x, cb_bc, dtb_row, alog_row, ee, shift, dsk_row, nw_row):
    step_rows = SSD_CHUNKS_PER_STEP * CHUNK
    nsteps = SEQ // step_rows
    rowblk = lambda b, c: b * nsteps + c
    full = lambda shape: pl.BlockSpec(shape, lambda b, c: (0, 0))
    return pl.pallas_call(
        _ssd_kernel,
        out_shape=jax.ShapeDtypeStruct((TOKENS, SSM_D_INNER), BF16),
        grid=(BATCH, nsteps),
        in_specs=[
            pl.BlockSpec((step_rows, SSM_D_INNER), lambda b, c: (rowblk(b, c), 0)),
            pl.BlockSpec((step_rows, SSM_D_INNER), lambda b, c: (rowblk(b, c), 4)),
            pl.BlockSpec((step_rows, GROUP_W), lambda b, c: (rowblk(b, c), 10)),
            pl.BlockSpec((step_rows, LANES), lambda b, c: (rowblk(b, c), 0)),
            full((CONV_WIDTH, SSM_D_INNER)), full((CONV_WIDTH, GROUP_W)),
            full((1, SSM_D_INNER)), full((1, GROUP_W)),
            full((1, LANES)), full((1, LANES)),
            full((2 * LANES, SSM_D_INNER)),
            full(((CONV_WIDTH - 1) * CHUNK, CHUNK + CONV_TAIL)),
            full((1, SSM_D_INNER)), full((1, SSM_D_INNER)),
        ],
        out_specs=pl.BlockSpec((step_rows, SSM_D_INNER), lambda b, c: (rowblk(b, c), 0)),
        scratch_shapes=[
            pltpu.VMEM((CONV_TAIL, SSM_D_INNER), BF16),
            pltpu.VMEM((CONV_TAIL, GROUP_W), BF16),
            pltpu.VMEM((SSM_GROUPS, SSM_STATE, GROUP_W), F32),
        ],
        compiler_params=pltpu.CompilerParams(
            dimension_semantics=("parallel", "arbitrary"),
            vmem_limit_bytes=VMEM_LIMIT),
        name="ssd",
    )(proj, proj, proj, gates_raw, cw_x, cw_bc, cb_x, cb_bc, dtb_row, alog_row, ee, shift, dsk_row,
      nw_row)


ATT_T = 256
ATT_K = 2 * LANES
ONES_ROWS = 16
VT_HEAD_ROWS = ATTN_HEAD_DIM + ONES_ROWS


def _attn_kernel(q_ref, k_ref, v_ref, aux_ref, gq_ref, gk_ref, o_ref, ka_ref, vt_ref, qt_ref, s_ref,
                 m_ref):
    nblk = SEQ // ATT_T
    pair = pl.program_id(1)
    lane = lax.broadcasted_iota(jnp.int32, (ATT_T, LANES), 1)
    sub = lane & (AUX_PAIR_LANES - 1)
    in_pair = (lane >> 4) == pair
    is_val = in_pair & (sub < 6)
    is_neg = in_pair & (sub >= 6) & (sub < 12)
    scale = ATTN_HEAD_DIM ** -0.5 * LOG2E
    krow = lax.broadcasted_iota(jnp.int32, (ATT_T, ATT_T), 0)
    qcol = lax.broadcasted_iota(jnp.int32, (ATT_T, ATT_T), 1)
    causal_t = qcol >= krow
    trow = lax.broadcasted_iota(jnp.int32, (LANES, ATT_T), 0)
    tsub = trow & (AUX_PAIR_LANES - 1)
    t_in_pair = (trow >> 4) == pair
    t_val = t_in_pair & (tsub < 6)
    t_neg = t_in_pair & (tsub >= 6) & (tsub < 12)

    def head_norm_t(u_t, g_t):
        sq = u_t * u_t
        halves = []
        for j in range(2):
            hrows = slice(j * ATTN_HEAD_DIM, (j + 1) * ATTN_HEAD_DIM)
            ms = jnp.sum(sq[hrows, :], axis=0, keepdims=True) * (1.0 / ATTN_HEAD_DIM)
            halves.append(u_t[hrows, :] * lax.rsqrt(ms + EPS))
        return jnp.concatenate(halves, axis=0) * g_t

    for j in range(2):
        vt_ref[j * VT_HEAD_ROWS + ATTN_HEAD_DIM:(j + 1) * VT_HEAD_ROWS, :] = jnp.ones(
            (ONES_ROWS, SEQ), BF16)

    def prepare(i):
        rows = slice(i * ATT_T, (i + 1) * ATT_T)
        qn_t = head_norm_t(q_ref[rows, :].astype(F32).T, gq_ref[...] * scale)
        kn = head_norm_t(k_ref[rows, :].astype(F32).T, gk_ref[...]).T
        aux = aux_ref[rows, :].astype(F32)
        aux_t = aux.T
        k_aux = jnp.where(is_neg, aux, jnp.where(is_val, 1.0, 0.0))
        ka_ref[rows, :] = jnp.concatenate([kn, k_aux], axis=-1).astype(BF16)
        v_t = v_ref[rows, :].astype(F32).T.astype(BF16)
        for j in range(2):
            vt_ref[j * VT_HEAD_ROWS:j * VT_HEAD_ROWS + ATTN_HEAD_DIM, rows] = (
                v_t[j * ATTN_HEAD_DIM:(j + 1) * ATTN_HEAD_DIM, :])
        for j in range(2):
            mine = (tsub & 1) == j
            q_aux_t = jnp.where(t_val & mine, aux_t, jnp.where(t_neg & mine, 1.0, 0.0))
            q_main_t = jnp.where((trow >> 6) == j, qn_t, 0.0)
            qt_ref[i, j] = jnp.concatenate([q_main_t, q_aux_t], axis=0).astype(BF16)
        yield

    def scores(i):
        for j in range(2):
            qa_t = qt_ref[i, j]
            tile_max = None
            for t in range(i + 1):
                s = _dot(ka_ref[t * ATT_T:(t + 1) * ATT_T, :], qa_t)
                if t == i:
                    s = jnp.where(causal_t, s, NEG_BIG)
                s_ref[i % 2, j, t] = s
                tile_max = s if tile_max is None else jnp.maximum(tile_max, s)
                yield
            m = jnp.max(tile_max, axis=0, keepdims=True)
            m_ref[i % 2, j] = jnp.broadcast_to(m, (8, ATT_T))

    def softmax_values(i):
        outs = []
        for j in range(2):
            m = m_ref[i % 2, j][0:1, :]
            acc = None
            for t in range(i + 1):
                pv = _dot(vt_ref[j * VT_HEAD_ROWS:(j + 1) * VT_HEAD_ROWS, t * ATT_T:(t + 1) * ATT_T],
                          jnp.exp2(s_ref[i % 2, j, t] - m).astype(BF16))
                acc = pv if acc is None else acc + pv
                yield
            outs.append(acc[:ATTN_HEAD_DIM, :] / acc[ATTN_HEAD_DIM:ATTN_HEAD_DIM + 1, :])
        o_t = jnp.concatenate(outs, axis=0)
        o_ref[i * ATT_T:(i + 1) * ATT_T, :] = o_t.T.astype(BF16)

    def chain(*gens):
        for g in gens:
            yield from g

    last = nblk - 1
    _interleave(chain(prepare(last), prepare(0)))
    _interleave(scores(last), chain(*[prepare(t) for t in range(1, last)]))
    for i in range(last, -1, -1):
        _interleave(softmax_values(i), *([scores(i - 1)] if i > 0 else []))


def _attention(proj, aux, gq_row, gk_row):
    qcol = ATTN_WIDTH // LANES
    return pl.pallas_call(
        _attn_kernel,
        out_shape=jax.ShapeDtypeStruct((TOKENS, ATTN_WIDTH), BF16),
        grid=(BATCH, HEAD_PAIRS),
        in_specs=[
            pl.BlockSpec((SEQ, LANES), lambda b, hp: (b, qcol + hp)),
            pl.BlockSpec((SEQ, LANES), lambda b, hp: (b, 2 * qcol + hp)),
            pl.BlockSpec((SEQ, LANES), lambda b, hp: (b, 3 * qcol + hp)),
            pl.BlockSpec((SEQ, LANES), lambda b, hp: (b, 0)),
            pl.BlockSpec((LANES, ATT_T), lambda b, hp: (0, 0)),
            pl.BlockSpec((LANES, ATT_T), lambda b, hp: (0, 0)),
        ],
        out_specs=pl.BlockSpec((SEQ, LANES), lambda b, hp: (b, hp)),
        scratch_shapes=[
            pltpu.VMEM((SEQ, ATT_K), BF16),
            pltpu.VMEM((2 * VT_HEAD_ROWS, SEQ), BF16),
            pltpu.VMEM((SEQ // ATT_T, 2, ATT_K, ATT_T), BF16),
            pltpu.VMEM((2, 2, SEQ // ATT_T, ATT_T, ATT_T), F32),
            pltpu.VMEM((2, 2, 8, ATT_T), F32),
        ],
        compiler_params=pltpu.CompilerParams(
            dimension_semantics=("parallel", "parallel"),
            vmem_limit_bytes=VMEM_LIMIT),
        name="fox_attention",
    )(proj, proj, proj, aux, gq_row, gk_row)


OUT_TN = 256


def _outproj_kernel(y_ref, o_ref, x_ref, w_ref, out_ref):
    y = y_ref[...]
    o = o_ref[...]
    for n in range(0, D_MODEL, OUT_TN):
        cols = slice(n, n + OUT_TN)
        out_ref[:, cols] = (x_ref[:, cols]
                            + _dot(y, w_ref[:SSM_D_INNER, cols].astype(BF16))
                            + _dot(o, w_ref[SSM_D_INNER:, cols].astype(BF16)))


def _outproj(y, o, x2d, w_out, tm=1024):
    return pl.pallas_call(
        _outproj_kernel,
        out_shape=jax.ShapeDtypeStruct((TOKENS, D_MODEL), F32),
        grid=(TOKENS // tm,),
        in_specs=[
            pl.BlockSpec((tm, SSM_D_INNER), lambda i: (i, 0)),
            pl.BlockSpec((tm, ATTN_WIDTH), lambda i: (i, 0)),
            pl.BlockSpec((tm, D_MODEL), lambda i: (i, 0)),
            pl.BlockSpec((SSM_D_INNER + ATTN_WIDTH, D_MODEL), lambda i: (0, 0),
                         pipeline_mode=pl.Buffered(1)),
        ],
        out_specs=pl.BlockSpec((tm, D_MODEL), lambda i: (i, 0)),
        compiler_params=pltpu.CompilerParams(
            dimension_semantics=("parallel",), vmem_limit_bytes=VMEM_LIMIT),
        name="out_proj",
    )(y, o, x2d, w_out)


def _memkv_kernel(m_ref, g_ref, w_ref, gk_ref, knt_ref, v_ref):
    h = _rms(m_ref[...], g_ref[...]).astype(BF16)
    for a in range(XATTN_HEADS):
        cols = slice(a * XATTN_HEAD_DIM, (a + 1) * XATTN_HEAD_DIM)
        kn = _rms(_dot(h, w_ref[:, cols].astype(BF16)), gk_ref[...])
        knt_ref[cols, :] = kn.T.astype(BF16)
    for n in range(0, D_MODEL, XATTN_HEAD_DIM):
        v_ref[:, n:n + XATTN_HEAD_DIM] = _dot(
            h, w_ref[:, D_MODEL + n:D_MODEL + n + XATTN_HEAD_DIM].astype(BF16)).astype(BF16)


def _memkv(mem2d, g_row, wkv, gk_row):
    return pl.pallas_call(
        _memkv_kernel,
        out_shape=(jax.ShapeDtypeStruct((BATCH * D_MODEL, MEM_LEN), BF16),
                   jax.ShapeDtypeStruct((BATCH * MEM_LEN, D_MODEL), BF16)),
        grid=(BATCH,),
        in_specs=[
            pl.BlockSpec((MEM_LEN, D_MODEL), lambda b: (b, 0)),
            pl.BlockSpec((1, D_MODEL), lambda b: (0, 0)),
            pl.BlockSpec((D_MODEL, 2 * D_MODEL), lambda b: (0, 0), pipeline_mode=pl.Buffered(1)),
            pl.BlockSpec((1, XATTN_HEAD_DIM), lambda b: (0, 0)),
        ],
        out_specs=(pl.BlockSpec((D_MODEL, MEM_LEN), lambda b: (b, 0)),
                   pl.BlockSpec((MEM_LEN, D_MODEL), lambda b: (b, 0))),
        compiler_params=pltpu.CompilerParams(
            dimension_semantics=("parallel",), vmem_limit_bytes=VMEM_LIMIT),
        name="mem_kv",
    )(mem2d, g_row, wkv, gk_row)


XATTN_SUB = 512


def _xattn_kernel(x_ref, g_ref, wq_ref, knt_ref, v_ref, gq_ref, wo_ref, out_ref, q_scr, o_scr):
    nsub = x_ref.shape[0] // XATTN_SUB
    scale = XATTN_HEAD_DIM ** -0.5 * LOG2E
    head_cols = [slice(a * XATTN_HEAD_DIM, (a + 1) * XATTN_HEAD_DIM) for a in range(XATTN_HEADS)]

    def project(k):
        rows = slice(k * XATTN_SUB, (k + 1) * XATTN_SUB)
        h = _rms(x_ref[rows, :], g_ref[...]).astype(BF16)
        for cols in head_cols:
            q_scr[k % 2, :, cols] = _dot(h, wq_ref[:, cols])
            yield

    def attend(k):
        for cols in head_cols:
            qn = (_rms(q_scr[k % 2, :, cols], gq_ref[...]) * scale).astype(BF16)
            s = _dot(qn, knt_ref[cols, :])
            e = jnp.exp2(s - jnp.max(s, axis=-1, keepdims=True))
            p = e / jnp.sum(e, axis=-1, keepdims=True)
            o_scr[k % 2, :, cols] = _dot(p.astype(BF16), v_ref[:, cols]).astype(BF16)
            yield

    def output(k):
        rows = slice(k * XATTN_SUB, (k + 1) * XATTN_SUB)
        o = o_scr[k % 2]
        for n in range(0, D_MODEL, OUT_TN):
            cols = slice(n, n + OUT_TN)
            out_ref[rows, cols] = x_ref[rows, cols] + _dot(o, wo_ref[:, cols])
            yield

    for step in range(nsub + 2):
        stage = []
        if 0 <= step - 2 < nsub:
            stage.append(output(step - 2))
        if 0 <= step - 1 < nsub:
            stage.append(attend(step - 1))
        if step < nsub:
            stage.append(project(step))
        _interleave(*stage)


def _xattn(x1, g_row, wq, knt, v, gq_row, wo, tm=SEQ):
    nt = SEQ // tm
    return pl.pallas_call(
        _xattn_kernel,
        out_shape=jax.ShapeDtypeStruct((TOKENS, D_MODEL), F32),
        grid=(BATCH, nt),
        in_specs=[
            pl.BlockSpec((tm, D_MODEL), lambda b, i: (b * nt + i, 0)),
            pl.BlockSpec((1, D_MODEL), lambda b, i: (0, 0)),
            pl.BlockSpec((D_MODEL, D_MODEL), lambda b, i: (0, 0)),
            pl.BlockSpec((D_MODEL, MEM_LEN), lambda b, i: (b, 0)),
            pl.BlockSpec((MEM_LEN, D_MODEL), lambda b, i: (b, 0)),
            pl.BlockSpec((1, XATTN_HEAD_DIM), lambda b, i: (0, 0)),
            pl.BlockSpec((D_MODEL, D_MODEL), lambda b, i: (0, 0)),
        ],
        out_specs=pl.BlockSpec((tm, D_MODEL), lambda b, i: (b * nt + i, 0)),
        scratch_shapes=[pltpu.VMEM((2, XATTN_SUB, D_MODEL), F32),
                        pltpu.VMEM((2, XATTN_SUB, D_MODEL), BF16)],
        compiler_params=pltpu.CompilerParams(
            dimension_semantics=("parallel", "parallel"), vmem_limit_bytes=VMEM_LIMIT),
        name="mem_xattn",
    )(x1, g_row, wq, knt, v, gq_row, wo)


FF_CHUNK = 1024


def _mlp_kernel(x_ref, g_ref, wu_ref, wd_ref, out_ref, h_scr, acc_scr):
    h_scr[...] = _rms(x_ref[...], g_ref[...]).astype(BF16)
    acc_scr[...] = x_ref[...]
    for f in range(0, D_FF, FF_CHUNK):
        u = jnp.maximum(_dot(h_scr[...], wu_ref[:, f:f + FF_CHUNK].astype(BF16)), 0.0)
        acc_scr[...] += _dot((u * u).astype(BF16), wd_ref[f:f + FF_CHUNK, :].astype(BF16))
    out_ref[...] = acc_scr[...]


def _mlp(x2, g_row, wu, wd, tm=512):
    return pl.pallas_call(
        _mlp_kernel,
        out_shape=jax.ShapeDtypeStruct((TOKENS, D_MODEL), F32),
        grid=(TOKENS // tm,),
        in_specs=[
            pl.BlockSpec((tm, D_MODEL), lambda i: (i, 0)),
            pl.BlockSpec((1, D_MODEL), lambda i: (0, 0)),
            pl.BlockSpec((D_MODEL, D_FF), lambda i: (0, 0), pipeline_mode=pl.Buffered(1)),
            pl.BlockSpec((D_FF, D_MODEL), lambda i: (0, 0), pipeline_mode=pl.Buffered(1)),
        ],
        out_specs=pl.BlockSpec((tm, D_MODEL), lambda i: (i, 0)),
        scratch_shapes=[pltpu.VMEM((tm, D_MODEL), BF16), pltpu.VMEM((tm, D_MODEL), F32)],
        compiler_params=pltpu.CompilerParams(
            dimension_semantics=("parallel",), vmem_limit_bytes=VMEM_LIMIT),
        name="relu2_mlp",
    )(x2, g_row, wu, wd)


def _lane_row(vec, offset):
    return jnp.zeros((1, LANES), F32).at[0, offset:offset + vec.shape[0]].set(vec.astype(F32))


def _layer(x2d, mem2d, g_mix, w_in, conv_w, conv_b, dt_bias, a_log, d_skip, ssm_norm_w,
           g_q, g_k, f_bias, w_out, g_xattn, g_mem, xq_w, xkv_w, xg_q, xg_k, xo_w,
           g_mlp, w_up, w_down):
    z0, xbc0 = 0, SSM_D_INNER
    bc0 = xbc0 + SSM_D_INNER
    dt0 = bc0 + 2 * SSM_GROUPS * SSM_STATE
    q0 = dt0 + SSM_HEADS
    k0 = q0 + ATTN_WIDTH
    v0 = k0 + ATTN_WIDTH
    f0 = v0 + ATTN_WIDTH
    w_main = jnp.concatenate(
        [w_in[:, z0:xbc0], w_in[:, q0:f0], w_in[:, xbc0:dt0]], axis=1).astype(BF16)
    w_gate = jnp.concatenate(
        [w_in[:, f0:f0 + ATTN_HEADS], w_in[:, dt0:q0],
         jnp.zeros((D_MODEL, LANES - ATTN_HEADS - SSM_HEADS), F32)], axis=1).astype(BF16)
    row = lambda v: v.astype(F32).reshape(1, -1)

    proj, gates_raw = _inproj(x2d, row(g_mix), w_main, w_gate)

    fb_t = jnp.broadcast_to(f_bias.astype(F32)[:, None], (GATE_ROWS, CHUNK))
    aux = _gates(gates_raw, fb_t, jnp.asarray(_gate_route_matrix(), BF16))

    y = _ssd(proj, gates_raw,
             conv_w[:, :SSM_D_INNER], conv_w[:, SSM_D_INNER:],
             row(conv_b[:SSM_D_INNER]), row(conv_b[SSM_D_INNER:]),
             _lane_row(dt_bias, DT_LANE), _lane_row(a_log, DT_LANE),
             jnp.asarray(_head_expand_matrix(), BF16), jnp.asarray(_conv_shift_matrix(), BF16),
             row(jnp.repeat(d_skip, SSM_HEAD_DIM)), row(ssm_norm_w))

    gain_t = lambda g: jnp.broadcast_to(jnp.tile(g.astype(F32), 2)[:, None], (LANES, ATT_T))
    o = _attention(proj, aux, gain_t(g_q), gain_t(g_k))

    x1 = _outproj(y, o, x2d, w_out)

    knt, mem_v = _memkv(mem2d, row(g_mem), xkv_w, row(xg_k))
    x2 = _xattn(x1, row(g_xattn), xq_w.astype(BF16), knt, mem_v, row(xg_q), xo_w.astype(BF16))

    return _mlp(x2, row(g_mlp), w_up, w_down)


def kernel(x, mem, g_mix, w_in, conv_w, conv_b, dt_bias, a_log, d_skip, ssm_norm_w, g_q, g_k,
           f_bias, w_out, g_xattn, g_mem, xq_w, xkv_w, xg_q, xg_k, xo_w, g_mlp, w_up, w_down):
    x2d = x.reshape(TOKENS, D_MODEL)
    mem2d = mem.reshape(BATCH * MEM_LEN, D_MODEL)
    depth = g_mix.shape[0]
    for l in range(depth):
        x2d = _layer(x2d, mem2d, g_mix[l], w_in[l], conv_w[l], conv_b[l], dt_bias[l], a_log[l],
                     d_skip[l], ssm_norm_w[l], g_q[l], g_k[l], f_bias[l], w_out[l], g_xattn[l],
                     g_mem[l], xq_w[l], xkv_w[l], xg_q[l], xg_k[l], xo_w[l], g_mlp[l], w_up[l],
                     w_down[l])
    return x2d.reshape(BATCH, SEQ, D_MODEL)
```

```python
import numpy as np
import jax
import jax.numpy as jnp
from jax import lax
from jax.experimental import pallas as pl
from jax.experimental.pallas import tpu as pltpu

F32 = jnp.float32
BF16 = jnp.bfloat16

D_MODEL = 1024
BATCH = 8
SEQ = 2048
TOKENS = BATCH * SEQ
MEM_LEN = 256
SSM_HEAD_DIM = 64
SSM_HEADS = 16
SSM_D_INNER = 1024
SSM_GROUPS = 2
SSM_STATE = 128
CONV_WIDTH = 4
CHUNK = 128
ATTN_HEAD_DIM = 64
ATTN_HEADS = 16
ATTN_WIDTH = 1024
XATTN_HEADS = 4
XATTN_HEAD_DIM = 256
D_FF = 4096
EPS = 1e-5

LANES = 128
N_MAIN = 5632
HEAD_PAIRS = ATTN_HEADS // 2
GROUP_W = SSM_D_INNER // SSM_GROUPS
DT_LANE = 16
VMEM_LIMIT = 56 * 1024 * 1024
NEG_BIG = -1e30
LOG2E = 1.4426950408889634


def _rms(xf, g_row):
    ms = jnp.mean(xf * xf, axis=-1, keepdims=True)
    return xf * lax.rsqrt(ms + EPS) * g_row


def _split2(a):
    hi = a.astype(BF16)
    mid = (a - hi.astype(F32)).astype(BF16)
    return jnp.concatenate([hi, mid], axis=-1)


def _split3(a):
    hi = a.astype(BF16)
    r1 = a - hi.astype(F32)
    mid = r1.astype(BF16)
    lo = (r1 - mid.astype(F32)).astype(BF16)
    return jnp.concatenate([hi, mid, lo], axis=-1)


def _softplus(x):
    return jnp.maximum(x, 0.0) + jnp.log1p(jnp.exp(-jnp.abs(x)))


def _silu(x):
    half = 0.5 * x
    return half + half * jnp.tanh(half)


def _dot(a, b):
    return jnp.dot(a, b, preferred_element_type=F32)


def _dot_nt(a, b):
    return lax.dot_general(a, b, (((1,), (1,)), ((), ())), preferred_element_type=F32)


def _interleave(*gens):
    live = list(gens)
    while live:
        live = [g for g in live if next(g, StopIteration) is not StopIteration]


AUX_PAIR_LANES = 16
GATE_ROWS = 16


def _gate_route_matrix():
    r = np.zeros((LANES, 3 * GATE_ROWS), np.float32)
    for h in range(ATTN_HEADS):
        for m in range(3):
            base = AUX_PAIR_LANES * (h // 2)
            r[base + 2 * m + (h % 2), GATE_ROWS * m + h] = 1.0
            r[base + 6 + 2 * m + (h % 2), GATE_ROWS * m + h] = -1.0
    return r


def _split3_rows(a):
    hi = a.astype(BF16)
    r1 = a - hi.astype(F32)
    mid = r1.astype(BF16)
    lo = (r1 - mid.astype(F32)).astype(BF16)
    return jnp.concatenate([hi, mid, lo], axis=0)


def _gates_kernel(g_ref, fb_ref, r_ref, aux_ref):
    row = lax.broadcasted_iota(jnp.int32, (CHUNK, CHUNK), 0)
    col = lax.broadcasted_iota(jnp.int32, (CHUNK, CHUNK), 1)
    triu = jnp.where(row <= col, 1.0, 0.0).astype(BF16)
    fb = fb_ref[...]
    rmat = r_ref[...]
    offset = jnp.zeros((GATE_ROWS, CHUNK), F32)
    for blk in range(SEQ // CHUNK):
        rows = slice(blk * CHUNK, (blk + 1) * CHUNK)
        f_t = g_ref[rows, :].T[:GATE_ROWS, :]
        log_f = -_softplus(-(f_t + fb))
        part = _dot(_split3_rows(log_f), triu)
        cum = (part[:GATE_ROWS] + part[GATE_ROWS:2 * GATE_ROWS] + part[2 * GATE_ROWS:]) + offset
        offset = jnp.broadcast_to(cum[:, CHUNK - 1:CHUNK], (GATE_ROWS, CHUNK))
        aux_t = _dot(rmat, _split3_rows(cum * LOG2E))
        aux_ref[rows, :] = aux_t.T.astype(BF16)


def _gates(gates_raw, fb_t, rmat):
    return pl.pallas_call(
        _gates_kernel,
        out_shape=jax.ShapeDtypeStruct((TOKENS, LANES), BF16),
        grid=(BATCH,),
        in_specs=[
            pl.BlockSpec((SEQ, LANES), lambda b: (b, 0)),
            pl.BlockSpec((GATE_ROWS, CHUNK), lambda b: (0, 0)),
            pl.BlockSpec((LANES, 3 * GATE_ROWS), lambda b: (0, 0)),
        ],
        out_specs=pl.BlockSpec((SEQ, LANES), lambda b: (b, 0)),
        compiler_params=pltpu.CompilerParams(dimension_semantics=("parallel",)),
        name="gates",
    )(gates_raw, fb_t, rmat)


def _head_expand_matrix():
    e = np.zeros((2 * LANES, SSM_D_INNER), np.float32)
    for h in range(SSM_HEADS):
        e[DT_LANE + h, h * SSM_HEAD_DIM:(h + 1) * SSM_HEAD_DIM] = 1.0
        e[LANES + DT_LANE + h, h * SSM_HEAD_DIM:(h + 1) * SSM_HEAD_DIM] = 1.0
    return e


CONV_TAIL = 16


def _conv_shift_matrix():
    s = np.zeros(((CONV_WIDTH - 1) * CHUNK, CHUNK + CONV_TAIL), np.float32)
    for k in range(CONV_WIDTH - 1):
        for t in range(CHUNK):
            src = t - 1 - k
            s[k * CHUNK + t, src if src >= 0 else CHUNK + CONV_TAIL + src] = 1.0
    return s


def _ssd_chunk(z_ref, xs_ref, bc_ref, g_ref, cwx_ref, cwb_ref, cbx_ref, cbb_ref,
               dtb_ref, alog_ref, ee_ref, shift_ref, dsk_ref, nw_ref, y_ref,
               xtail, btail, states):
    def conv_silu(u_ref, tail, w_ref, b_ref):
        cur = u_ref[...]
        shifted = _dot(shift_ref[...], jnp.concatenate([cur, tail], axis=0))
        acc = b_ref[...] + w_ref[3:4, :] * cur.astype(F32)
        for k in range(CONV_WIDTH - 1):
            acc = acc + w_ref[2 - k:3 - k, :] * shifted[k * CHUNK:(k + 1) * CHUNK, :]
        return _silu(acc), cur[CHUNK - CONV_TAIL:, :]

    xs, xtail = conv_silu(xs_ref, xtail, cwx_ref, cbx_ref)
    yield
    bc, btail = conv_silu(bc_ref, btail, cwb_ref, cbb_ref)

    dt = _softplus(g_ref[...] + dtb_ref[...])
    da = dt * (-jnp.exp(alog_ref[...]))
    row = lax.broadcasted_iota(jnp.int32, (CHUNK, CHUNK), 0)
    col = lax.broadcasted_iota(jnp.int32, (CHUNK, CHUNK), 1)
    causal = row >= col
    tril = jnp.where(causal, 1.0, 0.0).astype(BF16)
    part = _dot(tril, _split3(da))
    acs = part[:, :LANES] + part[:, LANES:2 * LANES] + part[:, 2 * LANES:]
    a_last = acs[CHUNK - 1:CHUNK, :]
    exp_a = jnp.exp(acs)
    dt_decay = dt * jnp.exp(a_last - acs)
    ee = ee_ref[...]
    e_dtdec = _dot(_split2(dt_decay), ee)
    e_expa = _dot(_split2(exp_a), ee)
    acs_t = acs.T
    dt_t = dt.T
    yield

    xs_b = xs.astype(BF16)
    xdec_b = (xs * e_dtdec).astype(BF16)
    lane = lax.broadcasted_iota(jnp.int32, (CHUNK, LANES), 1)
    first_head = lane < SSM_HEAD_DIM

    y_parts = []
    new_states = []
    for g in range(SSM_GROUPS):
        b_g = bc[:, g * SSM_STATE:(g + 1) * SSM_STATE]
        c_g = bc[:, (SSM_GROUPS + g) * SSM_STATE:(SSM_GROUPS + g + 1) * SSM_STATE]
        c_gb = c_g.astype(BF16)
        cb = _dot_nt(c_gb, b_g.astype(BF16))
        state = states[g]
        cols = slice(g * GROUP_W, (g + 1) * GROUP_W)
        y_off = _dot(c_gb, state.astype(BF16)) * e_expa[:, cols]
        st_new = _dot(b_g.T.astype(BF16), xdec_b[:, cols])
        new_states.append(state * e_expa[CHUNK - 1:CHUNK, cols] + st_new)
        diag = []
        for pair in range(GROUP_W // LANES):
            x_pair = xs_b[:, g * GROUP_W + pair * LANES:g * GROUP_W + (pair + 1) * LANES]
            res = []
            for j in range(2):
                h = g * (SSM_HEADS // SSM_GROUPS) + 2 * pair + j
                hl = DT_LANE + h
                a_col = jnp.broadcast_to(acs[:, hl:hl + 1], (CHUNK, CHUNK))
                seg = jnp.where(causal, a_col - acs_t[hl:hl + 1, :], -jnp.inf)
                m_h = cb * jnp.exp(seg) * dt_t[hl:hl + 1, :]
                res.append(_dot(m_h.astype(BF16), x_pair))
            diag.append(jnp.where(first_head, res[0], res[1]))
            if pair % 2 == 1:
                yield
        y_parts.append(jnp.concatenate(diag, axis=-1) + y_off)
    y = jnp.concatenate(y_parts, axis=-1) + dsk_ref[...] * xs
    y = y * _silu(z_ref[...].astype(F32))
    normed = []
    for g in range(SSM_GROUPS):
        y_g = y[:, g * GROUP_W:(g + 1) * GROUP_W]
        normed.append(y_g * lax.rsqrt(jnp.mean(y_g * y_g, axis=-1, keepdims=True) + EPS))
    y_ref[...] = (jnp.concatenate(normed, axis=-1) * nw_ref[...]).astype(BF16)
    return xtail, btail, new_states


FRONT_TM = 512
FRONT_TILES = TOKENS // FRONT_TM
FRONT_CHUNKS = FRONT_TM // CHUNK
TILES_PER_SEQ = SEQ // FRONT_TM
QKV_W = 3 * ATTN_WIDTH
Z_END = SSM_D_INNER
QKV_END = Z_END + QKV_W
XS_END = QKV_END + SSM_D_INNER
FRONT_TN = 256


def _front_kernel(x_ref, g_ref, w_ref, wg_ref, cwx_ref, cwb_ref, cbx_ref, cbb_ref, dtb_ref, alog_ref,
                  ee_ref, shift_ref, dsk_ref, nw_ref, qkv_ref, gate_ref, y_ref,
                  z_scr, xs_scr, bc_scr, g_scr, xtail_ref, btail_ref, *state_refs):
    i = pl.program_id(0)
    consts = (cwx_ref, cwb_ref, cbx_ref, cbb_ref, dtb_ref, alog_ref, ee_ref, shift_ref, dsk_ref,
              nw_ref)

    @pl.when(i == 0)
    def _():
        z_scr[1] = jnp.zeros(z_scr.shape[1:], BF16)
        xs_scr[1] = jnp.zeros(xs_scr.shape[1:], BF16)
        bc_scr[1] = jnp.zeros(bc_scr.shape[1:], BF16)
        g_scr[1] = jnp.zeros(g_scr.shape[1:], F32)
        xtail_ref[...] = jnp.zeros_like(xtail_ref)
        btail_ref[...] = jnp.zeros_like(btail_ref)
        for ref in state_refs:
            ref[...] = jnp.zeros_like(ref)

    def project(slot):
        h = _rms(x_ref[...], g_ref[...]).astype(BF16)
        for n in range(0, N_MAIN, FRONT_TN):
            val = _dot(h, w_ref[:, n:n + FRONT_TN]).astype(BF16)
            if n < Z_END:
                z_scr[slot, :, n:n + FRONT_TN] = val
            elif n < QKV_END:
                qkv_ref[:, n - Z_END:n - Z_END + FRONT_TN] = val
            elif n < XS_END:
                xs_scr[slot, :, n - QKV_END:n - QKV_END + FRONT_TN] = val
            else:
                bc_scr[slot, :, n - XS_END:n - XS_END + FRONT_TN] = val
            yield
        gates = _dot(h, wg_ref[...])
        gate_ref[...] = gates
        g_scr[slot] = gates
        yield

    def scan(slot):
        restart = lax.rem(i + (TILES_PER_SEQ - 1), TILES_PER_SEQ) == 0
        xtail = jnp.where(restart, 0.0, xtail_ref[...].astype(F32)).astype(BF16)
        btail = jnp.where(restart, 0.0, btail_ref[...].astype(F32)).astype(BF16)
        states = [jnp.where(restart, 0.0, ref[...]) for ref in state_refs]
        for sc in range(FRONT_CHUNKS):
            rows = pl.ds(sc * CHUNK, CHUNK)
            xtail, btail, states = yield from _ssd_chunk(
                z_scr.at[slot, rows], xs_scr.at[slot, rows], bc_scr.at[slot, rows],
                g_scr.at[slot, rows], *consts, y_ref.at[rows], xtail, btail, states)
            yield
        for ref, state in zip(state_refs, states):
            ref[...] = state
        xtail_ref[...] = xtail
        btail_ref[...] = btail

    for parity in (0, 1):
        @pl.when(lax.rem(i, 2) == parity)
        def _(parity=parity):
            _interleave(scan(1 - parity), project(parity))


def _front(x2d, g_row, w_main, w_gate, cw_x, cw_bc, cb_x, cb_bc, dtb_row, alog_row, ee, shift,
           dsk_row, nw_row):
    last = FRONT_TILES - 1
    cur = lambda i: (jnp.minimum(i, last), 0)
    prev = lambda i: (jnp.maximum(i - 1, 0), 0)
    full = lambda shape: pl.BlockSpec(shape, lambda i: (0, 0))
    return pl.pallas_call(
        _front_kernel,
        out_shape=(jax.ShapeDtypeStruct((TOKENS, QKV_W), BF16),
                   jax.ShapeDtypeStruct((TOKENS, LANES), F32),
                   jax.ShapeDtypeStruct((TOKENS, SSM_D_INNER), BF16)),
        grid=(FRONT_TILES + 1,),
        in_specs=[
            pl.BlockSpec((FRONT_TM, D_MODEL), cur),
            full((1, D_MODEL)),
            pl.BlockSpec((D_MODEL, N_MAIN), lambda i: (0, 0), pipeline_mode=pl.Buffered(1)),
            full((D_MODEL, LANES)),
            full((CONV_WIDTH, SSM_D_INNER)), full((CONV_WIDTH, GROUP_W)),
            full((1, SSM_D_INNER)), full((1, GROUP_W)),
            full((1, LANES)), full((1, LANES)),
            full((2 * LANES, SSM_D_INNER)),
            full(((CONV_WIDTH - 1) * CHUNK, CHUNK + CONV_TAIL)),
            full((1, SSM_D_INNER)), full((1, SSM_D_INNER)),
        ],
        out_specs=(pl.BlockSpec((FRONT_TM, QKV_W), cur),
                   pl.BlockSpec((FRONT_TM, LANES), cur),
                   pl.BlockSpec((FRONT_TM, SSM_D_INNER), prev)),
        scratch_shapes=[
            pltpu.VMEM((2, FRONT_TM, SSM_D_INNER), BF16),
            pltpu.VMEM((2, FRONT_TM, SSM_D_INNER), BF16),
            pltpu.VMEM((2, FRONT_TM, GROUP_W), BF16),
            pltpu.VMEM((2, FRONT_TM, LANES), F32),
            pltpu.VMEM((CONV_TAIL, SSM_D_INNER), BF16),
            pltpu.VMEM((CONV_TAIL, GROUP_W), BF16),
            *[pltpu.VMEM((SSM_STATE, GROUP_W), F32) for _ in range(SSM_GROUPS)],
        ],
        compiler_params=pltpu.CompilerParams(
            dimension_semantics=("arbitrary",), vmem_limit_bytes=VMEM_LIMIT),
        name="front",
    )(x2d, g_row, w_main, w_gate, cw_x, cw_bc, cb_x, cb_bc, dtb_row, alog_row, ee, shift, dsk_row,
      nw_row)


ATT_T = 256
ATT_K = 2 * LANES
ONES_ROWS = 16
VT_HEAD_ROWS = ATTN_HEAD_DIM + ONES_ROWS


def _attn_kernel(q_ref, k_ref, v_ref, aux_ref, gq_ref, gk_ref, o_ref, ka_ref, vt_ref, qt_ref, s_ref,
                 m_ref):
    nblk = SEQ // ATT_T
    pair = pl.program_id(1)
    lane = lax.broadcasted_iota(jnp.int32, (ATT_T, LANES), 1)
    sub = lane & (AUX_PAIR_LANES - 1)
    in_pair = (lane >> 4) == pair
    is_val = in_pair & (sub < 6)
    is_neg = in_pair & (sub >= 6) & (sub < 12)
    scale = ATTN_HEAD_DIM ** -0.5 * LOG2E
    krow = lax.broadcasted_iota(jnp.int32, (ATT_T, ATT_T), 0)
    qcol = lax.broadcasted_iota(jnp.int32, (ATT_T, ATT_T), 1)
    causal_t = qcol >= krow
    trow = lax.broadcasted_iota(jnp.int32, (LANES, ATT_T), 0)
    tsub = trow & (AUX_PAIR_LANES - 1)
    t_in_pair = (trow >> 4) == pair
    t_val = t_in_pair & (tsub < 6)
    t_neg = t_in_pair & (tsub >= 6) & (tsub < 12)

    def head_norm_t(u_t, g_t):
        sq = u_t * u_t
        halves = []
        for j in range(2):
            hrows = slice(j * ATTN_HEAD_DIM, (j + 1) * ATTN_HEAD_DIM)
            ms = jnp.sum(sq[hrows, :], axis=0, keepdims=True) * (1.0 / ATTN_HEAD_DIM)
            halves.append(u_t[hrows, :] * lax.rsqrt(ms + EPS))
        return jnp.concatenate(halves, axis=0) * g_t

    for j in range(2):
        vt_ref[j * VT_HEAD_ROWS + ATTN_HEAD_DIM:(j + 1) * VT_HEAD_ROWS, :] = jnp.ones(
            (ONES_ROWS, SEQ), BF16)

    def prepare(i):
        rows = slice(i * ATT_T, (i + 1) * ATT_T)
        qn_t = head_norm_t(q_ref[rows, :].astype(F32).T, gq_ref[...] * scale)
        kn = head_norm_t(k_ref[rows, :].astype(F32).T, gk_ref[...]).T
        aux = aux_ref[rows, :].astype(F32)
        aux_t = aux.T
        k_aux = jnp.where(is_neg, aux, jnp.where(is_val, 1.0, 0.0))
        ka_ref[rows, :] = jnp.concatenate([kn, k_aux], axis=-1).astype(BF16)
        v_t = v_ref[rows, :].astype(F32).T.astype(BF16)
        for j in range(2):
            vt_ref[j * VT_HEAD_ROWS:j * VT_HEAD_ROWS + ATTN_HEAD_DIM, rows] = (
                v_t[j * ATTN_HEAD_DIM:(j + 1) * ATTN_HEAD_DIM, :])
        for j in range(2):
            mine = (tsub & 1) == j
            q_aux_t = jnp.where(t_val & mine, aux_t, jnp.where(t_neg & mine, 1.0, 0.0))
            q_main_t = jnp.where((trow >> 6) == j, qn_t, 0.0)
            qt_ref[i, j] = jnp.concatenate([q_main_t, q_aux_t], axis=0).astype(BF16)
        yield

    def scores(i):
        for j in range(2):
            qa_t = qt_ref[i, j]
            tile_max = None
            for t in range(i + 1):
                s = _dot(ka_ref[t * ATT_T:(t + 1) * ATT_T, :], qa_t)
                if t == i:
                    s = jnp.where(causal_t, s, NEG_BIG)
                s_ref[i % 2, j, t] = s
                tile_max = s if tile_max is None else jnp.maximum(tile_max, s)
                yield
            m = jnp.max(tile_max, axis=0, keepdims=True)
            m_ref[i % 2, j] = jnp.broadcast_to(m, (8, ATT_T))

    def softmax_values(i):
        outs = []
        for j in range(2):
            m = m_ref[i % 2, j][0:1, :]
            acc = None
            for t in range(i + 1):
                pv = _dot(vt_ref[j * VT_HEAD_ROWS:(j + 1) * VT_HEAD_ROWS, t * ATT_T:(t + 1) * ATT_T],
                          jnp.exp2(s_ref[i % 2, j, t] - m).astype(BF16))
                acc = pv if acc is None else acc + pv
                yield
            outs.append(acc[:ATTN_HEAD_DIM, :] / acc[ATTN_HEAD_DIM:ATTN_HEAD_DIM + 1, :])
        o_t = jnp.concatenate(outs, axis=0)
        o_ref[i * ATT_T:(i + 1) * ATT_T, :] = o_t.T.astype(BF16)

    def chain(*gens):
        for g in gens:
            yield from g

    last = nblk - 1
    _interleave(chain(prepare(last), prepare(0)))
    _interleave(scores(last), chain(*[prepare(t) for t in range(1, last)]))
    for i in range(last, -1, -1):
        _interleave(softmax_values(i), *([scores(i - 1)] if i > 0 else []))


def _attention(qkv, aux, gq_row, gk_row):
    qcol = ATTN_WIDTH // LANES
    return pl.pallas_call(
        _attn_kernel,
        out_shape=jax.ShapeDtypeStruct((TOKENS, ATTN_WIDTH), BF16),
        grid=(BATCH, HEAD_PAIRS),
        in_specs=[
            pl.BlockSpec((SEQ, LANES), lambda b, hp: (b, hp)),
            pl.BlockSpec((SEQ, LANES), lambda b, hp: (b, qcol + hp)),
            pl.BlockSpec((SEQ, LANES), lambda b, hp: (b, 2 * qcol + hp)),
            pl.BlockSpec((SEQ, LANES), lambda b, hp: (b, 0)),
            pl.BlockSpec((LANES, ATT_T), lambda b, hp: (0, 0)),
            pl.BlockSpec((LANES, ATT_T), lambda b, hp: (0, 0)),
        ],
        out_specs=pl.BlockSpec((SEQ, LANES), lambda b, hp: (b, hp)),
        scratch_shapes=[
            pltpu.VMEM((SEQ, ATT_K), BF16),
            pltpu.VMEM((2 * VT_HEAD_ROWS, SEQ), BF16),
            pltpu.VMEM((SEQ // ATT_T, 2, ATT_K, ATT_T), BF16),
            pltpu.VMEM((2, 2, SEQ // ATT_T, ATT_T, ATT_T), F32),
            pltpu.VMEM((2, 2, 8, ATT_T), F32),
        ],
        compiler_params=pltpu.CompilerParams(
            dimension_semantics=("parallel", "parallel"),
            vmem_limit_bytes=VMEM_LIMIT),
        name="fox_attention",
    )(qkv, qkv, qkv, aux, gq_row, gk_row)


OUT_TN = 256


def _outproj_kernel(y_ref, o_ref, x_ref, w_ref, out_ref):
    y = y_ref[...]
    o = o_ref[...]
    for n in range(0, D_MODEL, OUT_TN):
        cols = slice(n, n + OUT_TN)
        out_ref[:, cols] = (x_ref[:, cols]
                            + _dot(y, w_ref[:SSM_D_INNER, cols].astype(BF16))
                            + _dot(o, w_ref[SSM_D_INNER:, cols].astype(BF16)))


def _outproj(y, o, x2d, w_out, tm=1024):
    return pl.pallas_call(
        _outproj_kernel,
        out_shape=jax.ShapeDtypeStruct((TOKENS, D_MODEL), F32),
        grid=(TOKENS // tm,),
        in_specs=[
            pl.BlockSpec((tm, SSM_D_INNER), lambda i: (i, 0)),
            pl.BlockSpec((tm, ATTN_WIDTH), lambda i: (i, 0)),
            pl.BlockSpec((tm, D_MODEL), lambda i: (i, 0)),
            pl.BlockSpec((SSM_D_INNER + ATTN_WIDTH, D_MODEL), lambda i: (0, 0),
                         pipeline_mode=pl.Buffered(1)),
        ],
        out_specs=pl.BlockSpec((tm, D_MODEL), lambda i: (i, 0)),
        compiler_params=pltpu.CompilerParams(
            dimension_semantics=("parallel",), vmem_limit_bytes=VMEM_LIMIT),
        name="out_proj",
    )(y, o, x2d, w_out)


def _memkv_kernel(m_ref, g_ref, w_ref, gk_ref, knt_ref, v_ref):
    h = _rms(m_ref[...], g_ref[...]).astype(BF16)
    for a in range(XATTN_HEADS):
        cols = slice(a * XATTN_HEAD_DIM, (a + 1) * XATTN_HEAD_DIM)
        kn = _rms(_dot(h, w_ref[:, cols].astype(BF16)), gk_ref[...])
        knt_ref[cols, :] = kn.T.astype(BF16)
    for n in range(0, D_MODEL, XATTN_HEAD_DIM):
        v_ref[:, n:n + XATTN_HEAD_DIM] = _dot(
            h, w_ref[:, D_MODEL + n:D_MODEL + n + XATTN_HEAD_DIM].astype(BF16)).astype(BF16)


def _memkv(mem2d, g_row, wkv, gk_row):
    return pl.pallas_call(
        _memkv_kernel,
        out_shape=(jax.ShapeDtypeStruct((BATCH * D_MODEL, MEM_LEN), BF16),
                   jax.ShapeDtypeStruct((BATCH * MEM_LEN, D_MODEL), BF16)),
        grid=(BATCH,),
        in_specs=[
            pl.BlockSpec((MEM_LEN, D_MODEL), lambda b: (b, 0)),
            pl.BlockSpec((1, D_MODEL), lambda b: (0, 0)),
            pl.BlockSpec((D_MODEL, 2 * D_MODEL), lambda b: (0, 0), pipeline_mode=pl.Buffered(1)),
            pl.BlockSpec((1, XATTN_HEAD_DIM), lambda b: (0, 0)),
        ],
        out_specs=(pl.BlockSpec((D_MODEL, MEM_LEN), lambda b: (b, 0)),
                   pl.BlockSpec((MEM_LEN, D_MODEL), lambda b: (b, 0))),
        compiler_params=pltpu.CompilerParams(
            dimension_semantics=("parallel",), vmem_limit_bytes=VMEM_LIMIT),
        name="mem_kv",
    )(mem2d, g_row, wkv, gk_row)


XATTN_SUB = 512


def _xattn_kernel(x_ref, g_ref, wq_ref, knt_ref, v_ref, gq_ref, wo_ref, out_ref, q_scr, o_scr):
    nsub = x_ref.shape[0] // XATTN_SUB
    scale = XATTN_HEAD_DIM ** -0.5 * LOG2E
    head_cols = [slice(a * XATTN_HEAD_DIM, (a + 1) * XATTN_HEAD_DIM) for a in range(XATTN_HEADS)]

    def project(k):
        rows = slice(k * XATTN_SUB, (k + 1) * XATTN_SUB)
        h = _rms(x_ref[rows, :], g_ref[...]).astype(BF16)
        for cols in head_cols:
            q_scr[k % 2, :, cols] = _dot(h, wq_ref[:, cols])
            yield

    def attend(k):
        for cols in head_cols:
            qn = (_rms(q_scr[k % 2, :, cols], gq_ref[...]) * scale).astype(BF16)
            s = _dot(qn, knt_ref[cols, :])
            e = jnp.exp2(s - jnp.max(s, axis=-1, keepdims=True))
            p = e / jnp.sum(e, axis=-1, keepdims=True)
            o_scr[k % 2, :, cols] = _dot(p.astype(BF16), v_ref[:, cols]).astype(BF16)
            yield

    def output(k):
        rows = slice(k * XATTN_SUB, (k + 1) * XATTN_SUB)
        o = o_scr[k % 2]
        for n in range(0, D_MODEL, OUT_TN):
            cols = slice(n, n + OUT_TN)
            out_ref[rows, cols] = x_ref[rows, cols] + _dot(o, wo_ref[:, cols])
            yield

    for step in range(nsub + 2):
        stage = []
        if 0 <= step - 2 < nsub:
            stage.append(output(step - 2))
        if 0 <= step - 1 < nsub:
            stage.append(attend(step - 1))
        if step < nsub:
            stage.append(project(step))
        _interleave(*stage)


def _xattn(x1, g_row, wq, knt, v, gq_row, wo, tm=SEQ):
    nt = SEQ // tm
    return pl.pallas_call(
        _xattn_kernel,
        out_shape=jax.ShapeDtypeStruct((TOKENS, D_MODEL), F32),
        grid=(BATCH, nt),
        in_specs=[
            pl.BlockSpec((tm, D_MODEL), lambda b, i: (b * nt + i, 0)),
            pl.BlockSpec((1, D_MODEL), lambda b, i: (0, 0)),
            pl.BlockSpec((D_MODEL, D_MODEL), lambda b, i: (0, 0)),
            pl.BlockSpec((D_MODEL, MEM_LEN), lambda b, i: (b, 0)),
            pl.BlockSpec((MEM_LEN, D_MODEL), lambda b, i: (b, 0)),
            pl.BlockSpec((1, XATTN_HEAD_DIM), lambda b, i: (0, 0)),
            pl.BlockSpec((D_MODEL, D_MODEL), lambda b, i: (0, 0)),
        ],
        out_specs=pl.BlockSpec((tm, D_MODEL), lambda b, i: (b * nt + i, 0)),
        scratch_shapes=[pltpu.VMEM((2, XATTN_SUB, D_MODEL), F32),
                        pltpu.VMEM((2, XATTN_SUB, D_MODEL), BF16)],
        compiler_params=pltpu.CompilerParams(
            dimension_semantics=("parallel", "parallel"), vmem_limit_bytes=VMEM_LIMIT),
        name="mem_xattn",
    )(x1, g_row, wq, knt, v, gq_row, wo)


FF_CHUNK = 1024


def _mlp_kernel(x_ref, g_ref, wu_ref, wd_ref, out_ref, h_scr, acc_scr):
    h_scr[...] = _rms(x_ref[...], g_ref[...]).astype(BF16)
    acc_scr[...] = x_ref[...]
    for f in range(0, D_FF, FF_CHUNK):
        u = jnp.maximum(_dot(h_scr[...], wu_ref[:, f:f + FF_CHUNK].astype(BF16)), 0.0)
        acc_scr[...] += _dot((u * u).astype(BF16), wd_ref[f:f + FF_CHUNK, :].astype(BF16))
    out_ref[...] = acc_scr[...]


def _mlp(x2, g_row, wu, wd, tm=512):
    return pl.pallas_call(
        _mlp_kernel,
        out_shape=jax.ShapeDtypeStruct((TOKENS, D_MODEL), F32),
        grid=(TOKENS // tm,),
        in_specs=[
            pl.BlockSpec((tm, D_MODEL), lambda i: (i, 0)),
            pl.BlockSpec((1, D_MODEL), lambda i: (0, 0)),
            pl.BlockSpec((D_MODEL, D_FF), lambda i: (0, 0), pipeline_mode=pl.Buffered(1)),
            pl.BlockSpec((D_FF, D_MODEL), lambda i: (0, 0), pipeline_mode=pl.Buffered(1)),
        ],
        out_specs=pl.BlockSpec((tm, D_MODEL), lambda i: (i, 0)),
        scratch_shapes=[pltpu.VMEM((tm, D_MODEL), BF16), pltpu.VMEM((tm, D_MODEL), F32)],
        compiler_params=pltpu.CompilerParams(
            dimension_semantics=("parallel",), vmem_limit_bytes=VMEM_LIMIT),
        name="relu2_mlp",
    )(x2, g_row, wu, wd)


def _lane_row(vec, offset):
    return jnp.zeros((1, LANES), F32).at[0, offset:offset + vec.shape[0]].set(vec.astype(F32))


def _layer(x2d, mem2d, g_mix, w_in, conv_w, conv_b, dt_bias, a_log, d_skip, ssm_norm_w,
           g_q, g_k, f_bias, w_out, g_xattn, g_mem, xq_w, xkv_w, xg_q, xg_k, xo_w,
           g_mlp, w_up, w_down):
    z0, xbc0 = 0, SSM_D_INNER
    bc0 = xbc0 + SSM_D_INNER
    dt0 = bc0 + 2 * SSM_GROUPS * SSM_STATE
    q0 = dt0 + SSM_HEADS
    k0 = q0 + ATTN_WIDTH
    v0 = k0 + ATTN_WIDTH
    f0 = v0 + ATTN_WIDTH
    w_main = jnp.concatenate(
        [w_in[:, z0:xbc0], w_in[:, q0:f0], w_in[:, xbc0:dt0]], axis=1).astype(BF16)
    w_gate = jnp.concatenate(
        [w_in[:, f0:f0 + ATTN_HEADS], w_in[:, dt0:q0],
         jnp.zeros((D_MODEL, LANES - ATTN_HEADS - SSM_HEADS), F32)], axis=1).astype(BF16)
    row = lambda v: v.astype(F32).reshape(1, -1)

    qkv, gates_raw, y = _front(
        x2d, row(g_mix), w_main, w_gate,
        conv_w[:, :SSM_D_INNER], conv_w[:, SSM_D_INNER:],
        row(conv_b[:SSM_D_INNER]), row(conv_b[SSM_D_INNER:]),
        _lane_row(dt_bias, DT_LANE), _lane_row(a_log, DT_LANE),
        jnp.asarray(_head_expand_matrix(), BF16), jnp.asarray(_conv_shift_matrix(), BF16),
        row(jnp.repeat(d_skip, SSM_HEAD_DIM)), row(ssm_norm_w))

    fb_t = jnp.broadcast_to(f_bias.astype(F32)[:, None], (GATE_ROWS, CHUNK))
    aux = _gates(gates_raw, fb_t, jnp.asarray(_gate_route_matrix(), BF16))

    gain_t = lambda g: jnp.broadcast_to(jnp.tile(g.astype(F32), 2)[:, None], (LANES, ATT_T))
    o = _attention(qkv, aux, gain_t(g_q), gain_t(g_k))

    x1 = _outproj(y, o, x2d, w_out)

    knt, mem_v = _memkv(mem2d, row(g_mem), xkv_w, row(xg_k))
    x2 = _xattn(x1, row(g_xattn), xq_w.astype(BF16), knt, mem_v, row(xg_q), xo_w.astype(BF16))

    return _mlp(x2, row(g_mlp), w_up, w_down)


def kernel(x, mem, g_mix, w_in, conv_w, conv_b, dt_bias, a_log, d_skip, ssm_norm_w, g_q, g_k,
           f_bias, w_out, g_xattn, g_mem, xq_w, xkv_w, xg_q, xg_k, xo_w, g_mlp, w_up, w_down):
    x2d = x.reshape(TOKENS, D_MODEL)
    mem2d = mem.reshape(BATCH * MEM_LEN, D_MODEL)
    depth = g_mix.shape[0]
    for l in range(depth):
        x2d = _layer(x2d, mem2d, g_mix[l], w_in[l], conv_w[l], conv_b[l], dt_bias[l], a_log[l],
                     d_skip[l], ssm_norm_w[l], g_q[l], g_k[l], f_bias[l], w_out[l], g_xattn[l],
                     g_mem[l], xq_w[l], xkv_w[l], xg_q[l], xg_k[l], xo_w[l], g_mlp[l], w_up[l],
                     w_down[l])
    return x2d.reshape(BATCH, SEQ, D_MODEL)
```

```python
import numpy as np
import jax
import jax.numpy as jnp
from jax import lax
from jax.experimental import pallas as pl
from jax.experimental.pallas import tpu as pltpu

F32 = jnp.float32
BF16 = jnp.bfloat16

D_MODEL = 1024
BATCH = 8
SEQ = 2048
TOKENS = BATCH * SEQ
MEM_LEN = 256
SSM_HEAD_DIM = 64
SSM_HEADS = 16
SSM_D_INNER = 1024
SSM_GROUPS = 2
SSM_STATE = 128
CONV_WIDTH = 4
CHUNK = 128
ATTN_HEAD_DIM = 64
ATTN_HEADS = 16
ATTN_WIDTH = 1024
XATTN_HEADS = 4
XATTN_HEAD_DIM = 256
D_FF = 4096
EPS = 1e-5

LANES = 128
HEAD_PAIRS = ATTN_HEADS // 2
GROUP_W = SSM_D_INNER // SSM_GROUPS
DT_LANE = 16
VMEM_LIMIT = 56 * 1024 * 1024
NEG_BIG = -1e30
LOG2E = 1.4426950408889634


def _rms(xf, g_row):
    ms = jnp.mean(xf * xf, axis=-1, keepdims=True)
    return xf * lax.rsqrt(ms + EPS) * g_row


def _split2(a):
    hi = a.astype(BF16)
    mid = (a - hi.astype(F32)).astype(BF16)
    return jnp.concatenate([hi, mid], axis=-1)


def _split3(a):
    hi = a.astype(BF16)
    r1 = a - hi.astype(F32)
    mid = r1.astype(BF16)
    lo = (r1 - mid.astype(F32)).astype(BF16)
    return jnp.concatenate([hi, mid, lo], axis=-1)


def _softplus(x):
    return jnp.maximum(x, 0.0) + jnp.log1p(jnp.exp(-jnp.abs(x)))


def _silu(x):
    half = 0.5 * x
    return half + half * jnp.tanh(half)


def _dot(a, b):
    return jnp.dot(a, b, preferred_element_type=F32)


def _dot_nt(a, b):
    return lax.dot_general(a, b, (((1,), (1,)), ((), ())), preferred_element_type=F32)


def _interleave(*gens):
    live = list(gens)
    while live:
        live = [g for g in live if next(g, StopIteration) is not StopIteration]


AUX_PAIR_LANES = 16
GATE_ROWS = 16


def _gate_route_matrix():
    r = np.zeros((LANES, 3 * GATE_ROWS), np.float32)
    for h in range(ATTN_HEADS):
        for m in range(3):
            base = AUX_PAIR_LANES * (h // 2)
            r[base + 2 * m + (h % 2), GATE_ROWS * m + h] = 1.0
            r[base + 6 + 2 * m + (h % 2), GATE_ROWS * m + h] = -1.0
    return r


def _split3_rows(a):
    hi = a.astype(BF16)
    r1 = a - hi.astype(F32)
    mid = r1.astype(BF16)
    lo = (r1 - mid.astype(F32)).astype(BF16)
    return jnp.concatenate([hi, mid, lo], axis=0)


def _gates_kernel(g_ref, fb_ref, r_ref, aux_ref):
    row = lax.broadcasted_iota(jnp.int32, (CHUNK, CHUNK), 0)
    col = lax.broadcasted_iota(jnp.int32, (CHUNK, CHUNK), 1)
    triu = jnp.where(row <= col, 1.0, 0.0).astype(BF16)
    fb = fb_ref[...]
    rmat = r_ref[...]
    offset = jnp.zeros((GATE_ROWS, CHUNK), F32)
    for blk in range(SEQ // CHUNK):
        rows = slice(blk * CHUNK, (blk + 1) * CHUNK)
        f_t = g_ref[rows, :].T[:GATE_ROWS, :]
        log_f = -_softplus(-(f_t + fb))
        part = _dot(_split3_rows(log_f), triu)
        cum = (part[:GATE_ROWS] + part[GATE_ROWS:2 * GATE_ROWS] + part[2 * GATE_ROWS:]) + offset
        offset = jnp.broadcast_to(cum[:, CHUNK - 1:CHUNK], (GATE_ROWS, CHUNK))
        aux_t = _dot(rmat, _split3_rows(cum * LOG2E))
        aux_ref[rows, :] = aux_t.T.astype(BF16)


def _gates(gates_raw, fb_t, rmat):
    return pl.pallas_call(
        _gates_kernel,
        out_shape=jax.ShapeDtypeStruct((TOKENS, LANES), BF16),
        grid=(BATCH,),
        in_specs=[
            pl.BlockSpec((SEQ, LANES), lambda b: (b, 0)),
            pl.BlockSpec((GATE_ROWS, CHUNK), lambda b: (0, 0)),
            pl.BlockSpec((LANES, 3 * GATE_ROWS), lambda b: (0, 0)),
        ],
        out_specs=pl.BlockSpec((SEQ, LANES), lambda b: (b, 0)),
        compiler_params=pltpu.CompilerParams(dimension_semantics=("parallel",)),
        name="gates",
    )(gates_raw, fb_t, rmat)


def _head_expand_matrix():
    e = np.zeros((2 * LANES, SSM_D_INNER), np.float32)
    for h in range(SSM_HEADS):
        e[DT_LANE + h, h * SSM_HEAD_DIM:(h + 1) * SSM_HEAD_DIM] = 1.0
        e[LANES + DT_LANE + h, h * SSM_HEAD_DIM:(h + 1) * SSM_HEAD_DIM] = 1.0
    return e


CONV_TAIL = 16


def _conv_shift_matrix():
    s = np.zeros(((CONV_WIDTH - 1) * CHUNK, CHUNK + CONV_TAIL), np.float32)
    for k in range(CONV_WIDTH - 1):
        for t in range(CHUNK):
            src = t - 1 - k
            s[k * CHUNK + t, src if src >= 0 else CHUNK + CONV_TAIL + src] = 1.0
    return s


def _ssd_chunk(z_ref, xs_ref, bc_ref, g_ref, cwx_ref, cwb_ref, cbx_ref, cbb_ref,
               dtb_ref, alog_ref, ee_ref, shift_ref, dsk_ref, nw_ref, y_ref,
               xtail, btail, states):
    def conv_silu(u_ref, tail, w_ref, b_ref):
        cur = u_ref[...]
        shifted = _dot(shift_ref[...], jnp.concatenate([cur, tail], axis=0))
        acc = b_ref[...] + w_ref[3:4, :] * cur.astype(F32)
        for k in range(CONV_WIDTH - 1):
            acc = acc + w_ref[2 - k:3 - k, :] * shifted[k * CHUNK:(k + 1) * CHUNK, :]
        return _silu(acc), cur[CHUNK - CONV_TAIL:, :]

    xs, xtail = conv_silu(xs_ref, xtail, cwx_ref, cbx_ref)
    yield
    bc, btail = conv_silu(bc_ref, btail, cwb_ref, cbb_ref)

    dt = _softplus(g_ref[...] + dtb_ref[...])
    da = dt * (-jnp.exp(alog_ref[...]))
    row = lax.broadcasted_iota(jnp.int32, (CHUNK, CHUNK), 0)
    col = lax.broadcasted_iota(jnp.int32, (CHUNK, CHUNK), 1)
    causal = row >= col
    tril = jnp.where(causal, 1.0, 0.0).astype(BF16)
    part = _dot(tril, _split3(da))
    acs = part[:, :LANES] + part[:, LANES:2 * LANES] + part[:, 2 * LANES:]
    a_last = acs[CHUNK - 1:CHUNK, :]
    exp_a = jnp.exp(acs)
    dt_decay = dt * jnp.exp(a_last - acs)
    ee = ee_ref[...]
    e_dtdec = _dot(_split2(dt_decay), ee)
    e_expa = _dot(_split2(exp_a), ee)
    acs_t = acs.T
    dt_t = dt.T
    yield

    xs_b = xs.astype(BF16)
    xdec_b = (xs * e_dtdec).astype(BF16)
    lane = lax.broadcasted_iota(jnp.int32, (CHUNK, LANES), 1)
    first_head = lane < SSM_HEAD_DIM

    y_parts = []
    new_states = []
    for g in range(SSM_GROUPS):
        b_g = bc[:, g * SSM_STATE:(g + 1) * SSM_STATE]
        c_g = bc[:, (SSM_GROUPS + g) * SSM_STATE:(SSM_GROUPS + g + 1) * SSM_STATE]
        c_gb = c_g.astype(BF16)
        cb = _dot_nt(c_gb, b_g.astype(BF16))
        state = states[g]
        cols = slice(g * GROUP_W, (g + 1) * GROUP_W)
        y_off = _dot(c_gb, state.astype(BF16)) * e_expa[:, cols]
        st_new = _dot(b_g.T.astype(BF16), xdec_b[:, cols])
        new_states.append(state * e_expa[CHUNK - 1:CHUNK, cols] + st_new)
        diag = []
        for pair in range(GROUP_W // LANES):
            x_pair = xs_b[:, g * GROUP_W + pair * LANES:g * GROUP_W + (pair + 1) * LANES]
            res = []
            for j in range(2):
                h = g * (SSM_HEADS // SSM_GROUPS) + 2 * pair + j
                hl = DT_LANE + h
                a_col = jnp.broadcast_to(acs[:, hl:hl + 1], (CHUNK, CHUNK))
                seg = jnp.where(causal, a_col - acs_t[hl:hl + 1, :], -jnp.inf)
                m_h = cb * jnp.exp(seg) * dt_t[hl:hl + 1, :]
                res.append(_dot(m_h.astype(BF16), x_pair))
            diag.append(jnp.where(first_head, res[0], res[1]))
            if pair % 2 == 1:
                yield
        y_parts.append(jnp.concatenate(diag, axis=-1) + y_off)
    y = jnp.concatenate(y_parts, axis=-1) + dsk_ref[...] * xs
    y = y * _silu(z_ref[...].astype(F32))
    normed = []
    for g in range(SSM_GROUPS):
        y_g = y[:, g * GROUP_W:(g + 1) * GROUP_W]
        normed.append(y_g * lax.rsqrt(jnp.mean(y_g * y_g, axis=-1, keepdims=True) + EPS))
    y_ref[...] = (jnp.concatenate(normed, axis=-1) * nw_ref[...]).astype(BF16)
    return xtail, btail, new_states


FRONT_TM = 512
FRONT_TILES = TOKENS // FRONT_TM
FRONT_CHUNKS = FRONT_TM // CHUNK
TILES_PER_SEQ = SEQ // FRONT_TM
QKV_W = 3 * ATTN_WIDTH
W_Z_ROW = 0
W_XS_ROW = W_Z_ROW + SSM_D_INNER
W_BC_ROW = W_XS_ROW + SSM_D_INNER
W_DT_ROW = W_BC_ROW + 2 * SSM_GROUPS * SSM_STATE
W_QKV_ROW = W_DT_ROW + SSM_HEADS
W_F_ROW = W_QKV_ROW + QKV_W
W_ROWS = W_F_ROW + ATTN_HEADS
FRONT_TN = 256


def _front_kernel(x_ref, g_ref, w_ref, wg_ref, cwx_ref, cwb_ref, cbx_ref, cbb_ref, dtb_ref, alog_ref,
                  ee_ref, shift_ref, dsk_ref, nw_ref, qkv_ref, gate_ref, y_ref,
                  z_scr, xs_scr, bc_scr, g_scr, xtail_ref, btail_ref, *state_refs):
    i = pl.program_id(0)
    consts = (cwx_ref, cwb_ref, cbx_ref, cbb_ref, dtb_ref, alog_ref, ee_ref, shift_ref, dsk_ref,
              nw_ref)

    @pl.when(i == 0)
    def _():
        z_scr[1] = jnp.zeros(z_scr.shape[1:], BF16)
        xs_scr[1] = jnp.zeros(xs_scr.shape[1:], BF16)
        bc_scr[1] = jnp.zeros(bc_scr.shape[1:], BF16)
        g_scr[1] = jnp.zeros(g_scr.shape[1:], F32)
        xtail_ref[...] = jnp.zeros_like(xtail_ref)
        btail_ref[...] = jnp.zeros_like(btail_ref)
        for ref in state_refs:
            ref[...] = jnp.zeros_like(ref)

    def project(slot):
        h = _rms(x_ref[...], g_ref[...]).astype(BF16)
        pieces = ((W_Z_ROW, SSM_D_INNER, z_scr.at[slot]), (W_QKV_ROW, QKV_W, qkv_ref),
                  (W_XS_ROW, SSM_D_INNER, xs_scr.at[slot]), (W_BC_ROW, GROUP_W, bc_scr.at[slot]))
        for src, width, dst_ref in pieces:
            for n in range(0, width, FRONT_TN):
                dst_ref[:, n:n + FRONT_TN] = _dot_nt(
                    h, w_ref[src + n:src + n + FRONT_TN, :]).astype(BF16)
                yield
        gates = _dot(h, wg_ref[...])
        gate_ref[...] = gates
        g_scr[slot] = gates
        yield

    def scan(slot):
        restart = lax.rem(i + (TILES_PER_SEQ - 1), TILES_PER_SEQ) == 0
        xtail = jnp.where(restart, 0.0, xtail_ref[...].astype(F32)).astype(BF16)
        btail = jnp.where(restart, 0.0, btail_ref[...].astype(F32)).astype(BF16)
        states = [jnp.where(restart, 0.0, ref[...]) for ref in state_refs]
        for sc in range(FRONT_CHUNKS):
            rows = pl.ds(sc * CHUNK, CHUNK)
            xtail, btail, states = yield from _ssd_chunk(
                z_scr.at[slot, rows], xs_scr.at[slot, rows], bc_scr.at[slot, rows],
                g_scr.at[slot, rows], *consts, y_ref.at[rows], xtail, btail, states)
            yield
        for ref, state in zip(state_refs, states):
            ref[...] = state
        xtail_ref[...] = xtail
        btail_ref[...] = btail

    for parity in (0, 1):
        @pl.when(lax.rem(i, 2) == parity)
        def _(parity=parity):
            _interleave(scan(1 - parity), project(parity))


def _front(x2d, g_row, w_t, w_gate, cw_x, cw_bc, cb_x, cb_bc, dtb_row, alog_row, ee, shift,
           dsk_row, nw_row):
    last = FRONT_TILES - 1
    cur = lambda i: (jnp.minimum(i, last), 0)
    prev = lambda i: (jnp.maximum(i - 1, 0), 0)
    full = lambda shape: pl.BlockSpec(shape, lambda i: (0, 0))
    return pl.pallas_call(
        _front_kernel,
        out_shape=(jax.ShapeDtypeStruct((TOKENS, QKV_W), BF16),
                   jax.ShapeDtypeStruct((TOKENS, LANES), F32),
                   jax.ShapeDtypeStruct((TOKENS, SSM_D_INNER), BF16)),
        grid=(FRONT_TILES + 1,),
        in_specs=[
            pl.BlockSpec((FRONT_TM, D_MODEL), cur),
            full((1, D_MODEL)),
            pl.BlockSpec((W_ROWS, D_MODEL), lambda i: (0, 0), pipeline_mode=pl.Buffered(1)),
            full((D_MODEL, LANES)),
            full((CONV_WIDTH, SSM_D_INNER)), full((CONV_WIDTH, GROUP_W)),
            full((1, SSM_D_INNER)), full((1, GROUP_W)),
            full((1, LANES)), full((1, LANES)),
            full((2 * LANES, SSM_D_INNER)),
            full(((CONV_WIDTH - 1) * CHUNK, CHUNK + CONV_TAIL)),
            full((1, SSM_D_INNER)), full((1, SSM_D_INNER)),
        ],
        out_specs=(pl.BlockSpec((FRONT_TM, QKV_W), cur),
                   pl.BlockSpec((FRONT_TM, LANES), cur),
                   pl.BlockSpec((FRONT_TM, SSM_D_INNER), prev)),
        scratch_shapes=[
            pltpu.VMEM((2, FRONT_TM, SSM_D_INNER), BF16),
            pltpu.VMEM((2, FRONT_TM, SSM_D_INNER), BF16),
            pltpu.VMEM((2, FRONT_TM, GROUP_W), BF16),
            pltpu.VMEM((2, FRONT_TM, LANES), F32),
            pltpu.VMEM((CONV_TAIL, SSM_D_INNER), BF16),
            pltpu.VMEM((CONV_TAIL, GROUP_W), BF16),
            *[pltpu.VMEM((SSM_STATE, GROUP_W), F32) for _ in range(SSM_GROUPS)],
        ],
        compiler_params=pltpu.CompilerParams(
            dimension_semantics=("arbitrary",), vmem_limit_bytes=VMEM_LIMIT),
        name="front",
    )(x2d, g_row, w_t, w_gate, cw_x, cw_bc, cb_x, cb_bc, dtb_row, alog_row, ee, shift, dsk_row,
      nw_row)


ATT_T = 256
ATT_K = 2 * LANES
ONES_ROWS = 16
VT_HEAD_ROWS = ATTN_HEAD_DIM + ONES_ROWS


def _attn_kernel(q_ref, k_ref, v_ref, aux_ref, gq_ref, gk_ref, o_ref, ka_ref, vt_ref, qt_ref, s_ref,
                 m_ref):
    nblk = SEQ // ATT_T
    pair = pl.program_id(1)
    lane = lax.broadcasted_iota(jnp.int32, (ATT_T, LANES), 1)
    sub = lane & (AUX_PAIR_LANES - 1)
    in_pair = (lane >> 4) == pair
    is_val = in_pair & (sub < 6)
    is_neg = in_pair & (sub >= 6) & (sub < 12)
    scale = ATTN_HEAD_DIM ** -0.5 * LOG2E
    krow = lax.broadcasted_iota(jnp.int32, (ATT_T, ATT_T), 0)
    qcol = lax.broadcasted_iota(jnp.int32, (ATT_T, ATT_T), 1)
    causal_t = qcol >= krow
    trow = lax.broadcasted_iota(jnp.int32, (LANES, ATT_T), 0)
    tsub = trow & (AUX_PAIR_LANES - 1)
    t_in_pair = (trow >> 4) == pair
    t_val = t_in_pair & (tsub < 6)
    t_neg = t_in_pair & (tsub >= 6) & (tsub < 12)

    def head_norm_t(u_t, g_t):
        sq = u_t * u_t
        halves = []
        for j in range(2):
            hrows = slice(j * ATTN_HEAD_DIM, (j + 1) * ATTN_HEAD_DIM)
            ms = jnp.sum(sq[hrows, :], axis=0, keepdims=True) * (1.0 / ATTN_HEAD_DIM)
            halves.append(u_t[hrows, :] * lax.rsqrt(ms + EPS))
        return jnp.concatenate(halves, axis=0) * g_t

    for j in range(2):
        vt_ref[j * VT_HEAD_ROWS + ATTN_HEAD_DIM:(j + 1) * VT_HEAD_ROWS, :] = jnp.ones(
            (ONES_ROWS, SEQ), BF16)

    def prepare(i):
        rows = slice(i * ATT_T, (i + 1) * ATT_T)
        qn_t = head_norm_t(q_ref[rows, :].astype(F32).T, gq_ref[...] * scale)
        kn = head_norm_t(k_ref[rows, :].astype(F32).T, gk_ref[...]).T
        aux = aux_ref[rows, :].astype(F32)
        aux_t = aux.T
        k_aux = jnp.where(is_neg, aux, jnp.where(is_val, 1.0, 0.0))
        ka_ref[rows, :] = jnp.concatenate([kn, k_aux], axis=-1).astype(BF16)
        v_t = v_ref[rows, :].astype(F32).T.astype(BF16)
        for j in range(2):
            vt_ref[j * VT_HEAD_ROWS:j * VT_HEAD_ROWS + ATTN_HEAD_DIM, rows] = (
                v_t[j * ATTN_HEAD_DIM:(j + 1) * ATTN_HEAD_DIM, :])
        for j in range(2):
            mine = (tsub & 1) == j
            q_aux_t = jnp.where(t_val & mine, aux_t, jnp.where(t_neg & mine, 1.0, 0.0))
            q_main_t = jnp.where((trow >> 6) == j, qn_t, 0.0)
            qt_ref[i, j] = jnp.concatenate([q_main_t, q_aux_t], axis=0).astype(BF16)
        yield

    def scores(i):
        for j in range(2):
            qa_t = qt_ref[i, j]
            tile_max = None
            for t in range(i + 1):
                s = _dot(ka_ref[t * ATT_T:(t + 1) * ATT_T, :], qa_t)
                if t == i:
                    s = jnp.where(causal_t, s, NEG_BIG)
                s_ref[i % 2, j, t] = s
                tile_max = s if tile_max is None else jnp.maximum(tile_max, s)
                yield
            m = jnp.max(tile_max, axis=0, keepdims=True)
            m_ref[i % 2, j] = jnp.broadcast_to(m, (8, ATT_T))

    def softmax_values(i):
        outs = []
        for j in range(2):
            m = m_ref[i % 2, j][0:1, :]
            acc = None
            for t in range(i + 1):
                pv = _dot(vt_ref[j * VT_HEAD_ROWS:(j + 1) * VT_HEAD_ROWS, t * ATT_T:(t + 1) * ATT_T],
                          jnp.exp2(s_ref[i % 2, j, t] - m).astype(BF16))
                acc = pv if acc is None else acc + pv
                yield
            outs.append(acc[:ATTN_HEAD_DIM, :] / acc[ATTN_HEAD_DIM:ATTN_HEAD_DIM + 1, :])
        o_t = jnp.concatenate(outs, axis=0)
        o_ref[i * ATT_T:(i + 1) * ATT_T, :] = o_t.T.astype(BF16)

    def chain(*gens):
        for g in gens:
            yield from g

    last = nblk - 1
    _interleave(chain(prepare(last), prepare(0)))
    _interleave(scores(last), chain(*[prepare(t) for t in range(1, last)]))
    for i in range(last, -1, -1):
        _interleave(softmax_values(i), *([scores(i - 1)] if i > 0 else []))


def _attention(qkv, aux, gq_row, gk_row):
    qcol = ATTN_WIDTH // LANES
    return pl.pallas_call(
        _attn_kernel,
        out_shape=jax.ShapeDtypeStruct((TOKENS, ATTN_WIDTH), BF16),
        grid=(BATCH, HEAD_PAIRS),
        in_specs=[
            pl.BlockSpec((SEQ, LANES), lambda b, hp: (b, hp)),
            pl.BlockSpec((SEQ, LANES), lambda b, hp: (b, qcol + hp)),
            pl.BlockSpec((SEQ, LANES), lambda b, hp: (b, 2 * qcol + hp)),
            pl.BlockSpec((SEQ, LANES), lambda b, hp: (b, 0)),
            pl.BlockSpec((LANES, ATT_T), lambda b, hp: (0, 0)),
            pl.BlockSpec((LANES, ATT_T), lambda b, hp: (0, 0)),
        ],
        out_specs=pl.BlockSpec((SEQ, LANES), lambda b, hp: (b, hp)),
        scratch_shapes=[
            pltpu.VMEM((SEQ, ATT_K), BF16),
            pltpu.VMEM((2 * VT_HEAD_ROWS, SEQ), BF16),
            pltpu.VMEM((SEQ // ATT_T, 2, ATT_K, ATT_T), BF16),
            pltpu.VMEM((2, 2, SEQ // ATT_T, ATT_T, ATT_T), F32),
            pltpu.VMEM((2, 2, 8, ATT_T), F32),
        ],
        compiler_params=pltpu.CompilerParams(
            dimension_semantics=("parallel", "parallel"),
            vmem_limit_bytes=VMEM_LIMIT),
        name="fox_attention",
    )(qkv, qkv, qkv, aux, gq_row, gk_row)


OUT_TN = 256


def _outproj_kernel(y_ref, o_ref, x_ref, w_ref, out_ref):
    y = y_ref[...]
    o = o_ref[...]
    for n in range(0, D_MODEL, OUT_TN):
        cols = slice(n, n + OUT_TN)
        out_ref[:, cols] = (x_ref[:, cols]
                            + _dot(y, w_ref[:SSM_D_INNER, cols].astype(BF16))
                            + _dot(o, w_ref[SSM_D_INNER:, cols].astype(BF16)))


def _outproj(y, o, x2d, w_out, tm=1024):
    return pl.pallas_call(
        _outproj_kernel,
        out_shape=jax.ShapeDtypeStruct((TOKENS, D_MODEL), F32),
        grid=(TOKENS // tm,),
        in_specs=[
            pl.BlockSpec((tm, SSM_D_INNER), lambda i: (i, 0)),
            pl.BlockSpec((tm, ATTN_WIDTH), lambda i: (i, 0)),
            pl.BlockSpec((tm, D_MODEL), lambda i: (i, 0)),
            pl.BlockSpec((SSM_D_INNER + ATTN_WIDTH, D_MODEL), lambda i: (0, 0),
                         pipeline_mode=pl.Buffered(1)),
        ],
        out_specs=pl.BlockSpec((tm, D_MODEL), lambda i: (i, 0)),
        compiler_params=pltpu.CompilerParams(
            dimension_semantics=("parallel",), vmem_limit_bytes=VMEM_LIMIT),
        name="out_proj",
    )(y, o, x2d, w_out)


def _memkv_kernel(m_ref, g_ref, w_ref, gk_ref, knt_ref, v_ref):
    h = _rms(m_ref[...], g_ref[...]).astype(BF16)
    for a in range(XATTN_HEADS):
        cols = slice(a * XATTN_HEAD_DIM, (a + 1) * XATTN_HEAD_DIM)
        kn = _rms(_dot(h, w_ref[:, cols].astype(BF16)), gk_ref[...])
        knt_ref[cols, :] = kn.T.astype(BF16)
    for n in range(0, D_MODEL, XATTN_HEAD_DIM):
        v_ref[:, n:n + XATTN_HEAD_DIM] = _dot(
            h, w_ref[:, D_MODEL + n:D_MODEL + n + XATTN_HEAD_DIM].astype(BF16)).astype(BF16)


def _memkv(mem2d, g_row, wkv, gk_row):
    return pl.pallas_call(
        _memkv_kernel,
        out_shape=(jax.ShapeDtypeStruct((BATCH * D_MODEL, MEM_LEN), BF16),
                   jax.ShapeDtypeStruct((BATCH * MEM_LEN, D_MODEL), BF16)),
        grid=(BATCH,),
        in_specs=[
            pl.BlockSpec((MEM_LEN, D_MODEL), lambda b: (b, 0)),
            pl.BlockSpec((1, D_MODEL), lambda b: (0, 0)),
            pl.BlockSpec((D_MODEL, 2 * D_MODEL), lambda b: (0, 0), pipeline_mode=pl.Buffered(1)),
            pl.BlockSpec((1, XATTN_HEAD_DIM), lambda b: (0, 0)),
        ],
        out_specs=(pl.BlockSpec((D_MODEL, MEM_LEN), lambda b: (b, 0)),
                   pl.BlockSpec((MEM_LEN, D_MODEL), lambda b: (b, 0))),
        compiler_params=pltpu.CompilerParams(
            dimension_semantics=("parallel",), vmem_limit_bytes=VMEM_LIMIT),
        name="mem_kv",
    )(mem2d, g_row, wkv, gk_row)


XATTN_SUB = 512


def _xattn_kernel(x_ref, g_ref, wq_ref, knt_ref, v_ref, gq_ref, wo_ref, out_ref, q_scr, o_scr):
    nsub = x_ref.shape[0] // XATTN_SUB
    scale = XATTN_HEAD_DIM ** -0.5 * LOG2E
    head_cols = [slice(a * XATTN_HEAD_DIM, (a + 1) * XATTN_HEAD_DIM) for a in range(XATTN_HEADS)]

    def project(k):
        rows = slice(k * XATTN_SUB, (k + 1) * XATTN_SUB)
        h = _rms(x_ref[rows, :], g_ref[...]).astype(BF16)
        for cols in head_cols:
            q_scr[k % 2, :, cols] = _dot(h, wq_ref[:, cols])
            yield

    def attend(k):
        for cols in head_cols:
            qn = (_rms(q_scr[k % 2, :, cols], gq_ref[...]) * scale).astype(BF16)
            s = _dot(qn, knt_ref[cols, :])
            e = jnp.exp2(s - jnp.max(s, axis=-1, keepdims=True))
            p = e / jnp.sum(e, axis=-1, keepdims=True)
            o_scr[k % 2, :, cols] = _dot(p.astype(BF16), v_ref[:, cols]).astype(BF16)
            yield

    def output(k):
        rows = slice(k * XATTN_SUB, (k + 1) * XATTN_SUB)
        o = o_scr[k % 2]
        for n in range(0, D_MODEL, OUT_TN):
            cols = slice(n, n + OUT_TN)
            out_ref[rows, cols] = x_ref[rows, cols] + _dot(o, wo_ref[:, cols])
            yield

    for step in range(nsub + 2):
        stage = []
        if 0 <= step - 2 < nsub:
            stage.append(output(step - 2))
        if 0 <= step - 1 < nsub:
            stage.append(attend(step - 1))
        if step < nsub:
            stage.append(project(step))
        _interleave(*stage)


def _xattn(x1, g_row, wq, knt, v, gq_row, wo, tm=SEQ):
    nt = SEQ // tm
    return pl.pallas_call(
        _xattn_kernel,
        out_shape=jax.ShapeDtypeStruct((TOKENS, D_MODEL), F32),
        grid=(BATCH, nt),
        in_specs=[
            pl.BlockSpec((tm, D_MODEL), lambda b, i: (b * nt + i, 0)),
            pl.BlockSpec((1, D_MODEL), lambda b, i: (0, 0)),
            pl.BlockSpec((D_MODEL, D_MODEL), lambda b, i: (0, 0)),
            pl.BlockSpec((D_MODEL, MEM_LEN), lambda b, i: (b, 0)),
            pl.BlockSpec((MEM_LEN, D_MODEL), lambda b, i: (b, 0)),
            pl.BlockSpec((1, XATTN_HEAD_DIM), lambda b, i: (0, 0)),
            pl.BlockSpec((D_MODEL, D_MODEL), lambda b, i: (0, 0)),
        ],
        out_specs=pl.BlockSpec((tm, D_MODEL), lambda b, i: (b * nt + i, 0)),
        scratch_shapes=[pltpu.VMEM((2, XATTN_SUB, D_MODEL), F32),
                        pltpu.VMEM((2, XATTN_SUB, D_MODEL), BF16)],
        compiler_params=pltpu.CompilerParams(
            dimension_semantics=("parallel", "parallel"), vmem_limit_bytes=VMEM_LIMIT),
        name="mem_xattn",
    )(x1, g_row, wq, knt, v, gq_row, wo)


FF_CHUNK = 1024


def _mlp_kernel(x_ref, g_ref, wu_ref, wd_ref, out_ref, h_scr, acc_scr):
    h_scr[...] = _rms(x_ref[...], g_ref[...]).astype(BF16)
    acc_scr[...] = x_ref[...]
    for f in range(0, D_FF, FF_CHUNK):
        u = jnp.maximum(_dot(h_scr[...], wu_ref[:, f:f + FF_CHUNK].astype(BF16)), 0.0)
        acc_scr[...] += _dot((u * u).astype(BF16), wd_ref[f:f + FF_CHUNK, :].astype(BF16))
    out_ref[...] = acc_scr[...]


def _mlp(x2, g_row, wu, wd, tm=512):
    return pl.pallas_call(
        _mlp_kernel,
        out_shape=jax.ShapeDtypeStruct((TOKENS, D_MODEL), F32),
        grid=(TOKENS // tm,),
        in_specs=[
            pl.BlockSpec((tm, D_MODEL), lambda i: (i, 0)),
            pl.BlockSpec((1, D_MODEL), lambda i: (0, 0)),
            pl.BlockSpec((D_MODEL, D_FF), lambda i: (0, 0), pipeline_mode=pl.Buffered(1)),
            pl.BlockSpec((D_FF, D_MODEL), lambda i: (0, 0), pipeline_mode=pl.Buffered(1)),
        ],
        out_specs=pl.BlockSpec((tm, D_MODEL), lambda i: (i, 0)),
        scratch_shapes=[pltpu.VMEM((tm, D_MODEL), BF16), pltpu.VMEM((tm, D_MODEL), F32)],
        compiler_params=pltpu.CompilerParams(
            dimension_semantics=("parallel",), vmem_limit_bytes=VMEM_LIMIT),
        name="relu2_mlp",
    )(x2, g_row, wu, wd)


def _lane_row(vec, offset):
    return jnp.zeros((1, LANES), F32).at[0, offset:offset + vec.shape[0]].set(vec.astype(F32))


def _layer(x2d, mem2d, g_mix, w_in, conv_w, conv_b, dt_bias, a_log, d_skip, ssm_norm_w,
           g_q, g_k, f_bias, w_out, g_xattn, g_mem, xq_w, xkv_w, xg_q, xg_k, xo_w,
           g_mlp, w_up, w_down):
    w_t = jnp.swapaxes(w_in, 0, 1).astype(BF16)
    w_gate = jnp.concatenate(
        [w_in[:, W_F_ROW:W_F_ROW + ATTN_HEADS], w_in[:, W_DT_ROW:W_DT_ROW + SSM_HEADS],
         jnp.zeros((D_MODEL, LANES - ATTN_HEADS - SSM_HEADS), F32)], axis=1).astype(BF16)
    row = lambda v: v.astype(F32).reshape(1, -1)

    qkv, gates_raw, y = _front(
        x2d, row(g_mix), w_t, w_gate,
        conv_w[:, :SSM_D_INNER], conv_w[:, SSM_D_INNER:],
        row(conv_b[:SSM_D_INNER]), row(conv_b[SSM_D_INNER:]),
        _lane_row(dt_bias, DT_LANE), _lane_row(a_log, DT_LANE),
        jnp.asarray(_head_expand_matrix(), BF16), jnp.asarray(_conv_shift_matrix(), BF16),
        row(jnp.repeat(d_skip, SSM_HEAD_DIM)), row(ssm_norm_w))

    fb_t = jnp.broadcast_to(f_bias.astype(F32)[:, None], (GATE_ROWS, CHUNK))
    aux = _gates(gates_raw, fb_t, jnp.asarray(_gate_route_matrix(), BF16))

    gain_t = lambda g: jnp.broadcast_to(jnp.tile(g.astype(F32), 2)[:, None], (LANES, ATT_T))
    o = _attention(qkv, aux, gain_t(g_q), gain_t(g_k))

    x1 = _outproj(y, o, x2d, w_out)

    knt, mem_v = _memkv(mem2d, row(g_mem), xkv_w, row(xg_k))
    x2 = _xattn(x1, row(g_xattn), xq_w.astype(BF16), knt, mem_v, row(xg_q), xo_w.astype(BF16))

    return _mlp(x2, row(g_mlp), w_up, w_down)


def kernel(x, mem, g_mix, w_in, conv_w, conv_b, dt_bias, a_log, d_skip, ssm_norm_w, g_q, g_k,
           f_bias, w_out, g_xattn, g_mem, xq_w, xkv_w, xg_q, xg_k, xo_w, g_mlp, w_up, w_down):
    x2d = x.reshape(TOKENS, D_MODEL)
    mem2d = mem.reshape(BATCH * MEM_LEN, D_MODEL)
    depth = g_mix.shape[0]
    for l in range(depth):
        x2d = _layer(x2d, mem2d, g_mix[l], w_in[l], conv_w[l], conv_b[l], dt_bias[l], a_log[l],
                     d_skip[l], ssm_norm_w[l], g_q[l], g_k[l], f_bias[l], w_out[l], g_xattn[l],
                     g_mem[l], xq_w[l], xkv_w[l], xg_q[l], xg_k[l], xo_w[l], g_mlp[l], w_up[l],
                     w_down[l])
    return x2d.reshape(BATCH, SEQ, D_MODEL)
```

```python
import numpy as np
import jax
import jax.numpy as jnp
from jax import lax
from jax.experimental import pallas as pl
from jax.experimental.pallas import tpu as pltpu

F32 = jnp.float32
BF16 = jnp.bfloat16

D_MODEL = 1024
BATCH = 8
SEQ = 2048
TOKENS = BATCH * SEQ
MEM_LEN = 256
SSM_HEAD_DIM = 64
SSM_HEADS = 16
SSM_D_INNER = 1024
SSM_GROUPS = 2
SSM_STATE = 128
CONV_WIDTH = 4
CHUNK = 128
ATTN_HEAD_DIM = 64
ATTN_HEADS = 16
ATTN_WIDTH = 1024
XATTN_HEADS = 4
XATTN_HEAD_DIM = 256
D_FF = 4096
EPS = 1e-5

LANES = 128
N_MAIN = 5632
HEAD_PAIRS = ATTN_HEADS // 2
GROUP_W = SSM_D_INNER // SSM_GROUPS
DT_LANE = 16
VMEM_LIMIT = 56 * 1024 * 1024
NEG_BIG = -1e30
LOG2E = 1.4426950408889634


def _rms(xf, g_row):
    ms = jnp.mean(xf * xf, axis=-1, keepdims=True)
    return xf * lax.rsqrt(ms + EPS) * g_row


def _split2(a):
    hi = a.astype(BF16)
    mid = (a - hi.astype(F32)).astype(BF16)
    return jnp.concatenate([hi, mid], axis=-1)


def _split3(a):
    hi = a.astype(BF16)
    r1 = a - hi.astype(F32)
    mid = r1.astype(BF16)
    lo = (r1 - mid.astype(F32)).astype(BF16)
    return jnp.concatenate([hi, mid, lo], axis=-1)


def _softplus(x):
    return jnp.maximum(x, 0.0) + jnp.log1p(jnp.exp(-jnp.abs(x)))


def _silu(x):
    half = 0.5 * x
    return half + half * jnp.tanh(half)


def _dot(a, b):
    return jnp.dot(a, b, preferred_element_type=F32)


def _dot_nt(a, b):
    return lax.dot_general(a, b, (((1,), (1,)), ((), ())), preferred_element_type=F32)


def _interleave(*gens):
    live = list(gens)
    while live:
        live = [g for g in live if next(g, StopIteration) is not StopIteration]


AUX_PAIR_LANES = 16
GATE_ROWS = 16


def _gate_route_matrix():
    r = np.zeros((LANES, 3 * GATE_ROWS), np.float32)
    for h in range(ATTN_HEADS):
        for m in range(3):
            base = AUX_PAIR_LANES * (h // 2)
            r[base + 2 * m + (h % 2), GATE_ROWS * m + h] = 1.0
            r[base + 6 + 2 * m + (h % 2), GATE_ROWS * m + h] = -1.0
    return r


def _split3_rows(a):
    hi = a.astype(BF16)
    r1 = a - hi.astype(F32)
    mid = r1.astype(BF16)
    lo = (r1 - mid.astype(F32)).astype(BF16)
    return jnp.concatenate([hi, mid, lo], axis=0)


def _gates_kernel(g_ref, fb_ref, r_ref, aux_ref):
    row = lax.broadcasted_iota(jnp.int32, (CHUNK, CHUNK), 0)
    col = lax.broadcasted_iota(jnp.int32, (CHUNK, CHUNK), 1)
    triu = jnp.where(row <= col, 1.0, 0.0).astype(BF16)
    fb = fb_ref[...]
    rmat = r_ref[...]
    offset = jnp.zeros((GATE_ROWS, CHUNK), F32)
    for blk in range(SEQ // CHUNK):
        rows = slice(blk * CHUNK, (blk + 1) * CHUNK)
        f_t = g_ref[rows, :].T[:GATE_ROWS, :]
        log_f = -_softplus(-(f_t + fb))
        part = _dot(_split3_rows(log_f), triu)
        cum = (part[:GATE_ROWS] + part[GATE_ROWS:2 * GATE_ROWS] + part[2 * GATE_ROWS:]) + offset
        offset = jnp.broadcast_to(cum[:, CHUNK - 1:CHUNK], (GATE_ROWS, CHUNK))
        aux_t = _dot(rmat, _split3_rows(cum * LOG2E))
        aux_ref[rows, :] = aux_t.T.astype(BF16)


def _gates(gates_raw, fb_t, rmat):
    return pl.pallas_call(
        _gates_kernel,
        out_shape=jax.ShapeDtypeStruct((TOKENS, LANES), BF16),
        grid=(BATCH,),
        in_specs=[
            pl.BlockSpec((SEQ, LANES), lambda b: (b, 0)),
            pl.BlockSpec((GATE_ROWS, CHUNK), lambda b: (0, 0)),
            pl.BlockSpec((LANES, 3 * GATE_ROWS), lambda b: (0, 0)),
        ],
        out_specs=pl.BlockSpec((SEQ, LANES), lambda b: (b, 0)),
        compiler_params=pltpu.CompilerParams(dimension_semantics=("parallel",)),
        name="gates",
    )(gates_raw, fb_t, rmat)


def _head_expand_matrix():
    e = np.zeros((2 * LANES, SSM_D_INNER), np.float32)
    for h in range(SSM_HEADS):
        e[DT_LANE + h, h * SSM_HEAD_DIM:(h + 1) * SSM_HEAD_DIM] = 1.0
        e[LANES + DT_LANE + h, h * SSM_HEAD_DIM:(h + 1) * SSM_HEAD_DIM] = 1.0
    return e


CONV_TAIL = 16


def _conv_shift_matrix():
    s = np.zeros(((CONV_WIDTH - 1) * CHUNK, CHUNK + CONV_TAIL), np.float32)
    for k in range(CONV_WIDTH - 1):
        for t in range(CHUNK):
            src = t - 1 - k
            s[k * CHUNK + t, src if src >= 0 else CHUNK + CONV_TAIL + src] = 1.0
    return s


def _ssd_chunk(z_ref, xs_ref, bc_ref, g_ref, cwx_ref, cwb_ref, cbx_ref, cbb_ref,
               dtb_ref, alog_ref, ee_ref, shift_ref, dsk_ref, nw_ref, y_ref,
               xtail, btail, states):
    def conv_silu(u_ref, tail, w_ref, b_ref):
        cur = u_ref[...]
        shifted = _dot(shift_ref[...], jnp.concatenate([cur, tail], axis=0))
        acc = b_ref[...] + w_ref[3:4, :] * cur.astype(F32)
        for k in range(CONV_WIDTH - 1):
            acc = acc + w_ref[2 - k:3 - k, :] * shifted[k * CHUNK:(k + 1) * CHUNK, :]
        return _silu(acc), cur[CHUNK - CONV_TAIL:, :]

    xs, xtail = conv_silu(xs_ref, xtail, cwx_ref, cbx_ref)
    yield
    bc, btail = conv_silu(bc_ref, btail, cwb_ref, cbb_ref)

    dt = _softplus(g_ref[...] + dtb_ref[...])
    da = dt * (-jnp.exp(alog_ref[...]))
    row = lax.broadcasted_iota(jnp.int32, (CHUNK, CHUNK), 0)
    col = lax.broadcasted_iota(jnp.int32, (CHUNK, CHUNK), 1)
    causal = row >= col
    tril = jnp.where(causal, 1.0, 0.0).astype(BF16)
    part = _dot(tril, _split3(da))
    acs = part[:, :LANES] + part[:, LANES:2 * LANES] + part[:, 2 * LANES:]
    a_last = acs[CHUNK - 1:CHUNK, :]
    exp_a = jnp.exp(acs)
    dt_decay = dt * jnp.exp(a_last - acs)
    ee = ee_ref[...]
    e_dtdec = _dot(_split2(dt_decay), ee)
    e_expa = _dot(_split2(exp_a), ee)
    acs_t = acs.T
    dt_t = dt.T
    yield

    xs_b = xs.astype(BF16)
    xdec_b = (xs * e_dtdec).astype(BF16)
    lane = lax.broadcasted_iota(jnp.int32, (CHUNK, LANES), 1)
    first_head = lane < SSM_HEAD_DIM

    y_parts = []
    new_states = []
    for g in range(SSM_GROUPS):
        b_g = bc[:, g * SSM_STATE:(g + 1) * SSM_STATE]
        c_g = bc[:, (SSM_GROUPS + g) * SSM_STATE:(SSM_GROUPS + g + 1) * SSM_STATE]
        c_gb = c_g.astype(BF16)
        cb = _dot_nt(c_gb, b_g.astype(BF16))
        state = states[g]
        cols = slice(g * GROUP_W, (g + 1) * GROUP_W)
        y_off = _dot(c_gb, state.astype(BF16)) * e_expa[:, cols]
        st_new = _dot(b_g.T.astype(BF16), xdec_b[:, cols])
        new_states.append(state * e_expa[CHUNK - 1:CHUNK, cols] + st_new)
        diag = []
        for pair in range(GROUP_W // LANES):
            x_pair = xs_b[:, g * GROUP_W + pair * LANES:g * GROUP_W + (pair + 1) * LANES]
            res = []
            for j in range(2):
                h = g * (SSM_HEADS // SSM_GROUPS) + 2 * pair + j
                hl = DT_LANE + h
                a_col = jnp.broadcast_to(acs[:, hl:hl + 1], (CHUNK, CHUNK))
                seg = jnp.where(causal, a_col - acs_t[hl:hl + 1, :], -jnp.inf)
                m_h = cb * jnp.exp(seg) * dt_t[hl:hl + 1, :]
                res.append(_dot(m_h.astype(BF16), x_pair))
            diag.append(jnp.where(first_head, res[0], res[1]))
            if pair % 2 == 1:
                yield
        y_parts.append(jnp.concatenate(diag, axis=-1) + y_off)
    y = jnp.concatenate(y_parts, axis=-1) + dsk_ref[...] * xs
    y = y * _silu(z_ref[...].astype(F32))
    normed = []
    for g in range(SSM_GROUPS):
        y_g = y[:, g * GROUP_W:(g + 1) * GROUP_W]
        normed.append(y_g * lax.rsqrt(jnp.mean(y_g * y_g, axis=-1, keepdims=True) + EPS))
    y_ref[...] = (jnp.concatenate(normed, axis=-1) * nw_ref[...]).astype(BF16)
    return xtail, btail, new_states


FRONT_TM = 512
FRONT_TILES = TOKENS // FRONT_TM
FRONT_CHUNKS = FRONT_TM // CHUNK
TILES_PER_SEQ = SEQ // FRONT_TM
QKV_W = 3 * ATTN_WIDTH
Z_END = SSM_D_INNER
QKV_END = Z_END + QKV_W
XS_END = QKV_END + SSM_D_INNER
FRONT_TN = 256


def _front_kernel(x_ref, g_ref, w_ref, wg_ref, cwx_ref, cwb_ref, cbx_ref, cbb_ref, dtb_ref, alog_ref,
                  ee_ref, shift_ref, dsk_ref, nw_ref, qkv_ref, gate_ref, y_ref,
                  z_scr, xs_scr, bc_scr, g_scr, xtail_ref, btail_ref, *state_refs):
    i = pl.program_id(0)
    consts = (cwx_ref, cwb_ref, cbx_ref, cbb_ref, dtb_ref, alog_ref, ee_ref, shift_ref, dsk_ref,
              nw_ref)

    @pl.when(i == 0)
    def _():
        xtail_ref[...] = jnp.zeros_like(xtail_ref)
        btail_ref[...] = jnp.zeros_like(btail_ref)
        for ref in state_refs:
            ref[...] = jnp.zeros_like(ref)

    def project(slot):
        h = _rms(x_ref[...], g_ref[...]).astype(BF16)
        for n in range(0, N_MAIN, FRONT_TN):
            val = _dot(h, w_ref[:, n:n + FRONT_TN]).astype(BF16)
            if n < Z_END:
                z_scr[slot, :, n:n + FRONT_TN] = val
            elif n < QKV_END:
                qkv_ref[:, n - Z_END:n - Z_END + FRONT_TN] = val
            elif n < XS_END:
                xs_scr[slot, :, n - QKV_END:n - QKV_END + FRONT_TN] = val
            else:
                bc_scr[slot, :, n - XS_END:n - XS_END + FRONT_TN] = val
            yield
        gates = _dot(h, wg_ref[...])
        gate_ref[...] = gates
        g_scr[slot] = gates
        yield

    def scan(slot):
        restart = lax.rem(i + (TILES_PER_SEQ - 1), TILES_PER_SEQ) == 0
        xtail = jnp.where(restart, 0.0, xtail_ref[...].astype(F32)).astype(BF16)
        btail = jnp.where(restart, 0.0, btail_ref[...].astype(F32)).astype(BF16)
        states = [jnp.where(restart, 0.0, ref[...]) for ref in state_refs]
        for sc in range(FRONT_CHUNKS):
            rows = pl.ds(sc * CHUNK, CHUNK)
            xtail, btail, states = yield from _ssd_chunk(
                z_scr.at[slot, rows], xs_scr.at[slot, rows], bc_scr.at[slot, rows],
                g_scr.at[slot, rows], *consts, y_ref.at[rows], xtail, btail, states)
            yield
        for ref, state in zip(state_refs, states):
            ref[...] = state
        xtail_ref[...] = xtail
        btail_ref[...] = btail

    for parity in (0, 1):
        @pl.when((i > 0) & (i < FRONT_TILES) & (lax.rem(i, 2) == parity))
        def _(parity=parity):
            _interleave(scan(1 - parity), project(parity))

    @pl.when(i == 0)
    def _():
        _interleave(project(0))

    @pl.when(i == FRONT_TILES)
    def _():
        _interleave(scan(1 - FRONT_TILES % 2))


def _front(x2d, g_row, w_main, w_gate, cw_x, cw_bc, cb_x, cb_bc, dtb_row, alog_row, ee, shift,
           dsk_row, nw_row):
    last = FRONT_TILES - 1
    cur = lambda i: (jnp.minimum(i, last), 0)
    prev = lambda i: (jnp.maximum(i - 1, 0), 0)
    full = lambda shape: pl.BlockSpec(shape, lambda i: (0, 0))
    return pl.pallas_call(
        _front_kernel,
        out_shape=(jax.ShapeDtypeStruct((TOKENS, QKV_W), BF16),
                   jax.ShapeDtypeStruct((TOKENS, LANES), F32),
                   jax.ShapeDtypeStruct((TOKENS, SSM_D_INNER), BF16)),
        grid=(FRONT_TILES + 1,),
        in_specs=[
            pl.BlockSpec((FRONT_TM, D_MODEL), cur),
            full((1, D_MODEL)),
            pl.BlockSpec((D_MODEL, N_MAIN), lambda i: (0, 0), pipeline_mode=pl.Buffered(1)),
            full((D_MODEL, LANES)),
            full((CONV_WIDTH, SSM_D_INNER)), full((CONV_WIDTH, GROUP_W)),
            full((1, SSM_D_INNER)), full((1, GROUP_W)),
            full((1, LANES)), full((1, LANES)),
            full((2 * LANES, SSM_D_INNER)),
            full(((CONV_WIDTH - 1) * CHUNK, CHUNK + CONV_TAIL)),
            full((1, SSM_D_INNER)), full((1, SSM_D_INNER)),
        ],
        out_specs=(pl.BlockSpec((FRONT_TM, QKV_W), cur),
                   pl.BlockSpec((FRONT_TM, LANES), cur),
                   pl.BlockSpec((FRONT_TM, SSM_D_INNER), prev)),
        scratch_shapes=[
            pltpu.VMEM((2, FRONT_TM, SSM_D_INNER), BF16),
            pltpu.VMEM((2, FRONT_TM, SSM_D_INNER), BF16),
            pltpu.VMEM((2, FRONT_TM, GROUP_W), BF16),
            pltpu.VMEM((2, FRONT_TM, LANES), F32),
            pltpu.VMEM((CONV_TAIL, SSM_D_INNER), BF16),
            pltpu.VMEM((CONV_TAIL, GROUP_W), BF16),
            *[pltpu.VMEM((SSM_STATE, GROUP_W), F32) for _ in range(SSM_GROUPS)],
        ],
        compiler_params=pltpu.CompilerParams(
            dimension_semantics=("arbitrary",), vmem_limit_bytes=VMEM_LIMIT),
        name="front",
    )(x2d, g_row, w_main, w_gate, cw_x, cw_bc, cb_x, cb_bc, dtb_row, alog_row, ee, shift, dsk_row,
      nw_row)


ATT_T = 256
ATT_K = 2 * LANES
ONES_ROWS = 16
VT_HEAD_ROWS = ATTN_HEAD_DIM + ONES_ROWS


def _attn_kernel(q_ref, k_ref, v_ref, aux_ref, gq_ref, gk_ref, o_ref, ka_ref, vt_ref, qt_ref, s_ref,
                 m_ref):
    nblk = SEQ // ATT_T
    pair = pl.program_id(1)
    lane = lax.broadcasted_iota(jnp.int32, (ATT_T, LANES), 1)
    sub = lane & (AUX_PAIR_LANES - 1)
    in_pair = (lane >> 4) == pair
    is_val = in_pair & (sub < 6)
    is_neg = in_pair & (sub >= 6) & (sub < 12)
    scale = ATTN_HEAD_DIM ** -0.5 * LOG2E
    krow = lax.broadcasted_iota(jnp.int32, (ATT_T, ATT_T), 0)
    qcol = lax.broadcasted_iota(jnp.int32, (ATT_T, ATT_T), 1)
    causal_t = qcol >= krow
    trow = lax.broadcasted_iota(jnp.int32, (LANES, ATT_T), 0)
    tsub = trow & (AUX_PAIR_LANES - 1)
    t_in_pair = (trow >> 4) == pair
    t_val = t_in_pair & (tsub < 6)
    t_neg = t_in_pair & (tsub >= 6) & (tsub < 12)

    def head_norm_t(u_t, g_t):
        sq = u_t * u_t
        halves = []
        for j in range(2):
            hrows = slice(j * ATTN_HEAD_DIM, (j + 1) * ATTN_HEAD_DIM)
            ms = jnp.sum(sq[hrows, :], axis=0, keepdims=True) * (1.0 / ATTN_HEAD_DIM)
            halves.append(u_t[hrows, :] * lax.rsqrt(ms + EPS))
        return jnp.concatenate(halves, axis=0) * g_t

    for j in range(2):
        vt_ref[j * VT_HEAD_ROWS + ATTN_HEAD_DIM:(j + 1) * VT_HEAD_ROWS, :] = jnp.ones(
            (ONES_ROWS, SEQ), BF16)

    def prepare(i):
        rows = slice(i * ATT_T, (i + 1) * ATT_T)
        qn_t = head_norm_t(q_ref[rows, :].astype(F32).T, gq_ref[...] * scale)
        kn = head_norm_t(k_ref[rows, :].astype(F32).T, gk_ref[...]).T
        aux = aux_ref[rows, :].astype(F32)
        aux_t = aux.T
        k_aux = jnp.where(is_neg, aux, jnp.where(is_val, 1.0, 0.0))
        ka_ref[rows, :] = jnp.concatenate([kn, k_aux], axis=-1).astype(BF16)
        v_t = v_ref[rows, :].astype(F32).T.astype(BF16)
        for j in range(2):
            vt_ref[j * VT_HEAD_ROWS:j * VT_HEAD_ROWS + ATTN_HEAD_DIM, rows] = (
                v_t[j * ATTN_HEAD_DIM:(j + 1) * ATTN_HEAD_DIM, :])
        for j in range(2):
            mine = (tsub & 1) == j
            q_aux_t = jnp.where(t_val & mine, aux_t, jnp.where(t_neg & mine, 1.0, 0.0))
            q_main_t = jnp.where((trow >> 6) == j, qn_t, 0.0)
            qt_ref[i, j] = jnp.concatenate([q_main_t, q_aux_t], axis=0).astype(BF16)
        yield

    def scores(i):
        for j in range(2):
            qa_t = qt_ref[i, j]
            tile_max = None
            for t in range(i + 1):
                s = _dot(ka_ref[t * ATT_T:(t + 1) * ATT_T, :], qa_t)
                if t == i:
                    s = jnp.where(causal_t, s, NEG_BIG)
                s_ref[i % 2, j, t] = s
                tile_max = s if tile_max is None else jnp.maximum(tile_max, s)
                yield
            m = jnp.max(tile_max, axis=0, keepdims=True)
            m_ref[i % 2, j] = jnp.broadcast_to(m, (8, ATT_T))

    def softmax_values(i):
        outs = []
        for j in range(2):
            m = m_ref[i % 2, j][0:1, :]
            acc = None
            for t in range(i + 1):
                pv = _dot(vt_ref[j * VT_HEAD_ROWS:(j + 1) * VT_HEAD_ROWS, t * ATT_T:(t + 1) * ATT_T],
                          jnp.exp2(s_ref[i % 2, j, t] - m).astype(BF16))
                acc = pv if acc is None else acc + pv
                yield
            outs.append(acc[:ATTN_HEAD_DIM, :] / acc[ATTN_HEAD_DIM:ATTN_HEAD_DIM + 1, :])
        o_t = jnp.concatenate(outs, axis=0)
        o_ref[i * ATT_T:(i + 1) * ATT_T, :] = o_t.T.astype(BF16)

    def chain(*gens):
        for g in gens:
            yield from g

    last = nblk - 1
    _interleave(chain(prepare(last), prepare(0)))
    _interleave(scores(last), chain(*[prepare(t) for t in range(1, last)]))
    for i in range(last, -1, -1):
        _interleave(softmax_values(i), *([scores(i - 1)] if i > 0 else []))


def _attention(qkv, aux, gq_row, gk_row):
    qcol = ATTN_WIDTH // LANES
    return pl.pallas_call(
        _attn_kernel,
        out_shape=jax.ShapeDtypeStruct((TOKENS, ATTN_WIDTH), BF16),
        grid=(BATCH, HEAD_PAIRS),
        in_specs=[
            pl.BlockSpec((SEQ, LANES), lambda b, hp: (b, hp)),
            pl.BlockSpec((SEQ, LANES), lambda b, hp: (b, qcol + hp)),
            pl.BlockSpec((SEQ, LANES), lambda b, hp: (b, 2 * qcol + hp)),
            pl.BlockSpec((SEQ, LANES), lambda b, hp: (b, 0)),
            pl.BlockSpec((LANES, ATT_T), lambda b, hp: (0, 0)),
            pl.BlockSpec((LANES, ATT_T), lambda b, hp: (0, 0)),
        ],
        out_specs=pl.BlockSpec((SEQ, LANES), lambda b, hp: (b, hp)),
        scratch_shapes=[
            pltpu.VMEM((SEQ, ATT_K), BF16),
            pltpu.VMEM((2 * VT_HEAD_ROWS, SEQ), BF16),
            pltpu.VMEM((SEQ // ATT_T, 2, ATT_K, ATT_T), BF16),
            pltpu.VMEM((2, 2, SEQ // ATT_T, ATT_T, ATT_T), F32),
            pltpu.VMEM((2, 2, 8, ATT_T), F32),
        ],
        compiler_params=pltpu.CompilerParams(
            dimension_semantics=("parallel", "parallel"),
            vmem_limit_bytes=VMEM_LIMIT),
        name="fox_attention",
    )(qkv, qkv, qkv, aux, gq_row, gk_row)


OUT_TN = 256


def _outproj_kernel(y_ref, o_ref, x_ref, w_ref, out_ref):
    y = y_ref[...]
    o = o_ref[...]
    for n in range(0, D_MODEL, OUT_TN):
        cols = slice(n, n + OUT_TN)
        out_ref[:, cols] = (x_ref[:, cols]
                            + _dot(y, w_ref[:SSM_D_INNER, cols].astype(BF16))
                            + _dot(o, w_ref[SSM_D_INNER:, cols].astype(BF16)))


def _outproj(y, o, x2d, w_out, tm=1024):
    return pl.pallas_call(
        _outproj_kernel,
        out_shape=jax.ShapeDtypeStruct((TOKENS, D_MODEL), F32),
        grid=(TOKENS // tm,),
        in_specs=[
            pl.BlockSpec((tm, SSM_D_INNER), lambda i: (i, 0)),
            pl.BlockSpec((tm, ATTN_WIDTH), lambda i: (i, 0)),
            pl.BlockSpec((tm, D_MODEL), lambda i: (i, 0)),
            pl.BlockSpec((SSM_D_INNER + ATTN_WIDTH, D_MODEL), lambda i: (0, 0),
                         pipeline_mode=pl.Buffered(1)),
        ],
        out_specs=pl.BlockSpec((tm, D_MODEL), lambda i: (i, 0)),
        compiler_params=pltpu.CompilerParams(
            dimension_semantics=("parallel",), vmem_limit_bytes=VMEM_LIMIT),
        name="out_proj",
    )(y, o, x2d, w_out)


def _memkv_kernel(m_ref, g_ref, w_ref, gk_ref, knt_ref, v_ref):
    h = _rms(m_ref[...], g_ref[...]).astype(BF16)
    for a in range(XATTN_HEADS):
        cols = slice(a * XATTN_HEAD_DIM, (a + 1) * XATTN_HEAD_DIM)
        kn = _rms(_dot(h, w_ref[:, cols].astype(BF16)), gk_ref[...])
        knt_ref[cols, :] = kn.T.astype(BF16)
    for n in range(0, D_MODEL, XATTN_HEAD_DIM):
        v_ref[:, n:n + XATTN_HEAD_DIM] = _dot(
            h, w_ref[:, D_MODEL + n:D_MODEL + n + XATTN_HEAD_DIM].astype(BF16)).astype(BF16)


def _memkv(mem2d, g_row, wkv, gk_row):
    return pl.pallas_call(
        _memkv_kernel,
        out_shape=(jax.ShapeDtypeStruct((BATCH * D_MODEL, MEM_LEN), BF16),
                   jax.ShapeDtypeStruct((BATCH * MEM_LEN, D_MODEL), BF16)),
        grid=(BATCH,),
        in_specs=[
            pl.BlockSpec((MEM_LEN, D_MODEL), lambda b: (b, 0)),
            pl.BlockSpec((1, D_MODEL), lambda b: (0, 0)),
            pl.BlockSpec((D_MODEL, 2 * D_MODEL), lambda b: (0, 0), pipeline_mode=pl.Buffered(1)),
            pl.BlockSpec((1, XATTN_HEAD_DIM), lambda b: (0, 0)),
        ],
        out_specs=(pl.BlockSpec((D_MODEL, MEM_LEN), lambda b: (b, 0)),
                   pl.BlockSpec((MEM_LEN, D_MODEL), lambda b: (b, 0))),
        compiler_params=pltpu.CompilerParams(
            dimension_semantics=("parallel",), vmem_limit_bytes=VMEM_LIMIT),
        name="mem_kv",
    )(mem2d, g_row, wkv, gk_row)


XATTN_SUB = 512


def _xattn_kernel(x_ref, g_ref, wq_ref, knt_ref, v_ref, gq_ref, wo_ref, out_ref, q_scr, o_scr):
    nsub = x_ref.shape[0] // XATTN_SUB
    scale = XATTN_HEAD_DIM ** -0.5 * LOG2E
    head_cols = [slice(a * XATTN_HEAD_DIM, (a + 1) * XATTN_HEAD_DIM) for a in range(XATTN_HEADS)]

    def project(k):
        rows = slice(k * XATTN_SUB, (k + 1) * XATTN_SUB)
        h = _rms(x_ref[rows, :], g_ref[...]).astype(BF16)
        for cols in head_cols:
            q_scr[k % 2, :, cols] = _dot(h, wq_ref[:, cols])
            yield

    def attend(k):
        for cols in head_cols:
            qn = (_rms(q_scr[k % 2, :, cols], gq_ref[...]) * scale).astype(BF16)
            s = _dot(qn, knt_ref[cols, :])
            e = jnp.exp2(s - jnp.max(s, axis=-1, keepdims=True))
            p = e / jnp.sum(e, axis=-1, keepdims=True)
            o_scr[k % 2, :, cols] = _dot(p.astype(BF16), v_ref[:, cols]).astype(BF16)
            yield

    def output(k):
        rows = slice(k * XATTN_SUB, (k + 1) * XATTN_SUB)
        o = o_scr[k % 2]
        for n in range(0, D_MODEL, OUT_TN):
            cols = slice(n, n + OUT_TN)
            out_ref[rows, cols] = x_ref[rows, cols] + _dot(o, wo_ref[:, cols])
            yield

    for step in range(nsub + 2):
        stage = []
        if 0 <= step - 2 < nsub:
            stage.append(output(step - 2))
        if 0 <= step - 1 < nsub:
            stage.append(attend(step - 1))
        if step < nsub:
            stage.append(project(step))
        _interleave(*stage)


def _xattn(x1, g_row, wq, knt, v, gq_row, wo, tm=SEQ):
    nt = SEQ // tm
    return pl.pallas_call(
        _xattn_kernel,
        out_shape=jax.ShapeDtypeStruct((TOKENS, D_MODEL), F32),
        grid=(BATCH, nt),
        in_specs=[
            pl.BlockSpec((tm, D_MODEL), lambda b, i: (b * nt + i, 0)),
            pl.BlockSpec((1, D_MODEL), lambda b, i: (0, 0)),
            pl.BlockSpec((D_MODEL, D_MODEL), lambda b, i: (0, 0)),
            pl.BlockSpec((D_MODEL, MEM_LEN), lambda b, i: (b, 0)),
            pl.BlockSpec((MEM_LEN, D_MODEL), lambda b, i: (b, 0)),
            pl.BlockSpec((1, XATTN_HEAD_DIM), lambda b, i: (0, 0)),
            pl.BlockSpec((D_MODEL, D_MODEL), lambda b, i: (0, 0)),
        ],
        out_specs=pl.BlockSpec((tm, D_MODEL), lambda b, i: (b * nt + i, 0)),
        scratch_shapes=[pltpu.VMEM((2, XATTN_SUB, D_MODEL), F32),
                        pltpu.VMEM((2, XATTN_SUB, D_MODEL), BF16)],
        compiler_params=pltpu.CompilerParams(
            dimension_semantics=("parallel", "parallel"), vmem_limit_bytes=VMEM_LIMIT),
        name="mem_xattn",
    )(x1, g_row, wq, knt, v, gq_row, wo)


FF_CHUNK = 1024


def _mlp_kernel(x_ref, g_ref, wu_ref, wd_ref, out_ref, h_scr, acc_scr):
    h_scr[...] = _rms(x_ref[...], g_ref[...]).astype(BF16)
    acc_scr[...] = x_ref[...]
    for f in range(0, D_FF, FF_CHUNK):
        u = jnp.maximum(_dot(h_scr[...], wu_ref[:, f:f + FF_CHUNK].astype(BF16)), 0.0)
        acc_scr[...] += _dot((u * u).astype(BF16), wd_ref[f:f + FF_CHUNK, :].astype(BF16))
    out_ref[...] = acc_scr[...]


def _mlp(x2, g_row, wu, wd, tm=512):
    return pl.pallas_call(
        _mlp_kernel,
        out_shape=jax.ShapeDtypeStruct((TOKENS, D_MODEL), F32),
        grid=(TOKENS // tm,),
        in_specs=[
            pl.BlockSpec((tm, D_MODEL), lambda i: (i, 0)),
            pl.BlockSpec((1, D_MODEL), lambda i: (0, 0)),
            pl.BlockSpec((D_MODEL, D_FF), lambda i: (0, 0), pipeline_mode=pl.Buffered(1)),
            pl.BlockSpec((D_FF, D_MODEL), lambda i: (0, 0), pipeline_mode=pl.Buffered(1)),
        ],
        out_specs=pl.BlockSpec((tm, D_MODEL), lambda i: (i, 0)),
        scratch_shapes=[pltpu.VMEM((tm, D_MODEL), BF16), pltpu.VMEM((tm, D_MODEL), F32)],
        compiler_params=pltpu.CompilerParams(
            dimension_semantics=("parallel",), vmem_limit_bytes=VMEM_LIMIT),
        name="relu2_mlp",
    )(x2, g_row, wu, wd)


def _lane_row(vec, offset):
    return jnp.zeros((1, LANES), F32).at[0, offset:offset + vec.shape[0]].set(vec.astype(F32))


def _layer(x2d, mem2d, g_mix, w_in, conv_w, conv_b, dt_bias, a_log, d_skip, ssm_norm_w,
           g_q, g_k, f_bias, w_out, g_xattn, g_mem, xq_w, xkv_w, xg_q, xg_k, xo_w,
           g_mlp, w_up, w_down):
    z0, xbc0 = 0, SSM_D_INNER
    bc0 = xbc0 + SSM_D_INNER
    dt0 = bc0 + 2 * SSM_GROUPS * SSM_STATE
    q0 = dt0 + SSM_HEADS
    k0 = q0 + ATTN_WIDTH
    v0 = k0 + ATTN_WIDTH
    f0 = v0 + ATTN_WIDTH
    w_main = jnp.concatenate(
        [w_in[:, z0:xbc0], w_in[:, q0:f0], w_in[:, xbc0:dt0]], axis=1).astype(BF16)
    w_gate = jnp.concatenate(
        [w_in[:, f0:f0 + ATTN_HEADS], w_in[:, dt0:q0],
         jnp.zeros((D_MODEL, LANES - ATTN_HEADS - SSM_HEADS), F32)], axis=1).astype(BF16)
    row = lambda v: v.astype(F32).reshape(1, -1)

    qkv, gates_raw, y = _front(
        x2d, row(g_mix), w_main, w_gate,
        conv_w[:, :SSM_D_INNER], conv_w[:, SSM_D_INNER:],
        row(conv_b[:SSM_D_INNER]), row(conv_b[SSM_D_INNER:]),
        _lane_row(dt_bias, DT_LANE), _lane_row(a_log, DT_LANE),
        jnp.asarray(_head_expand_matrix(), BF16), jnp.asarray(_conv_shift_matrix(), BF16),
        row(jnp.repeat(d_skip, SSM_HEAD_DIM)), row(ssm_norm_w))

    fb_t = jnp.broadcast_to(f_bias.astype(F32)[:, None], (GATE_ROWS, CHUNK))
    aux = _gates(gates_raw, fb_t, jnp.asarray(_gate_route_matrix(), BF16))

    gain_t = lambda g: jnp.broadcast_to(jnp.tile(g.astype(F32), 2)[:, None], (LANES, ATT_T))
    o = _attention(qkv, aux, gain_t(g_q), gain_t(g_k))

    x1 = _outproj(y, o, x2d, w_out)

    knt, mem_v = _memkv(mem2d, row(g_mem), xkv_w, row(xg_k))
    x2 = _xattn(x1, row(g_xattn), xq_w.astype(BF16), knt, mem_v, row(xg_q), xo_w.astype(BF16))

    return _mlp(x2, row(g_mlp), w_up, w_down)


def kernel(x, mem, g_mix, w_in, conv_w, conv_b, dt_bias, a_log, d_skip, ssm_norm_w, g_q, g_k,
           f_bias, w_out, g_xattn, g_mem, xq_w, xkv_w, xg_q, xg_k, xo_w, g_mlp, w_up, w_down):
    x2d = x.reshape(TOKENS, D_MODEL)
    mem2d = mem.reshape(BATCH * MEM_LEN, D_MODEL)
    depth = g_mix.shape[0]
    for l in range(depth):
        x2d = _layer(x2d, mem2d, g_mix[l], w_in[l], conv_w[l], conv_b[l], dt_bias[l], a_log[l],
                     d_skip[l], ssm_norm_w[l], g_q[l], g_k[l], f_bias[l], w_out[l], g_xattn[l],
                     g_mem[l], xq_w[l], xkv_w[l], xg_q[l], xg_k[l], xo_w[l], g_mlp[l], w_up[l],
                     w_down[l])
    return x2d.reshape(BATCH, SEQ, D_MODEL)
```

```python
import numpy as np
import jax
import jax.numpy as jnp
from jax import lax
from jax.experimental import pallas as pl
from jax.experimental.pallas import tpu as pltpu

F32 = jnp.float32
BF16 = jnp.bfloat16

D_MODEL = 1024
BATCH = 8
SEQ = 2048
TOKENS = BATCH * SEQ
MEM_LEN = 256
SSM_HEAD_DIM = 64
SSM_HEADS = 16
SSM_D_INNER = 1024
SSM_GROUPS = 2
SSM_STATE = 128
CONV_WIDTH = 4
CHUNK = 128
ATTN_HEAD_DIM = 64
ATTN_HEADS = 16
ATTN_WIDTH = 1024
XATTN_HEADS = 4
XATTN_HEAD_DIM = 256
D_FF = 4096
EPS = 1e-5

LANES = 128
N_MAIN = 5632
HEAD_PAIRS = ATTN_HEADS // 2
GROUP_W = SSM_D_INNER // SSM_GROUPS
DT_LANE = 16
VMEM_LIMIT = 56 * 1024 * 1024
NEG_BIG = -1e30
LOG2E = 1.4426950408889634


def _rms(xf, g_row):
    ms = jnp.mean(xf * xf, axis=-1, keepdims=True)
    return xf * lax.rsqrt(ms + EPS) * g_row


def _split2(a):
    hi = a.astype(BF16)
    mid = (a - hi.astype(F32)).astype(BF16)
    return jnp.concatenate([hi, mid], axis=-1)


def _split3(a):
    hi = a.astype(BF16)
    r1 = a - hi.astype(F32)
    mid = r1.astype(BF16)
    lo = (r1 - mid.astype(F32)).astype(BF16)
    return jnp.concatenate([hi, mid, lo], axis=-1)


def _softplus(x):
    return jnp.maximum(x, 0.0) + jnp.log1p(jnp.exp(-jnp.abs(x)))


def _silu(x):
    half = 0.5 * x
    return half + half * jnp.tanh(half)


def _dot(a, b):
    return jnp.dot(a, b, preferred_element_type=F32)


def _dot_nt(a, b):
    return lax.dot_general(a, b, (((1,), (1,)), ((), ())), preferred_element_type=F32)


def _interleave(*gens):
    live = list(gens)
    while live:
        live = [g for g in live if next(g, StopIteration) is not StopIteration]


AUX_PAIR_LANES = 16
GATE_ROWS = 16


def _gate_route_matrix():
    r = np.zeros((LANES, 3 * GATE_ROWS), np.float32)
    for h in range(ATTN_HEADS):
        for m in range(3):
            base = AUX_PAIR_LANES * (h // 2)
            r[base + 2 * m + (h % 2), GATE_ROWS * m + h] = 1.0
            r[base + 6 + 2 * m + (h % 2), GATE_ROWS * m + h] = -1.0
    return r


def _split3_rows(a):
    hi = a.astype(BF16)
    r1 = a - hi.astype(F32)
    mid = r1.astype(BF16)
    lo = (r1 - mid.astype(F32)).astype(BF16)
    return jnp.concatenate([hi, mid, lo], axis=0)


def _gates_kernel(g_ref, fb_ref, r_ref, aux_ref):
    row = lax.broadcasted_iota(jnp.int32, (CHUNK, CHUNK), 0)
    col = lax.broadcasted_iota(jnp.int32, (CHUNK, CHUNK), 1)
    triu = jnp.where(row <= col, 1.0, 0.0).astype(BF16)
    fb = fb_ref[...]
    rmat = r_ref[...]
    offset = jnp.zeros((GATE_ROWS, CHUNK), F32)
    for blk in range(SEQ // CHUNK):
        rows = slice(blk * CHUNK, (blk + 1) * CHUNK)
        f_t = g_ref[rows, :].T[:GATE_ROWS, :]
        log_f = -_softplus(-(f_t + fb))
        part = _dot(_split3_rows(log_f), triu)
        cum = (part[:GATE_ROWS] + part[GATE_ROWS:2 * GATE_ROWS] + part[2 * GATE_ROWS:]) + offset
        offset = jnp.broadcast_to(cum[:, CHUNK - 1:CHUNK], (GATE_ROWS, CHUNK))
        aux_t = _dot(rmat, _split3_rows(cum * LOG2E))
        aux_ref[rows, :] = aux_t.T.astype(BF16)


def _gates(gates_raw, fb_t, rmat):
    return pl.pallas_call(
        _gates_kernel,
        out_shape=jax.ShapeDtypeStruct((TOKENS, LANES), BF16),
        grid=(BATCH,),
        in_specs=[
            pl.BlockSpec((SEQ, LANES), lambda b: (b, 0)),
            pl.BlockSpec((GATE_ROWS, CHUNK), lambda b: (0, 0)),
            pl.BlockSpec((LANES, 3 * GATE_ROWS), lambda b: (0, 0)),
        ],
        out_specs=pl.BlockSpec((SEQ, LANES), lambda b: (b, 0)),
        compiler_params=pltpu.CompilerParams(dimension_semantics=("parallel",)),
        name="gates",
    )(gates_raw, fb_t, rmat)


def _head_expand_matrix():
    e = np.zeros((2 * LANES, SSM_D_INNER), np.float32)
    for h in range(SSM_HEADS):
        e[DT_LANE + h, h * SSM_HEAD_DIM:(h + 1) * SSM_HEAD_DIM] = 1.0
        e[LANES + DT_LANE + h, h * SSM_HEAD_DIM:(h + 1) * SSM_HEAD_DIM] = 1.0
    return e


CONV_TAIL = 16


def _conv_shift_matrix():
    s = np.zeros(((CONV_WIDTH - 1) * CHUNK, CHUNK + CONV_TAIL), np.float32)
    for k in range(CONV_WIDTH - 1):
        for t in range(CHUNK):
            src = t - 1 - k
            s[k * CHUNK + t, src if src >= 0 else CHUNK + CONV_TAIL + src] = 1.0
    return s


def _ssd_chunk(z_ref, xs_ref, bc_ref, g_ref, cwx_ref, cwb_ref, cbx_ref, cbb_ref,
               dtb_ref, alog_ref, ee_ref, shift_ref, dsk_ref, nw_ref, y_ref,
               xtail, btail, states):
    def conv_silu(u_ref, tail, w_ref, b_ref):
        cur = u_ref[...]
        shifted = _dot(shift_ref[...], jnp.concatenate([cur, tail], axis=0))
        acc = b_ref[...] + w_ref[3:4, :] * cur.astype(F32)
        for k in range(CONV_WIDTH - 1):
            acc = acc + w_ref[2 - k:3 - k, :] * shifted[k * CHUNK:(k + 1) * CHUNK, :]
        return _silu(acc), cur[CHUNK - CONV_TAIL:, :]

    xs, xtail = conv_silu(xs_ref, xtail, cwx_ref, cbx_ref)
    yield
    bc, btail = conv_silu(bc_ref, btail, cwb_ref, cbb_ref)

    dt = _softplus(g_ref[...] + dtb_ref[...])
    da = dt * (-jnp.exp(alog_ref[...]))
    row = lax.broadcasted_iota(jnp.int32, (CHUNK, CHUNK), 0)
    col = lax.broadcasted_iota(jnp.int32, (CHUNK, CHUNK), 1)
    causal = row >= col
    tril = jnp.where(causal, 1.0, 0.0).astype(BF16)
    part = _dot(tril, _split3(da))
    acs = part[:, :LANES] + part[:, LANES:2 * LANES] + part[:, 2 * LANES:]
    a_last = acs[CHUNK - 1:CHUNK, :]
    exp_a = jnp.exp(acs)
    dt_decay = dt * jnp.exp(a_last - acs)
    ee = ee_ref[...]
    e_dtdec = _dot(_split2(dt_decay), ee)
    e_expa = _dot(_split2(exp_a), ee)
    acs_t = acs.T
    dt_tb = dt.T.astype(BF16)
    yield

    xs_b = xs.astype(BF16)
    xdec_b = (xs * e_dtdec).astype(BF16)
    lane = lax.broadcasted_iota(jnp.int32, (CHUNK, LANES), 1)
    first_head = lane < SSM_HEAD_DIM

    y_parts = []
    new_states = []
    for g in range(SSM_GROUPS):
        b_g = bc[:, g * SSM_STATE:(g + 1) * SSM_STATE]
        c_g = bc[:, (SSM_GROUPS + g) * SSM_STATE:(SSM_GROUPS + g + 1) * SSM_STATE]
        c_gb = c_g.astype(BF16)
        cb_b = _dot_nt(c_gb, b_g.astype(BF16)).astype(BF16)
        state = states[g]
        cols = slice(g * GROUP_W, (g + 1) * GROUP_W)
        y_off = _dot(c_gb, state.astype(BF16)) * e_expa[:, cols]
        st_new = _dot(b_g.T.astype(BF16), xdec_b[:, cols])
        new_states.append(state * e_expa[CHUNK - 1:CHUNK, cols] + st_new)
        diag = []
        for pair in range(GROUP_W // LANES):
            x_pair = xs_b[:, g * GROUP_W + pair * LANES:g * GROUP_W + (pair + 1) * LANES]
            res = []
            for j in range(2):
                h = g * (SSM_HEADS // SSM_GROUPS) + 2 * pair + j
                hl = DT_LANE + h
                a_col = jnp.broadcast_to(acs[:, hl:hl + 1], (CHUNK, CHUNK))
                seg = jnp.where(causal, a_col - acs_t[hl:hl + 1, :], -jnp.inf)
                m_h = jnp.exp(seg).astype(BF16) * cb_b * dt_tb[hl:hl + 1, :]
                res.append(_dot(m_h, x_pair))
            diag.append(jnp.where(first_head, res[0], res[1]))
            if pair % 2 == 1:
                yield
        y_parts.append(jnp.concatenate(diag, axis=-1) + y_off)
    y = jnp.concatenate(y_parts, axis=-1) + dsk_ref[...] * xs
    y = y * _silu(z_ref[...].astype(F32))
    normed = []
    for g in range(SSM_GROUPS):
        y_g = y[:, g * GROUP_W:(g + 1) * GROUP_W]
        normed.append(y_g * lax.rsqrt(jnp.mean(y_g * y_g, axis=-1, keepdims=True) + EPS))
    y_ref[...] = (jnp.concatenate(normed, axis=-1) * nw_ref[...]).astype(BF16)
    return xtail, btail, new_states


FRONT_TM = 512
FRONT_TILES = TOKENS // FRONT_TM
FRONT_CHUNKS = FRONT_TM // CHUNK
TILES_PER_SEQ = SEQ // FRONT_TM
QKV_W = 3 * ATTN_WIDTH
Z_END = SSM_D_INNER
QKV_END = Z_END + QKV_W
XS_END = QKV_END + SSM_D_INNER
FRONT_TN = 256


def _front_kernel(x_ref, g_ref, w_ref, wg_ref, cwx_ref, cwb_ref, cbx_ref, cbb_ref, dtb_ref, alog_ref,
                  ee_ref, shift_ref, dsk_ref, nw_ref, qkv_ref, gate_ref, y_ref,
                  z_scr, xs_scr, bc_scr, g_scr, xtail_ref, btail_ref, *state_refs):
    i = pl.program_id(0)
    consts = (cwx_ref, cwb_ref, cbx_ref, cbb_ref, dtb_ref, alog_ref, ee_ref, shift_ref, dsk_ref,
              nw_ref)

    @pl.when(i == 0)
    def _():
        z_scr[1] = jnp.zeros(z_scr.shape[1:], BF16)
        xs_scr[1] = jnp.zeros(xs_scr.shape[1:], BF16)
        bc_scr[1] = jnp.zeros(bc_scr.shape[1:], BF16)
        g_scr[1] = jnp.zeros(g_scr.shape[1:], F32)
        xtail_ref[...] = jnp.zeros_like(xtail_ref)
        btail_ref[...] = jnp.zeros_like(btail_ref)
        for ref in state_refs:
            ref[...] = jnp.zeros_like(ref)

    def project(slot):
        h = _rms(x_ref[...], g_ref[...]).astype(BF16)
        for n in range(0, N_MAIN, FRONT_TN):
            val = _dot(h, w_ref[:, n:n + FRONT_TN]).astype(BF16)
            if n < Z_END:
                z_scr[slot, :, n:n + FRONT_TN] = val
            elif n < QKV_END:
                qkv_ref[:, n - Z_END:n - Z_END + FRONT_TN] = val
            elif n < XS_END:
                xs_scr[slot, :, n - QKV_END:n - QKV_END + FRONT_TN] = val
            else:
                bc_scr[slot, :, n - XS_END:n - XS_END + FRONT_TN] = val
            yield
        gates = _dot(h, wg_ref[...])
        gate_ref[...] = gates
        g_scr[slot] = gates
        yield

    def scan(slot):
        restart = lax.rem(i + (TILES_PER_SEQ - 1), TILES_PER_SEQ) == 0
        xtail = jnp.where(restart, 0.0, xtail_ref[...].astype(F32)).astype(BF16)
        btail = jnp.where(restart, 0.0, btail_ref[...].astype(F32)).astype(BF16)
        states = [jnp.where(restart, 0.0, ref[...]) for ref in state_refs]
        for sc in range(FRONT_CHUNKS):
            rows = pl.ds(sc * CHUNK, CHUNK)
            xtail, btail, states = yield from _ssd_chunk(
                z_scr.at[slot, rows], xs_scr.at[slot, rows], bc_scr.at[slot, rows],
                g_scr.at[slot, rows], *consts, y_ref.at[rows], xtail, btail, states)
            yield
        for ref, state in zip(state_refs, states):
            ref[...] = state
        xtail_ref[...] = xtail
        btail_ref[...] = btail

    for parity in (0, 1):
        @pl.when(lax.rem(i, 2) == parity)
        def _(parity=parity):
            _interleave(scan(1 - parity), project(parity))


def _front(x2d, g_row, w_main, w_gate, cw_x, cw_bc, cb_x, cb_bc, dtb_row, alog_row, ee, shift,
           dsk_row, nw_row):
    last = FRONT_TILES - 1
    cur = lambda i: (jnp.minimum(i, last), 0)
    prev = lambda i: (jnp.maximum(i - 1, 0), 0)
    full = lambda shape: pl.BlockSpec(shape, lambda i: (0, 0))
    return pl.pallas_call(
        _front_kernel,
        out_shape=(jax.ShapeDtypeStruct((TOKENS, QKV_W), BF16),
                   jax.ShapeDtypeStruct((TOKENS, LANES), F32),
                   jax.ShapeDtypeStruct((TOKENS, SSM_D_INNER), BF16)),
        grid=(FRONT_TILES + 1,),
        in_specs=[
            pl.BlockSpec((FRONT_TM, D_MODEL), cur),
            full((1, D_MODEL)),
            pl.BlockSpec((D_MODEL, N_MAIN), lambda i: (0, 0), pipeline_mode=pl.Buffered(1)),
            full((D_MODEL, LANES)),
            full((CONV_WIDTH, SSM_D_INNER)), full((CONV_WIDTH, GROUP_W)),
            full((1, SSM_D_INNER)), full((1, GROUP_W)),
            full((1, LANES)), full((1, LANES)),
            full((2 * LANES, SSM_D_INNER)),
            full(((CONV_WIDTH - 1) * CHUNK, CHUNK + CONV_TAIL)),
            full((1, SSM_D_INNER)), full((1, SSM_D_INNER)),
        ],
        out_specs=(pl.BlockSpec((FRONT_TM, QKV_W), cur),
                   pl.BlockSpec((FRONT_TM, LANES), cur),
                   pl.BlockSpec((FRONT_TM, SSM_D_INNER), prev)),
        scratch_shapes=[
            pltpu.VMEM((2, FRONT_TM, SSM_D_INNER), BF16),
            pltpu.VMEM((2, FRONT_TM, SSM_D_INNER), BF16),
            pltpu.VMEM((2, FRONT_TM, GROUP_W), BF16),
            pltpu.VMEM((2, FRONT_TM, LANES), F32),
            pltpu.VMEM((CONV_TAIL, SSM_D_INNER), BF16),
            pltpu.VMEM((CONV_TAIL, GROUP_W), BF16),
            *[pltpu.VMEM((SSM_STATE, GROUP_W), F32) for _ in range(SSM_GROUPS)],
        ],
        compiler_params=pltpu.CompilerParams(
            dimension_semantics=("arbitrary",), vmem_limit_bytes=VMEM_LIMIT),
        name="front",
    )(x2d, g_row, w_main, w_gate, cw_x, cw_bc, cb_x, cb_bc, dtb_row, alog_row, ee, shift, dsk_row,
      nw_row)


ATT_T = 256
ATT_K = 2 * LANES
ONES_ROWS = 16
VT_HEAD_ROWS = ATTN_HEAD_DIM + ONES_ROWS


def _attn_kernel(q_ref, k_ref, v_ref, aux_ref, gq_ref, gk_ref, o_ref, ka_ref, vt_ref, qt_ref, s_ref,
                 m_ref):
    nblk = SEQ // ATT_T
    pair = pl.program_id(1)
    lane = lax.broadcasted_iota(jnp.int32, (ATT_T, LANES), 1)
    sub = lane & (AUX_PAIR_LANES - 1)
    in_pair = (lane >> 4) == pair
    is_val = in_pair & (sub < 6)
    is_neg = in_pair & (sub >= 6) & (sub < 12)
    scale = ATTN_HEAD_DIM ** -0.5 * LOG2E
    krow = lax.broadcasted_iota(jnp.int32, (ATT_T, ATT_T), 0)
    qcol = lax.broadcasted_iota(jnp.int32, (ATT_T, ATT_T), 1)
    causal_t = qcol >= krow
    trow = lax.broadcasted_iota(jnp.int32, (LANES, ATT_T), 0)
    tsub = trow & (AUX_PAIR_LANES - 1)
    t_in_pair = (trow >> 4) == pair
    t_val = t_in_pair & (tsub < 6)
    t_neg = t_in_pair & (tsub >= 6) & (tsub < 12)

    def head_norm_t(u_t, g_t):
        sq = u_t * u_t
        halves = []
        for j in range(2):
            hrows = slice(j * ATTN_HEAD_DIM, (j + 1) * ATTN_HEAD_DIM)
            ms = jnp.sum(sq[hrows, :], axis=0, keepdims=True) * (1.0 / ATTN_HEAD_DIM)
            halves.append(u_t[hrows, :] * lax.rsqrt(ms + EPS))
        return jnp.concatenate(halves, axis=0) * g_t

    for j in range(2):
        vt_ref[j * VT_HEAD_ROWS + ATTN_HEAD_DIM:(j + 1) * VT_HEAD_ROWS, :] = jnp.ones(
            (ONES_ROWS, SEQ), BF16)

    def prepare(i):
        rows = slice(i * ATT_T, (i + 1) * ATT_T)
        qn_t = head_norm_t(q_ref[rows, :].astype(F32).T, gq_ref[...] * scale)
        kn = head_norm_t(k_ref[rows, :].astype(F32).T, gk_ref[...]).T
        aux = aux_ref[rows, :].astype(F32)
        aux_t = aux.T
        k_aux = jnp.where(is_neg, aux, jnp.where(is_val, 1.0, 0.0))
        ka_ref[rows, :] = jnp.concatenate([kn, k_aux], axis=-1).astype(BF16)
        v_t = v_ref[rows, :].astype(F32).T.astype(BF16)
        for j in range(2):
            vt_ref[j * VT_HEAD_ROWS:j * VT_HEAD_ROWS + ATTN_HEAD_DIM, rows] = (
                v_t[j * ATTN_HEAD_DIM:(j + 1) * ATTN_HEAD_DIM, :])
        for j in range(2):
            mine = (tsub & 1) == j
            q_aux_t = jnp.where(t_val & mine, aux_t, jnp.where(t_neg & mine, 1.0, 0.0))
            q_main_t = jnp.where((trow >> 6) == j, qn_t, 0.0)
            qt_ref[i, j] = jnp.concatenate([q_main_t, q_aux_t], axis=0).astype(BF16)
        yield

    def scores(i):
        for j in range(2):
            qa_t = qt_ref[i, j]
            tile_max = None
            for t in range(i + 1):
                s = _dot(ka_ref[t * ATT_T:(t + 1) * ATT_T, :], qa_t)
                if t == i:
                    s = jnp.where(causal_t, s, NEG_BIG)
                s_ref[i % 2, j, t] = s
                tile_max = s if tile_max is None else jnp.maximum(tile_max, s)
                yield
            m = jnp.max(tile_max, axis=0, keepdims=True)
            m_ref[i % 2, j] = jnp.broadcast_to(m, (8, ATT_T))

    def softmax_values(i):
        outs = []
        for j in range(2):
            m = m_ref[i % 2, j][0:1, :]
            acc = None
            for t in range(i + 1):
                pv = _dot(vt_ref[j * VT_HEAD_ROWS:(j + 1) * VT_HEAD_ROWS, t * ATT_T:(t + 1) * ATT_T],
                          jnp.exp2(s_ref[i % 2, j, t] - m).astype(BF16))
                acc = pv if acc is None else acc + pv
                yield
            outs.append(acc[:ATTN_HEAD_DIM, :] / acc[ATTN_HEAD_DIM:ATTN_HEAD_DIM + 1, :])
        o_t = jnp.concatenate(outs, axis=0)
        o_ref[i * ATT_T:(i + 1) * ATT_T, :] = o_t.T.astype(BF16)

    def chain(*gens):
        for g in gens:
            yield from g

    last = nblk - 1
    _interleave(chain(prepare(last), prepare(0)))
    _interleave(scores(last), chain(*[prepare(t) for t in range(1, last)]))
    for i in range(last, -1, -1):
        _interleave(softmax_values(i), *([scores(i - 1)] if i > 0 else []))


def _attention(qkv, aux, gq_row, gk_row):
    qcol = ATTN_WIDTH // LANES
    return pl.pallas_call(
        _attn_kernel,
        out_shape=jax.ShapeDtypeStruct((TOKENS, ATTN_WIDTH), BF16),
        grid=(BATCH, HEAD_PAIRS),
        in_specs=[
            pl.BlockSpec((SEQ, LANES), lambda b, hp: (b, hp)),
            pl.BlockSpec((SEQ, LANES), lambda b, hp: (b, qcol + hp)),
            pl.BlockSpec((SEQ, LANES), lambda b, hp: (b, 2 * qcol + hp)),
            pl.BlockSpec((SEQ, LANES), lambda b, hp: (b, 0)),
            pl.BlockSpec((LANES, ATT_T), lambda b, hp: (0, 0)),
            pl.BlockSpec((LANES, ATT_T), lambda b, hp: (0, 0)),
        ],
        out_specs=pl.BlockSpec((SEQ, LANES), lambda b, hp: (b, hp)),
        scratch_shapes=[
            pltpu.VMEM((SEQ, ATT_K), BF16),
            pltpu.VMEM((2 * VT_HEAD_ROWS, SEQ), BF16),
            pltpu.VMEM((SEQ // ATT_T, 2, ATT_K, ATT_T), BF16),
            pltpu.VMEM((2, 2, SEQ // ATT_T, ATT_T, ATT_T), F32),
            pltpu.VMEM((2, 2, 8, ATT_T), F32),
        ],
        compiler_params=pltpu.CompilerParams(
            dimension_semantics=("parallel", "parallel"),
            vmem_limit_bytes=VMEM_LIMIT),
        name="fox_attention",
    )(qkv, qkv, qkv, aux, gq_row, gk_row)


OUT_TN = 256


def _outproj_kernel(y_ref, o_ref, x_ref, w_ref, out_ref):
    y = y_ref[...]
    o = o_ref[...]
    for n in range(0, D_MODEL, OUT_TN):
        cols = slice(n, n + OUT_TN)
        out_ref[:, cols] = (x_ref[:, cols]
                            + _dot(y, w_ref[:SSM_D_INNER, cols].astype(BF16))
                            + _dot(o, w_ref[SSM_D_INNER:, cols].astype(BF16)))


def _outproj(y, o, x2d, w_out, tm=1024):
    return pl.pallas_call(
        _outproj_kernel,
        out_shape=jax.ShapeDtypeStruct((TOKENS, D_MODEL), F32),
        grid=(TOKENS // tm,),
        in_specs=[
            pl.BlockSpec((tm, SSM_D_INNER), lambda i: (i, 0)),
            pl.BlockSpec((tm, ATTN_WIDTH), lambda i: (i, 0)),
            pl.BlockSpec((tm, D_MODEL), lambda i: (i, 0)),
            pl.BlockSpec((SSM_D_INNER + ATTN_WIDTH, D_MODEL), lambda i: (0, 0),
                         pipeline_mode=pl.Buffered(1)),
        ],
        out_specs=pl.BlockSpec((tm, D_MODEL), lambda i: (i, 0)),
        compiler_params=pltpu.CompilerParams(
            dimension_semantics=("parallel",), vmem_limit_bytes=VMEM_LIMIT),
        name="out_proj",
    )(y, o, x2d, w_out)


def _memkv_kernel(m_ref, g_ref, w_ref, gk_ref, knt_ref, v_ref):
    h = _rms(m_ref[...], g_ref[...]).astype(BF16)
    for a in range(XATTN_HEADS):
        cols = slice(a * XATTN_HEAD_DIM, (a + 1) * XATTN_HEAD_DIM)
        kn = _rms(_dot(h, w_ref[:, cols].astype(BF16)), gk_ref[...])
        knt_ref[cols, :] = kn.T.astype(BF16)
    for n in range(0, D_MODEL, XATTN_HEAD_DIM):
        v_ref[:, n:n + XATTN_HEAD_DIM] = _dot(
            h, w_ref[:, D_MODEL + n:D_MODEL + n + XATTN_HEAD_DIM].astype(BF16)).astype(BF16)


def _memkv(mem2d, g_row, wkv, gk_row):
    return pl.pallas_call(
        _memkv_kernel,
        out_shape=(jax.ShapeDtypeStruct((BATCH * D_MODEL, MEM_LEN), BF16),
                   jax.ShapeDtypeStruct((BATCH * MEM_LEN, D_MODEL), BF16)),
        grid=(BATCH,),
        in_specs=[
            pl.BlockSpec((MEM_LEN, D_MODEL), lambda b: (b, 0)),
            pl.BlockSpec((1, D_MODEL), lambda b: (0, 0)),
            pl.BlockSpec((D_MODEL, 2 * D_MODEL), lambda b: (0, 0), pipeline_mode=pl.Buffered(1)),
            pl.BlockSpec((1, XATTN_HEAD_DIM), lambda b: (0, 0)),
        ],
        out_specs=(pl.BlockSpec((D_MODEL, MEM_LEN), lambda b: (b, 0)),
                   pl.BlockSpec((MEM_LEN, D_MODEL), lambda b: (b, 0))),
        compiler_params=pltpu.CompilerParams(
            dimension_semantics=("parallel",), vmem_limit_bytes=VMEM_LIMIT),
        name="mem_kv",
    )(mem2d, g_row, wkv, gk_row)


XATTN_SUB = 512


def _xattn_kernel(x_ref, g_ref, wq_ref, knt_ref, v_ref, gq_ref, wo_ref, out_ref, q_scr, o_scr):
    nsub = x_ref.shape[0] // XATTN_SUB
    scale = XATTN_HEAD_DIM ** -0.5 * LOG2E
    head_cols = [slice(a * XATTN_HEAD_DIM, (a + 1) * XATTN_HEAD_DIM) for a in range(XATTN_HEADS)]

    def project(k):
        rows = slice(k * XATTN_SUB, (k + 1) * XATTN_SUB)
        h = _rms(x_ref[rows, :], g_ref[...]).astype(BF16)
        for cols in head_cols:
            q_scr[k % 2, :, cols] = _dot(h, wq_ref[:, cols])
            yield

    def attend(k):
        for cols in head_cols:
            qn = (_rms(q_scr[k % 2, :, cols], gq_ref[...]) * scale).astype(BF16)
            s = _dot(qn, knt_ref[cols, :])
            e = jnp.exp2(s - jnp.max(s, axis=-1, keepdims=True))
            p = e / jnp.sum(e, axis=-1, keepdims=True)
            o_scr[k % 2, :, cols] = _dot(p.astype(BF16), v_ref[:, cols]).astype(BF16)
            yield

    def output(k):
        rows = slice(k * XATTN_SUB, (k + 1) * XATTN_SUB)
        o = o_scr[k % 2]
        for n in range(0, D_MODEL, OUT_TN):
            cols = slice(n, n + OUT_TN)
            out_ref[rows, cols] = x_ref[rows, cols] + _dot(o, wo_ref[:, cols])
            yield

    for step in range(nsub + 2):
        stage = []
        if 0 <= step - 2 < nsub:
            stage.append(output(step - 2))
        if 0 <= step - 1 < nsub:
            stage.append(attend(step - 1))
        if step < nsub:
            stage.append(project(step))
        _interleave(*stage)


def _xattn(x1, g_row, wq, knt, v, gq_row, wo, tm=SEQ):
    nt = SEQ // tm
    return pl.pallas_call(
        _xattn_kernel,
        out_shape=jax.ShapeDtypeStruct((TOKENS, D_MODEL), F32),
        grid=(BATCH, nt),
        in_specs=[
            pl.BlockSpec((tm, D_MODEL), lambda b, i: (b * nt + i, 0)),
            pl.BlockSpec((1, D_MODEL), lambda b, i: (0, 0)),
            pl.BlockSpec((D_MODEL, D_MODEL), lambda b, i: (0, 0)),
            pl.BlockSpec((D_MODEL, MEM_LEN), lambda b, i: (b, 0)),
            pl.BlockSpec((MEM_LEN, D_MODEL), lambda b, i: (b, 0)),
            pl.BlockSpec((1, XATTN_HEAD_DIM), lambda b, i: (0, 0)),
            pl.BlockSpec((D_MODEL, D_MODEL), lambda b, i: (0, 0)),
        ],
        out_specs=pl.BlockSpec((tm, D_MODEL), lambda b, i: (b * nt + i, 0)),
        scratch_shapes=[pltpu.VMEM((2, XATTN_SUB, D_MODEL), F32),
                        pltpu.VMEM((2, XATTN_SUB, D_MODEL), BF16)],
        compiler_params=pltpu.CompilerParams(
            dimension_semantics=("parallel", "parallel"), vmem_limit_bytes=VMEM_LIMIT),
        name="mem_xattn",
    )(x1, g_row, wq, knt, v, gq_row, wo)


FF_CHUNK = 1024


def _mlp_kernel(x_ref, g_ref, wu_ref, wd_ref, out_ref, h_scr, acc_scr):
    h_scr[...] = _rms(x_ref[...], g_ref[...]).astype(BF16)
    acc_scr[...] = x_ref[...]
    for f in range(0, D_FF, FF_CHUNK):
        u = jnp.maximum(_dot(h_scr[...], wu_ref[:, f:f + FF_CHUNK].astype(BF16)), 0.0)
        acc_scr[...] += _dot((u * u).astype(BF16), wd_ref[f:f + FF_CHUNK, :].astype(BF16))
    out_ref[...] = acc_scr[...]


def _mlp(x2, g_row, wu, wd, tm=512):
    return pl.pallas_call(
        _mlp_kernel,
        out_shape=jax.ShapeDtypeStruct((TOKENS, D_MODEL), F32),
        grid=(TOKENS // tm,),
        in_specs=[
            pl.BlockSpec((tm, D_MODEL), lambda i: (i, 0)),
            pl.BlockSpec((1, D_MODEL), lambda i: (0, 0)),
            pl.BlockSpec((D_MODEL, D_FF), lambda i: (0, 0), pipeline_mode=pl.Buffered(1)),
            pl.BlockSpec((D_FF, D_MODEL), lambda i: (0, 0), pipeline_mode=pl.Buffered(1)),
        ],
        out_specs=pl.BlockSpec((tm, D_MODEL), lambda i: (i, 0)),
        scratch_shapes=[pltpu.VMEM((tm, D_MODEL), BF16), pltpu.VMEM((tm, D_MODEL), F32)],
        compiler_params=pltpu.CompilerParams(
            dimension_semantics=("parallel",), vmem_limit_bytes=VMEM_LIMIT),
        name="relu2_mlp",
    )(x2, g_row, wu, wd)


def _lane_row(vec, offset):
    return jnp.zeros((1, LANES), F32).at[0, offset:offset + vec.shape[0]].set(vec.astype(F32))


def _layer(x2d, mem2d, g_mix, w_in, conv_w, conv_b, dt_bias, a_log, d_skip, ssm_norm_w,
           g_q, g_k, f_bias, w_out, g_xattn, g_mem, xq_w, xkv_w, xg_q, xg_k, xo_w,
           g_mlp, w_up, w_down):
    z0, xbc0 = 0, SSM_D_INNER
    bc0 = xbc0 + SSM_D_INNER
    dt0 = bc0 + 2 * SSM_GROUPS * SSM_STATE
    q0 = dt0 + SSM_HEADS
    k0 = q0 + ATTN_WIDTH
    v0 = k0 + ATTN_WIDTH
    f0 = v0 + ATTN_WIDTH
    w_main = jnp.concatenate(
        [w_in[:, z0:xbc0], w_in[:, q0:f0], w_in[:, xbc0:dt0]], axis=1).astype(BF16)
    w_gate = jnp.concatenate(
        [w_in[:, f0:f0 + ATTN_HEADS], w_in[:, dt0:q0],
         jnp.zeros((D_MODEL, LANES - ATTN_HEADS - SSM_HEADS), F32)], axis=1).astype(BF16)
    row = lambda v: v.astype(F32).reshape(1, -1)

    qkv, gates_raw, y = _front(
        x2d, row(g_mix), w_main, w_gate,
        conv_w[:, :SSM_D_INNER], conv_w[:, SSM_D_INNER:],
        row(conv_b[:SSM_D_INNER]), row(conv_b[SSM_D_INNER:]),
        _lane_row(dt_bias, DT_LANE), _lane_row(a_log, DT_LANE),
        jnp.asarray(_head_expand_matrix(), BF16), jnp.asarray(_conv_shift_matrix(), BF16),
        row(jnp.repeat(d_skip, SSM_HEAD_DIM)), row(ssm_norm_w))

    fb_t = jnp.broadcast_to(f_bias.astype(F32)[:, None], (GATE_ROWS, CHUNK))
    aux = _gates(gates_raw, fb_t, jnp.asarray(_gate_route_matrix(), BF16))

    gain_t = lambda g: jnp.broadcast_to(jnp.tile(g.astype(F32), 2)[:, None], (LANES, ATT_T))
    o = _attention(qkv, aux, gain_t(g_q), gain_t(g_k))

    x1 = _outproj(y, o, x2d, w_out)

    knt, mem_v = _memkv(mem2d, row(g_mem), xkv_w, row(xg_k))
    x2 = _xattn(x1, row(g_xattn), xq_w.astype(BF16), knt, mem_v, row(xg_q), xo_w.astype(BF16))

    return _mlp(x2, row(g_mlp), w_up, w_down)


def kernel(x, mem, g_mix, w_in, conv_w, conv_b, dt_bias, a_log, d_skip, ssm_norm_w, g_q, g_k,
           f_bias, w_out, g_xattn, g_mem, xq_w, xkv_w, xg_q, xg_k, xo_w, g_mlp, w_up, w_down):
    x2d = x.reshape(TOKENS, D_MODEL)
    mem2d = mem.reshape(BATCH * MEM_LEN, D_MODEL)
    depth = g_mix.shape[0]
    for l in range(depth):
        x2d = _layer(x2d, mem2d, g_mix[l], w_in[l], conv_w[l], conv_b[l], dt_bias[l], a_log[l],
                     d_skip[l], ssm_norm_w[l], g_q[l], g_k[l], f_bias[l], w_out[l], g_xattn[l],
                     g_mem[l], xq_w[l], xkv_w[l], xg_q[l], xg_k[l], xo_w[l], g_mlp[l], w_up[l],
                     w_down[l])
    return x2d.reshape(BATCH, SEQ, D_MODEL)
```
